```python
import math
import jax, jax.numpy as jnp
from jax import lax
import numpy as np

D_MODEL = 2048
BATCH = 4
SEQ = 8192
DEPTH = 4

GRID_W = 64
CTX_LEN = 256
EPS = 1e-6
ROPE_THETA = 10000.0
Q_BLOCK = 128

SSM_HEADS = 32
SSM_HEAD_DIM = 64
SSM_INNER = SSM_HEADS * SSM_HEAD_DIM
SSM_GROUPS = 4
SSM_STATE = 128
SSM_CONV = 5
SSM_CHUNK = 128
SSM_CONV_DIM = SSM_INNER + 2 * SSM_GROUPS * SSM_STATE

GQA_HEADS = 8
GQA_KV_HEADS = 4
GQA_HEAD_DIM = 128
GQA_WIDTH = GQA_HEADS * GQA_HEAD_DIM
GQA_KV_WIDTH = GQA_KV_HEADS * GQA_HEAD_DIM

NA_HEADS = 8
NA_HEAD_DIM = 128
NA_WIDTH = NA_HEADS * NA_HEAD_DIM
NA_WIN_H = 8
NA_WIN_W = 16

MLA_HEADS = 8
MLA_Q_LORA = 768
MLA_KV_LORA = 512
MLA_NOPE = 128
MLA_ROPE = 64
MLA_V = 128
MLA_WIDTH = MLA_HEADS * MLA_V

N_BRANCH = 4

IN_SIZES = (SSM_INNER, SSM_CONV_DIM, 2 * SSM_HEADS,
            GQA_WIDTH, GQA_KV_WIDTH, GQA_KV_WIDTH, GQA_WIDTH,
            NA_WIDTH, NA_WIDTH, NA_WIDTH, NA_WIDTH,
            MLA_Q_LORA, MLA_KV_LORA, MLA_ROPE, MLA_WIDTH,
            N_BRANCH * D_MODEL)
IN_WIDTH = sum(IN_SIZES)

kernel_name = 'hybrid_ssd_gqa_natten_mla_dit'


def rmsnorm(x, w):
    xf = x.astype(jnp.float32)
    y = xf * lax.rsqrt(jnp.mean(xf * xf, axis=-1, keepdims=True) + EPS)
    return (y * w.astype(jnp.float32)).astype(x.dtype)


def heads(t, n):
    return t.reshape(t.shape[:-1] + (n, t.shape[-1] // n))


def group_q(q, g):
    return q.reshape(q.shape[:2] + (g, q.shape[2] // g, q.shape[3]))


def split_cols(p):
    return jnp.split(p, np.cumsum(IN_SIZES)[:-1].tolist(), axis=-1)


def rope_tables(n_tok, dim):
    t = jnp.arange(n_tok, dtype=jnp.int32)
    row = (t // GRID_W).astype(jnp.float32)
    col = (t % GRID_W).astype(jnp.float32)
    quarter = dim // 4
    freqs = ROPE_THETA ** (-jnp.arange(quarter, dtype=jnp.float32) / quarter)
    ang = jnp.concatenate([row[:, None] * freqs, col[:, None] * freqs], axis=-1)
    return jnp.cos(ang), jnp.sin(ang)


def apply_rope(x, cos, sin):
    xp = x.reshape(x.shape[:-1] + (x.shape[-1] // 2, 2))
    x0, x1 = xp[..., 0], xp[..., 1]
    c = cos[:, None, :].astype(x.dtype)
    s = sin[:, None, :].astype(x.dtype)
    return jnp.stack([x0 * c - x1 * s, x0 * s + x1 * c], axis=-1).reshape(x.shape)


def dwconv_silu(u, w, b):
    y = lax.conv_general_dilated(u, w[:, None, :].astype(u.dtype), window_strides=(1,),
                                 padding=[(SSM_CONV // 2, SSM_CONV // 2)],
                                 dimension_numbers=('NWC', 'WIO', 'NWC'),
                                 feature_group_count=u.shape[-1])
    return jax.nn.silu(y + b.astype(u.dtype))


def ssd_scan(xh, dt, a, bm, cm, s0):
    f32 = jnp.float32
    b_, n_tok, n_h, p_dim = xh.shape
    g_n = bm.shape[2]
    r_n = n_h // g_n
    s_dim = bm.shape[-1]
    nc, cl = n_tok // SSM_CHUNK, SSM_CHUNK
    dtf = dt.astype(f32)
    xs = (xh.astype(f32) * dtf[..., None]).reshape(b_, nc, cl, g_n, r_n, p_dim)
    da = (dtf * a.astype(f32)).reshape(b_, nc, cl, g_n, r_n)
    bc = bm.astype(f32).reshape(b_, nc, cl, g_n, s_dim)
    cc = cm.astype(f32).reshape(b_, nc, cl, g_n, s_dim)
    a_cs = jnp.cumsum(da, axis=2)
    lower = jnp.tril(jnp.ones((cl, cl), dtype=bool))[None, None, :, :, None, None]
    seg = a_cs[:, :, :, None] - a_cs[:, :, None, :]
    decay = jnp.exp(jnp.where(lower, seg, -jnp.inf))
    cb = jnp.einsum('bclgn,bcsgn->bclsg', cc, bc)
    y_diag = jnp.einsum('bclsgr,bcsgrp->bclgrp', cb[..., None] * decay, xs)
    to_end = jnp.exp(a_cs[:, :, -1:] - a_cs)
    chunk_states = jnp.einsum('bclgn,bclgrp->bcgrpn', bc, xs * to_end[..., None])
    chunk_decay = jnp.exp(a_cs[:, :, -1])

    def step(state, inp):
        st, dec = inp
        return dec[..., None, None] * state + st, state

    final, s_in = lax.scan(step, s0.astype(f32),
                           (jnp.moveaxis(chunk_states, 1, 0), jnp.moveaxis(chunk_decay, 1, 0)))
    s_in = jnp.moveaxis(s_in, 0, 1)
    y_off = jnp.einsum('bclgn,bcgrpn->bclgrp', cc, s_in) * jnp.exp(a_cs)[..., None]
    return (y_diag + y_off).reshape(b_, n_tok, n_h, p_dim), final


def gated_norm(y, z, w):
    g = y.reshape(z.shape).astype(jnp.float32) * jax.nn.silu(z.astype(jnp.float32))
    gg = g.reshape(z.shape[:-1] + (SSM_GROUPS, -1))
    gg = gg * lax.rsqrt(jnp.mean(gg * gg, axis=-1, keepdims=True) + EPS)
    return (gg.reshape(z.shape) * w.astype(jnp.float32)).astype(z.dtype)


def ssm_branch(z, xbc, dtr, z_c, xbc_c, dtr_c, conv_w, conv_b, a_log, dt_bias, d_skip,
               norm_w, need_ctx):
    def prep(u, dt_raw):
        u = dwconv_silu(u, conv_w, conv_b)
        xs, bm, cm = jnp.split(u, [SSM_INNER, SSM_INNER + SSM_GROUPS * SSM_STATE], axis=-1)
        dt = jax.nn.softplus(dt_raw.astype(jnp.float32).reshape(dt_raw.shape[:-1] + (2, SSM_HEADS))
                             + dt_bias.astype(jnp.float32))
        return heads(xs, SSM_HEADS), heads(bm, SSM_GROUPS), heads(cm, SSM_GROUPS), dt

    xl, bl, cl_, dtl = prep(xbc, dtr)
    xc, bc, cc, dtc = prep(xbc_c, dtr_c)
    a = -jnp.exp(a_log.astype(jnp.float32))
    s0 = jnp.zeros((xl.shape[0], SSM_GROUPS, SSM_HEADS // SSM_GROUPS, SSM_HEAD_DIM, SSM_STATE),
                   jnp.float32)
    flip = lambda t: jnp.flip(t, axis=1)
    yc_f, sc_f = ssd_scan(xc, dtc[:, :, 0], a[0], bc, cc, s0)
    yl_f, _ = ssd_scan(xl, dtl[:, :, 0], a[0], bl, cl_, sc_f)
    yc_b, sc_b = ssd_scan(flip(xc), flip(dtc[:, :, 1]), a[1], flip(bc), flip(cc), s0)
    yl_b, _ = ssd_scan(flip(xl), flip(dtl[:, :, 1]), a[1], flip(bl), flip(cl_), sc_b)
    skip = d_skip.astype(jnp.float32)[:, None]
    out = gated_norm(yl_f + flip(yl_b) + skip * xl.astype(jnp.float32), z, norm_w)
    out_c = None
    if need_ctx:
        out_c = gated_norm(yc_f + flip(yc_b) + skip * xc.astype(jnp.float32), z_c, norm_w)
    return out, out_c


def attend_latent(q, k, v, k_ctx, v_ctx, scale):
    b_, n_tok, g_n, r_n, dq = q.shape
    nb = n_tok // Q_BLOCK
    n_ctx = k_ctx.shape[1]
    qb = jnp.moveaxis(q.reshape(b_, nb, Q_BLOCK, g_n, r_n, dq), 1, 0)

    def block(qi):
        s = jnp.concatenate([jnp.einsum('bqgrd,bkgd->bgrqk', qi, k_ctx),
                             jnp.einsum('bqgrd,bkgd->bgrqk', qi, k)], axis=-1)
        p = jax.nn.softmax(s.astype(jnp.float32) * scale, axis=-1).astype(v.dtype)
        return (jnp.einsum('bgrqk,bkgd->bqgrd', p[..., :n_ctx], v_ctx)
                + jnp.einsum('bgrqk,bkgd->bqgrd', p[..., n_ctx:], v))

    o = lax.map(block, qb)
    return jnp.moveaxis(o, 0, 1).reshape(b_, n_tok, g_n * r_n * v.shape[-1])


def attend_context(q, k, v, scale):
    s = jnp.einsum('bqgrd,bkgd->bgrqk', q, k).astype(jnp.float32) * scale
    p = jax.nn.softmax(s, axis=-1).astype(v.dtype)
    o = jnp.einsum('bgrqk,bkgd->bqgrd', p, v)
    return o.reshape(o.shape[:2] + (-1,))


def na_latent(q, k, v, k_ctx, v_ctx, rpb):
    b_, n_tok, n_h, d = q.shape
    rows = n_tok // GRID_W
    wh = min(NA_WIN_H, rows)
    ww = NA_WIN_W
    n_ctx = k_ctx.shape[1]
    scale = d ** -0.5
    qg = q.reshape(b_, rows, GRID_W, n_h, d)
    kg = k.reshape(b_, rows, GRID_W, n_h, d)
    vg = v.reshape(b_, rows, GRID_W, n_h, d)
    cols = jnp.arange(GRID_W)
    col_idx = jnp.clip(cols - ww // 2, 0, GRID_W - ww)[:, None] + jnp.arange(ww)[None, :]
    rpb_x = rpb[:, :, col_idx - cols[:, None] + (NA_WIN_W - 1)]

    def row(r):
        r0 = jnp.clip(r - wh // 2, 0, rows - wh)
        q_r = lax.dynamic_index_in_dim(qg, r, axis=1, keepdims=False)
        k_win = lax.dynamic_slice_in_dim(kg, r0, wh, axis=1)[:, :, col_idx]
        v_win = lax.dynamic_slice_in_dim(vg, r0, wh, axis=1)[:, :, col_idx]
        bias = rpb_x[:, r0 + jnp.arange(wh) - r + (NA_WIN_H - 1)]
        s_loc = (jnp.einsum('bqhd,byqxhd->bhqyx', q_r, k_win).astype(jnp.float32) * scale
                 + jnp.transpose(bias, (0, 2, 1, 3))[None].astype(jnp.float32))
        s_ctx = jnp.einsum('bqhd,bkhd->bhqk', q_r, k_ctx).astype(jnp.float32) * scale
        s = jnp.concatenate([s_ctx, s_loc.reshape(b_, n_h, GRID_W, wh * ww)], axis=-1)
        p = jax.nn.softmax(s, axis=-1).astype(v.dtype)
        p_loc = p[..., n_ctx:].reshape(b_, n_h, GRID_W, wh, ww)
        return (jnp.einsum('bhqk,bkhd->bqhd', p[..., :n_ctx], v_ctx)
                + jnp.einsum('bhqyx,byqxhd->bqhd', p_loc, v_win))

    o = lax.map(row, jnp.arange(rows))
    return jnp.moveaxis(o, 0, 1).reshape(b_, n_tok, n_h * d)


def mla_q(qa, q_norm, w_uq, rope):
    q = heads(rmsnorm(qa, q_norm) @ w_uq, MLA_HEADS)
    q_nope, q_rope = jnp.split(q, [MLA_NOPE], axis=-1)
    if rope is not None:
        q_rope = apply_rope(q_rope, *rope)
    return jnp.concatenate([q_nope, q_rope], axis=-1)


def mla_kv(kva, kr, kv_norm, w_ukv, rope):
    kv = heads(rmsnorm(kva, kv_norm) @ w_ukv, MLA_HEADS)
    k_nope, v = jnp.split(kv, [MLA_NOPE], axis=-1)
    k_rope = kr[:, :, None, :]
    if rope is not None:
        k_rope = apply_rope(k_rope, *rope)
    k = jnp.concatenate([k_nope, jnp.broadcast_to(k_rope, k_nope.shape[:-1] + (MLA_ROPE,))],
                        axis=-1)
    return k, v


def merge(o_ssm, o_gqa, o_na, o_mla, gate_logits, w_o_ssm, w_o_gqa, w_o_na, w_o_mla, w_out):
    g = jax.nn.sigmoid(gate_logits.astype(jnp.float32)).astype(gate_logits.dtype)
    g_ssm, g_gqa, g_na, g_mla = jnp.split(g, N_BRANCH, axis=-1)
    y = (g_ssm * (o_ssm @ w_o_ssm) + g_gqa * (o_gqa @ w_o_gqa)
         + g_na * (o_na @ w_o_na) + g_mla * (o_mla @ w_o_mla))
    return y @ w_out


def layer(x, ctx, c, c_ctx, rope_g, rope_m, ada_w, ada_b, norm_pre, norm_post, w_in,
          conv_w, conv_b, a_log, dt_bias, d_skip, ssm_norm, w_o_ssm,
          gqa_q_norm, gqa_k_norm, w_o_gqa, na_rpb, w_o_na,
          mla_q_norm, w_uq, mla_kv_norm, w_ukv, w_o_mla, w_out, need_ctx):
    shift, scale, gate = jnp.split((jax.nn.silu(c) @ ada_w + ada_b)[:, None, :], 3, axis=-1)
    shift_c, scale_c, gate_c = jnp.split(jax.nn.silu(c_ctx) @ ada_w + ada_b, 3, axis=-1)
    h = rmsnorm(x, norm_pre) * (1 + scale) + shift
    h_c = rmsnorm(ctx, norm_pre) * (1 + scale_c) + shift_c
    (z, xbc, dtr, gq, gk, gv, gg, nq, nk, nv, ng,
     mqa, mkva, mkr, mg, mix) = split_cols(h @ w_in)
    (z_c, xbc_c, dtr_c, gq_c, gk_c, gv_c, gg_c, nq_c, nk_c, nv_c, ng_c,
     mqa_c, mkva_c, mkr_c, mg_c, mix_c) = split_cols(h_c @ w_in)

    o_ssm, o_ssm_c = ssm_branch(z, xbc, dtr, z_c, xbc_c, dtr_c, conv_w, conv_b, a_log, dt_bias,
                                d_skip, ssm_norm, need_ctx)

    gqa_scale = GQA_HEAD_DIM ** -0.5
    q = apply_rope(rmsnorm(heads(gq, GQA_HEADS), gqa_q_norm), *rope_g)
    k = apply_rope(rmsnorm(heads(gk, GQA_KV_HEADS), gqa_k_norm), *rope_g)
    v = heads(gv, GQA_KV_HEADS)
    k_c = rmsnorm(heads(gk_c, GQA_KV_HEADS), gqa_k_norm)
    v_c = heads(gv_c, GQA_KV_HEADS)
    o_gqa = attend_latent(group_q(q, GQA_KV_HEADS), k, v, k_c, v_c, gqa_scale) * jax.nn.silu(gg)

    nk_ch, nv_ch = heads(nk_c, NA_HEADS), heads(nv_c, NA_HEADS)
    o_na = na_latent(heads(nq, NA_HEADS), heads(nk, NA_HEADS), heads(nv, NA_HEADS),
                     nk_ch, nv_ch, na_rpb) * jax.nn.silu(ng)

    mla_scale = (MLA_NOPE + MLA_ROPE) ** -0.5
    q_m = mla_q(mqa, mla_q_norm, w_uq, rope_m)
    k_m, v_m = mla_kv(mkva, mkr, mla_kv_norm, w_ukv, rope_m)
    k_mc, v_mc = mla_kv(mkva_c, mkr_c, mla_kv_norm, w_ukv, None)
    o_mla = attend_latent(group_q(q_m, MLA_HEADS), k_m, v_m, k_mc, v_mc, mla_scale) * jax.nn.silu(mg)

    y = merge(o_ssm, o_gqa, o_na, o_mla, mix, w_o_ssm, w_o_gqa, w_o_na, w_o_mla, w_out)
    x = x + gate * rmsnorm(y, norm_post)

    if need_ctx:
        q_c = rmsnorm(heads(gq_c, GQA_HEADS), gqa_q_norm)
        o_gqa_c = attend_context(group_q(q_c, GQA_KV_HEADS), k_c, v_c, gqa_scale) * jax.nn.silu(gg_c)
        o_na_c = attend_context(group_q(heads(nq_c, NA_HEADS), NA_HEADS), nk_ch, nv_ch,
                                NA_HEAD_DIM ** -0.5) * jax.nn.silu(ng_c)
        q_mc = mla_q(mqa_c, mla_q_norm, w_uq, None)
        o_mla_c = attend_context(group_q(q_mc, MLA_HEADS), k_mc, v_mc, mla_scale) * jax.nn.silu(mg_c)
        y_c = merge(o_ssm_c, o_gqa_c, o_na_c, o_mla_c, mix_c, w_o_ssm, w_o_gqa, w_o_na, w_o_mla, w_out)
        ctx = ctx + gate_c * rmsnorm(y_c, norm_post)
    return x, ctx


def setup_inputs(seed: int = 0) -> dict:
    key = jax.random.key(seed)
    ks = jax.random.split(key, 32)
    f32 = jnp.float32
    n_l = DEPTH

    def dense(k, shape, fan_in):
        return jax.random.normal(k, shape, f32) * fan_in ** -0.5

    def gain(k, shape):
        return 1.0 + 0.05 * jax.random.normal(k, shape, f32)

    dt0 = jnp.exp(jax.random.uniform(ks[10], (n_l, 2, SSM_HEADS), f32,
                                     math.log(1e-3), math.log(1e-1)))
    return {
        'x': jax.random.normal(ks[0], (BATCH, SEQ, D_MODEL), f32),
        'c': jax.random.normal(ks[1], (BATCH, D_MODEL), f32),
        'ctx': jax.random.normal(ks[2], (BATCH, CTX_LEN, D_MODEL), f32),
        'c_ctx': jax.random.normal(ks[3], (D_MODEL,), f32),
        'ada_w': dense(ks[4], (n_l, D_MODEL, 3 * D_MODEL), D_MODEL),
        'ada_b': 0.02 * jax.random.normal(ks[5], (n_l, 3 * D_MODEL), f32),
        'norm_pre': gain(ks[6], (n_l, D_MODEL)),
        'norm_post': gain(ks[7], (n_l, D_MODEL)),
        'w_in': dense(ks[8], (n_l, D_MODEL, IN_WIDTH), D_MODEL),
        'conv_w': dense(ks[9], (n_l, SSM_CONV, SSM_CONV_DIM), SSM_CONV),
        'conv_b': 0.02 * jax.random.normal(ks[11], (n_l, SSM_CONV_DIM), f32),
        'a_log': jnp.log(jax.random.uniform(ks[12], (n_l, 2, SSM_HEADS), f32, 1.0, 16.0)),
        'dt_bias': dt0 + jnp.log(-jnp.expm1(-dt0)),
        'd_skip': 1.0 + 0.1 * jax.random.normal(ks[13], (n_l, SSM_HEADS), f32),
        'ssm_norm': gain(ks[14], (n_l, SSM_INNER)),
        'w_o_ssm': dense(ks[15], (n_l, SSM_INNER, D_MODEL), SSM_INNER),
        'gqa_q_norm': gain(ks[16], (n_l, GQA_HEAD_DIM)),
        'gqa_k_norm': gain(ks[17], (n_l, GQA_HEAD_DIM)),
        'w_o_gqa': dense(ks[18], (n_l, GQA_WIDTH, D_MODEL), GQA_WIDTH),
        'na_rpb': 0.1 * jax.random.normal(ks[19], (n_l, NA_HEADS, 2 * NA_WIN_H - 1, 2 * NA_WIN_W - 1), f32),
        'w_o_na': dense(ks[20], (n_l, NA_WIDTH, D_MODEL), NA_WIDTH),
        'mla_q_norm': gain(ks[21], (n_l, MLA_Q_LORA)),
        'w_uq': dense(ks[22], (n_l, MLA_Q_LORA, MLA_HEADS * (MLA_NOPE + MLA_ROPE)), MLA_Q_LORA),
        'mla_kv_norm': gain(ks[23], (n_l, MLA_KV_LORA)),
        'w_ukv': dense(ks[24], (n_l, MLA_KV_LORA, MLA_HEADS * (MLA_NOPE + MLA_V)), MLA_KV_LORA),
        'w_o_mla': dense(ks[25], (n_l, MLA_WIDTH, D_MODEL), MLA_WIDTH),
        'w_out': dense(ks[26], (n_l, D_MODEL, D_MODEL), D_MODEL),
    }


def reference(x, c, ctx, c_ctx, ada_w, ada_b, norm_pre, norm_post, w_in, conv_w, conv_b,
              a_log, dt_bias, d_skip, ssm_norm, w_o_ssm, gqa_q_norm, gqa_k_norm, w_o_gqa,
              na_rpb, w_o_na, mla_q_norm, w_uq, mla_kv_norm, w_ukv, w_o_mla, w_out):
    n_tok = x.shape[1]
    rope_g = rope_tables(n_tok, GQA_HEAD_DIM)
    rope_m = rope_tables(n_tok, MLA_ROPE)
    for l in range(DEPTH):
        x, ctx = layer(x, ctx, c, c_ctx, rope_g, rope_m, ada_w[l], ada_b[l], norm_pre[l],
                       norm_post[l], w_in[l], conv_w[l], conv_b[l], a_log[l], dt_bias[l],
                       d_skip[l], ssm_norm[l], w_o_ssm[l], gqa_q_norm[l], gqa_k_norm[l],
                       w_o_gqa[l], na_rpb[l], w_o_na[l], mla_q_norm[l], w_uq[l],
                       mla_kv_norm[l], w_ukv[l], w_o_mla[l], w_out[l],
                       need_ctx=l < DEPTH - 1)
    return x
```

```python
import functools
import math

import jax
import jax.numpy as jnp
import numpy as np
from jax import lax
from jax.experimental import pallas as pl
from jax.experimental.pallas import tpu as pltpu

F32 = jnp.float32
BF16 = jnp.bfloat16

GRID_W = 64
EPS = 1e-6
ROPE_THETA = 10000.0

SSM_HEADS = 32
SSM_HEAD_DIM = 64
SSM_INNER = SSM_HEADS * SSM_HEAD_DIM
SSM_GROUPS = 4
SSM_STATE = 128
SSM_CONV = 5
SSM_CHUNK = 128
SSM_CONV_DIM = SSM_INNER + 2 * SSM_GROUPS * SSM_STATE

GQA_HEADS = 8
GQA_KV_HEADS = 4
HEAD_DIM = 128
NA_HEADS = 8
NA_WIN_H = 8
NA_WIN_W = 16
NA_QROWS = 4
NA_KROWS = NA_QROWS + NA_WIN_H

MLA_HEADS = 8
MLA_Q_LORA = 768
MLA_KV_LORA = 512
MLA_NOPE = 128
MLA_ROPE = 64
MLA_QK_PAD = 256

N_BRANCH = 4
LANE = 128
VMEM_LIMIT = 56 * 1024 * 1024
NEG_BIG = -1e30

_SIZES = (SSM_INNER, SSM_CONV_DIM, 2 * SSM_HEADS, 1024, 512, 512, 1024, 1024, 1024, 1024, 1024,
          MLA_Q_LORA, MLA_KV_LORA, MLA_ROPE, 1024, N_BRANCH * 2048)
_OFF = dict(zip(("z", "xbc", "dtr", "gq", "gk", "gv", "gg", "nq", "nk", "nv", "ng",
                 "mqa", "mkva", "mkr", "mg", "mix"), np.cumsum((0,) + _SIZES[:-1]).tolist()))

P_MIX, P_Z, P_XBC = 0, 8192, 10240
P_GQ, P_GG, P_NQ, P_NK, P_NV, P_NG, P_MG, P_MQA = (13312, 14336, 15360, 16384, 17408, 18432,
                                                    19456, 20480)
P_GK, P_GV, P_MKVA, P_MKR = 21504, 22016, 22528, 23040
P_WIDTH = 23552
MQA_PAD = 1024


def _pick(n, candidates):
    for c in candidates:
        if n % c == 0:
            return c
    raise ValueError(f"no tile for {n} among {candidates}")


def _params(sem):
    return pltpu.CompilerParams(dimension_semantics=sem, vmem_limit_bytes=VMEM_LIMIT)


def _silu(v):
    return v * (1.0 / (1.0 + jnp.exp(-v)))


def _rope(v, cos, sin):
    return v * cos + pltpu.roll(v, 64, axis=1) * sin


def _mod_kernel(c_ref, w_ref, b_ref, o_ref):
    h = _silu(c_ref[...]).astype(BF16)
    o_ref[...] = jnp.dot(h, w_ref[...].astype(BF16), preferred_element_type=F32) + b_ref[...]


def _modulation(cc, ada_w, ada_b):
    rows, d = cc.shape
    n = ada_w.shape[1]
    tn = _pick(n, (512, 256, 128))
    return pl.pallas_call(
        _mod_kernel,
        grid=(n // tn,),
        in_specs=[pl.BlockSpec((rows, d), lambda j: (0, 0)),
                  pl.BlockSpec((d, tn), lambda j: (0, j)),
                  pl.BlockSpec((1, tn), lambda j: (0, j))],
        out_specs=pl.BlockSpec((rows, tn), lambda j: (0, j)),
        out_shape=jax.ShapeDtypeStruct((rows, n), F32),
        compiler_params=_params(("parallel",)),
        name="adaln_mod",
    )(cc, ada_w, ada_b.reshape(1, n))


def _is_ctx_rows(i, tm, tiles_per_batch, n_lat):
    row = (i % tiles_per_batch) * tm + lax.broadcasted_iota(jnp.int32, (tm, 1), 0)
    return row >= n_lat


def _inproj_kernel(x_ref, nw_ref, sc_ref, sh_ref, w_ref, ws_ref, o_ref, os_ref, h_ref, *,
                   tm, tiles_per_batch, n_lat):
    i = pl.program_id(0)

    @pl.when(pl.program_id(1) == 0)
    def _():
        x = x_ref[...]
        y = x * lax.rsqrt(jnp.mean(x * x, axis=-1, keepdims=True) + EPS) * nw_ref[...]
        is_ctx = _is_ctx_rows(i, tm, tiles_per_batch, n_lat)
        sc = jnp.where(is_ctx, sc_ref[1:2, :], sc_ref[0:1, :])
        sh = jnp.where(is_ctx, sh_ref[1:2, :], sh_ref[0:1, :])
        h = (y * (1.0 + sc) + sh).astype(BF16)
        h_ref[...] = h
        os_ref[...] = jnp.dot(h, ws_ref[...], preferred_element_type=F32)

    o_ref[...] = jnp.dot(h_ref[...], w_ref[...], preferred_element_type=F32).astype(o_ref.dtype)


def _in_projection(xa, norm_w, scale, shift, w_p, w_side, n_lat, seq):
    t, d = xa.shape
    tm = _pick(seq, (768, 512, 256))
    tn = 512
    tpb = seq // tm
    kern = functools.partial(_inproj_kernel, tm=tm, tiles_per_batch=tpb, n_lat=n_lat)
    return pl.pallas_call(
        kern,
        grid=(t // tm, P_WIDTH // tn),
        in_specs=[pl.BlockSpec((tm, d), lambda i, j: (i, 0)),
                  pl.BlockSpec((1, d), lambda i, j: (0, 0)),
                  pl.BlockSpec((None, 2, d), lambda i, j: (i // tpb, 0, 0)),
                  pl.BlockSpec((None, 2, d), lambda i, j: (i // tpb, 0, 0)),
                  pl.BlockSpec((d, tn), lambda i, j: (0, j)),
                  pl.BlockSpec((d, 2 * LANE), lambda i, j: (0, 0))],
        out_specs=[pl.BlockSpec((tm, tn), lambda i, j: (i, j)),
                   pl.BlockSpec((tm, 2 * LANE), lambda i, j: (i, 0))],
        out_shape=[jax.ShapeDtypeStruct((t, P_WIDTH), BF16),
                   jax.ShapeDtypeStruct((t, 2 * LANE), F32)],
        scratch_shapes=[pltpu.VMEM((tm, d), BF16)],
        compiler_params=_params(("parallel", "arbitrary")),
        name="in_proj",
    )(xa, norm_w.reshape(1, d), scale, shift, w_p, w_side)


def _conv_kernel(u_ref, prev_ref, next_ref, w_ref, b_ref, o_ref, ext_ref, *,
                 tm, tiles_per_batch, lat_tiles):
    ib = pl.program_id(0) % tiles_per_batch
    first = jnp.logical_or(ib == 0, ib == lat_tiles)
    last = jnp.logical_or(ib == lat_tiles - 1, ib == tiles_per_batch - 1)
    ext_ref[0:8, :] = jnp.where(first, 0.0, prev_ref[...].astype(F32))
    ext_ref[8:8 + tm, :] = u_ref[...].astype(F32)
    ext_ref[8 + tm:16 + tm, :] = jnp.where(last, 0.0, next_ref[...].astype(F32))
    acc = b_ref[...] + w_ref[0:1, :] * ext_ref[pl.ds(8 - SSM_CONV // 2, tm), :]
    for k in range(1, SSM_CONV):
        acc = acc + w_ref[k:k + 1, :] * ext_ref[pl.ds(8 - SSM_CONV // 2 + k, tm), :]
    o_ref[...] = _silu(acc).astype(o_ref.dtype)


def _conv_silu(p, conv_w, conv_b, n_lat, seq):
    t = p.shape[0]
    tm = 256
    tc = 1024
    tpb = seq // tm
    n_row8 = t // 8
    col0 = P_XBC // tc
    kern = functools.partial(_conv_kernel, tm=tm, tiles_per_batch=tpb, lat_tiles=n_lat // tm)
    return pl.pallas_call(
        kern,
        grid=(t // tm, SSM_CONV_DIM // tc),
        in_specs=[pl.BlockSpec((tm, tc), lambda i, j: (i, col0 + j)),
                  pl.BlockSpec((8, tc), lambda i, j: (jnp.maximum(i * (tm // 8) - 1, 0), col0 + j)),
                  pl.BlockSpec((8, tc), lambda i, j: (jnp.minimum((i + 1) * (tm // 8), n_row8 - 1),
                                                      col0 + j)),
                  pl.BlockSpec((8, tc), lambda i, j: (0, j)),
                  pl.BlockSpec((1, tc), lambda i, j: (0, j))],
        out_specs=pl.BlockSpec((tm, tc), lambda i, j: (i, j)),
        out_shape=jax.ShapeDtypeStruct((t, SSM_CONV_DIM), BF16),
        scratch_shapes=[pltpu.VMEM((tm + 16, tc), F32)],
        compiler_params=_params(("parallel", "parallel")),
        name="ssm_conv",
    )(p, p, p, jnp.pad(conv_w, ((0, 8 - SSM_CONV), (0, 0))), conv_b.reshape(1, -1))


def _split_dot(a_bf16, v):
    v1 = v.astype(BF16)
    r1 = v - v1.astype(F32)
    v2 = r1.astype(BF16)
    v3 = (r1 - v2.astype(F32)).astype(BF16)
    dot = functools.partial(jnp.dot, preferred_element_type=F32)
    return dot(a_bf16, v1) + dot(a_bf16, v2) + dot(a_bf16, v3)


def _ssd_kernel(u_ref, dtr_ref, alog_ref, bias_ref, eh_ref, y_ref, st_ref):
    cl = SSM_CHUNK
    fwd = pl.program_id(1) == 0

    @pl.when(pl.program_id(2) == 0)
    def _():
        st_ref[...] = jnp.zeros_like(st_ref)

    raw = dtr_ref[...] + bias_ref[...]
    dt = jnp.maximum(raw, 0.0) + jnp.log(1.0 + jnp.exp(-jnp.abs(raw)))
    da = dt * (-jnp.exp(alog_ref[...]))
    r = lax.broadcasted_iota(jnp.int32, (cl, cl), 0)
    c = lax.broadcasted_iota(jnp.int32, (cl, cl), 1)
    tri = jnp.where(fwd, r - c, c - r) >= 0
    acs = _split_dot(jnp.where(tri, 1.0, 0.0).astype(BF16), da)
    total = jnp.where(fwd, acs[cl - 1:cl, :], acs[0:1, :])
    e_acs = jnp.exp(acs)
    w_end = dt * jnp.exp(total - acs)
    acs_t = acs.T
    dt_t = dt.T

    def hi_lo(v):
        hi = v.astype(BF16)
        return hi, (v - hi.astype(F32)).astype(BF16)

    stack = jnp.concatenate(hi_lo(w_end) + hi_lo(e_acs), axis=0)
    ex = jnp.dot(stack, eh_ref[...], preferred_element_type=F32)
    w_exp = ex[0:cl] + ex[cl:2 * cl]
    e_exp = ex[2 * cl:3 * cl] + ex[3 * cl:4 * cl]
    dec = jnp.where(fwd, e_exp[cl - 1:cl, :], e_exp[0:1, :])
    xw = (u_ref[:, 0:SSM_INNER].astype(F32) * w_exp).astype(BF16)
    lane = lax.broadcasted_iota(jnp.int32, (cl, LANE), 1)
    gw = SSM_INNER // SSM_GROUPS
    hpg = SSM_HEADS // SSM_GROUPS
    for g in range(SSM_GROUPS):
        b_g = u_ref[:, SSM_INNER + g * SSM_STATE:SSM_INNER + (g + 1) * SSM_STATE]
        c_off = SSM_INNER + SSM_GROUPS * SSM_STATE
        c_g = u_ref[:, c_off + g * SSM_STATE:c_off + (g + 1) * SSM_STATE]
        cb = lax.dot_general(c_g, b_g, (((1,), (1,)), ((), ())), preferred_element_type=F32)
        st_g = st_ref[:, g * gw:(g + 1) * gw]
        y_off = jnp.dot(c_g, st_g.astype(BF16), preferred_element_type=F32)
        ys = []
        for k in range(hpg // 2):
            ms = []
            for h in (g * hpg + 2 * k, g * hpg + 2 * k + 1):
                seg = acs[:, h:h + 1] - acs_t[h:h + 1, :]
                dec_h = jnp.exp(jnp.where(tri, seg, NEG_BIG))
                ms.append((dec_h * cb * dt_t[h:h + 1, :]).astype(BF16))
            x_p = u_ref[:, g * gw + k * LANE:g * gw + (k + 1) * LANE]
            zero = jnp.zeros_like(x_p)
            rhs = jnp.concatenate([jnp.where(lane < SSM_HEAD_DIM, x_p, zero),
                                   jnp.where(lane >= SSM_HEAD_DIM, x_p, zero)], axis=0)
            ys.append(jnp.dot(jnp.concatenate(ms, axis=1), rhs, preferred_element_type=F32))
        y_g = jnp.concatenate(ys, axis=1) + y_off * e_exp[:, g * gw:(g + 1) * gw]
        y_ref[:, g * gw:(g + 1) * gw] = y_g.astype(y_ref.dtype)
        upd = lax.dot_general(b_g, xw[:, g * gw:(g + 1) * gw], (((0,), (0,)), ((), ())),
                              preferred_element_type=F32)
        st_ref[:, g * gw:(g + 1) * gw] = st_g * dec[:, g * gw:(g + 1) * gw] + upd


def _ssd(u, dtr, a_log, dt_bias, batch, n_lat, seq):
    cl = SSM_CHUNK
    nch, nlat, nctx = seq // cl, n_lat // cl, (seq - n_lat) // cl

    def chunk(d, c):
        f = jnp.where(c < nctx, nlat + c, c - nctx)
        b = jnp.where(c < nctx, nlat + nctx - 1 - c, nlat - 1 - (c - nctx))
        return jnp.where(d == 0, f, b)

    def pad_heads(v):
        return jnp.pad(v, ((0, 0), (0, LANE - SSM_HEADS))).reshape(2, 1, LANE)

    eh = np.zeros((LANE, SSM_INNER), np.float32)
    for h in range(SSM_HEADS):
        eh[h, h * SSM_HEAD_DIM:(h + 1) * SSM_HEAD_DIM] = 1.0
    return pl.pallas_call(
        _ssd_kernel,
        grid=(batch, 2, nch),
        in_specs=[pl.BlockSpec((cl, SSM_CONV_DIM), lambda b, d, c: (b * nch + chunk(d, c), 0)),
                  pl.BlockSpec((cl, LANE), lambda b, d, c: (b * nch + chunk(d, c), d)),
                  pl.BlockSpec((None, 1, LANE), lambda b, d, c: (d, 0, 0)),
                  pl.BlockSpec((None, 1, LANE), lambda b, d, c: (d, 0, 0)),
                  pl.BlockSpec((LANE, SSM_INNER), lambda b, d, c: (0, 0))],
        out_specs=pl.BlockSpec((None, cl, SSM_INNER), lambda b, d, c: (d, b * nch + chunk(d, c), 0)),
        out_shape=jax.ShapeDtypeStruct((2, u.shape[0], SSM_INNER), BF16),
        scratch_shapes=[pltpu.VMEM((SSM_STATE, SSM_INNER), F32)],
        compiler_params=_params(("parallel", "parallel", "arbitrary")),
        name="ssd_scan",
    )(u, dtr, pad_heads(a_log), pad_heads(dt_bias), jnp.asarray(eh, BF16))


def _ssm_out_kernel(yf_ref, yb_ref, x_ref, z_ref, skip_ref, nw_ref, o_ref):
    z = z_ref[...].astype(F32)
    g = (yf_ref[...].astype(F32) + yb_ref[...].astype(F32)
         + skip_ref[...] * x_ref[...].astype(F32)) * _silu(z)
    gw = SSM_INNER // SSM_GROUPS
    for k in range(SSM_GROUPS):
        gk = g[:, k * gw:(k + 1) * gw]
        gk = gk * lax.rsqrt(jnp.mean(gk * gk, axis=-1, keepdims=True) + EPS)
        o_ref[:, k * gw:(k + 1) * gw] = (gk * nw_ref[:, k * gw:(k + 1) * gw]).astype(o_ref.dtype)


def _ssm_out(y, u, p, d_skip, ssm_norm, seq):
    t = u.shape[0]
    tm = _pick(seq, (384, 256, 128))
    w = SSM_INNER
    return pl.pallas_call(
        _ssm_out_kernel,
        grid=(t // tm,),
        in_specs=[pl.BlockSpec((None, tm, w), lambda i: (0, i, 0)),
                  pl.BlockSpec((None, tm, w), lambda i: (1, i, 0)),
                  pl.BlockSpec((tm, w), lambda i: (i, 0)),
                  pl.BlockSpec((tm, w), lambda i: (i, P_Z // w)),
                  pl.BlockSpec((1, w), lambda i: (0, 0)),
                  pl.BlockSpec((1, w), lambda i: (0, 0))],
        out_specs=pl.BlockSpec((tm, w), lambda i: (i, 0)),
        out_shape=jax.ShapeDtypeStruct((t, w), BF16),
        compiler_params=_params(("parallel",)),
        name="ssm_gated_norm",
    )(y, y, u, p, jnp.repeat(d_skip, SSM_HEAD_DIM).reshape(1, w), ssm_norm.reshape(1, w))


def _head_norm_rope_kernel(x_ref, nw_ref, cos_ref, sin_ref, o_ref):
    x = x_ref[...].astype(F32)
    y = x * lax.rsqrt(jnp.mean(x * x, axis=-1, keepdims=True) + EPS) * nw_ref[...]
    o_ref[...] = _rope(y, cos_ref[...], sin_ref[...]).astype(o_ref.dtype)


def _head_norm_rope(p, col0, n_heads, norm_w, cos, sin, seq):
    t = p.shape[0]
    tm = _pick(seq, (768, 512, 256))
    tpb = seq // tm
    return pl.pallas_call(
        _head_norm_rope_kernel,
        grid=(t // tm, n_heads),
        in_specs=[pl.BlockSpec((tm, LANE), lambda i, h: (i, col0 // LANE + h)),
                  pl.BlockSpec((1, LANE), lambda i, h: (0, 0)),
                  pl.BlockSpec((tm, LANE), lambda i, h: (i % tpb, 0)),
                  pl.BlockSpec((tm, LANE), lambda i, h: (i % tpb, 0))],
        out_specs=pl.BlockSpec((tm, LANE), lambda i, h: (i, h)),
        out_shape=jax.ShapeDtypeStruct((t, n_heads * LANE), BF16),
        compiler_params=_params(("parallel", "parallel")),
        name="head_norm_rope",
    )(p, norm_w.reshape(1, LANE), cos, sin)


def _mla_q_kernel(x_ref, nw_ref, w_ref, cos_ref, sin_ref, o_ref, h_ref):
    @pl.when(pl.program_id(1) == 0)
    def _():
        x = x_ref[...].astype(F32)
        ms = jnp.sum(x * x, axis=-1, keepdims=True) * (1.0 / MLA_Q_LORA)
        h_ref[...] = (x * lax.rsqrt(ms + EPS) * nw_ref[...]).astype(BF16)

    res = jnp.dot(h_ref[...], w_ref[...], preferred_element_type=F32)
    o_ref[:, 0:LANE] = res[:, 0:LANE].astype(o_ref.dtype)
    o_ref[:, LANE:] = _rope(res[:, LANE:], cos_ref[...], sin_ref[...]).astype(o_ref.dtype)


def _mla_q(p, q_norm, w_uq, cos, sin, seq):
    t = p.shape[0]
    tm = _pick(seq, (768, 512, 256))
    tpb = seq // tm
    kdim = MQA_PAD
    return pl.pallas_call(
        _mla_q_kernel,
        grid=(t // tm, MLA_HEADS),
        in_specs=[pl.BlockSpec((tm, kdim), lambda i, h: (i, P_MQA // kdim)),
                  pl.BlockSpec((1, kdim), lambda i, h: (0, 0)),
                  pl.BlockSpec((kdim, MLA_QK_PAD), lambda i, h: (0, h)),
                  pl.BlockSpec((tm, LANE), lambda i, h: (i % tpb, 0)),
                  pl.BlockSpec((tm, LANE), lambda i, h: (i % tpb, 0))],
        out_specs=pl.BlockSpec((tm, MLA_QK_PAD), lambda i, h: (i, h)),
        out_shape=jax.ShapeDtypeStruct((t, MLA_HEADS * MLA_QK_PAD), BF16),
        scratch_shapes=[pltpu.VMEM((tm, kdim), BF16)],
        compiler_params=_params(("parallel", "arbitrary")),
        name="mla_q_up",
    )(p, q_norm, w_uq, cos, sin)


def _mla_kv_kernel(x_ref, kr_ref, nw_ref, w_ref, cos_ref, sin_ref, k_ref, v_ref, h_ref):
    @pl.when(pl.program_id(1) == 0)
    def _():
        x = x_ref[...].astype(F32)
        h_ref[...] = (x * lax.rsqrt(jnp.mean(x * x, axis=-1, keepdims=True) + EPS)
                      * nw_ref[...]).astype(BF16)

    res = jnp.dot(h_ref[...], w_ref[...], preferred_element_type=F32)
    k_ref[:, 0:LANE] = res[:, 0:LANE].astype(k_ref.dtype)
    k_ref[:, LANE:] = _rope(kr_ref[...].astype(F32), cos_ref[...], sin_ref[...]).astype(k_ref.dtype)
    v_ref[...] = res[:, LANE:].astype(v_ref.dtype)


def _mla_kv(p, kv_norm, w_ukv, cos, sin, seq):
    t = p.shape[0]
    tm = _pick(seq, (768, 512, 256))
    tpb = seq // tm
    kdim = MLA_KV_LORA
    return pl.pallas_call(
        _mla_kv_kernel,
        grid=(t // tm, MLA_HEADS),
        in_specs=[pl.BlockSpec((tm, kdim), lambda i, h: (i, P_MKVA // kdim)),
                  pl.BlockSpec((tm, LANE), lambda i, h: (i, P_MKR // LANE)),
                  pl.BlockSpec((1, kdim), lambda i, h: (0, 0)),
                  pl.BlockSpec((kdim, 2 * LANE), lambda i, h: (0, h)),
                  pl.BlockSpec((tm, LANE), lambda i, h: (i % tpb, 0)),
                  pl.BlockSpec((tm, LANE), lambda i, h: (i % tpb, 0))],
        out_specs=[pl.BlockSpec((tm, MLA_QK_PAD), lambda i, h: (i, h)),
                   pl.BlockSpec((tm, LANE), lambda i, h: (i, h))],
        out_shape=[jax.ShapeDtypeStruct((t, MLA_HEADS * MLA_QK_PAD), BF16),
                   jax.ShapeDtypeStruct((t, MLA_HEADS * LANE), BF16)],
        scratch_shapes=[pltpu.VMEM((tm, kdim), BF16)],
        compiler_params=_params(("parallel", "arbitrary")),
        name="mla_kv_up",
    )(p, p, kv_norm.reshape(1, kdim), w_ukv, cos, sin)


def _flash_kernel(q_ref, k_ref, v_ref, g_ref, o_ref, *, r, tq, dq, tk, n_chunks, tail, scale):
    q = jnp.concatenate([q_ref[:, j * dq:(j + 1) * dq] for j in range(r)], axis=0)
    q = (q.astype(F32) * scale).astype(BF16)
    m_rows = r * tq

    def step(carry, k_c, v_c):
        m, l, acc = carry
        s = lax.dot_general(q, k_c, (((1,), (1,)), ((), ())), preferred_element_type=F32)
        m_new = jnp.maximum(m, jnp.max(s, axis=-1, keepdims=True))
        alpha = jnp.exp(m - m_new)
        pr = jnp.exp(s - m_new)
        l = alpha * l + jnp.sum(pr, axis=-1, keepdims=True)
        acc = alpha * acc + jnp.dot(pr.astype(BF16), v_c, preferred_element_type=F32)
        return m_new, l, acc

    carry = (jnp.full((m_rows, 1), NEG_BIG, F32), jnp.zeros((m_rows, 1), F32),
             jnp.zeros((m_rows, LANE), F32))
    if n_chunks:
        def body(c, carry):
            start = pl.multiple_of(c * tk, tk)
            return step(carry, k_ref[pl.ds(start, tk), :], v_ref[pl.ds(start, tk), :])

        carry = lax.fori_loop(0, n_chunks, body, carry)
    if tail:
        lo = n_chunks * tk
        carry = step(carry, k_ref[lo:lo + tail, :], v_ref[lo:lo + tail, :])
    _, l, acc = carry
    o = acc / l
    for j in range(r):
        gate = _silu(g_ref[:, j * LANE:(j + 1) * LANE].astype(F32))
        o_ref[:, j * LANE:(j + 1) * LANE] = (o[j * tq:(j + 1) * tq] * gate).astype(o_ref.dtype)


def _flash(q, k, v, gates, *, q_col0, k_col0, v_col0, g_col0, v_stride, n_kv_heads, r, dq, scale,
           batch, n_lat, seq, ctx_only, tq):
    n_ctx = seq - n_lat
    tk = 512
    if ctx_only:
        tq = n_ctx
        q_blk0, n_q, kv_rows, kv_blk0 = n_lat // tq, 1, n_ctx, n_lat // n_ctx
        n_chunks, tail = 0, n_ctx
    else:
        q_blk0, n_q, kv_rows, kv_blk0 = 0, n_lat // tq, seq, 0
        n_chunks, tail = n_lat // tk, n_ctx
    kern = functools.partial(_flash_kernel, r=r, tq=tq, dq=dq, tk=tk, n_chunks=n_chunks,
                             tail=tail, scale=scale)
    w_out = n_kv_heads * r * LANE
    out = pl.pallas_call(
        kern,
        grid=(batch, n_kv_heads, n_q),
        in_specs=[pl.BlockSpec((None, tq, r * dq), lambda b, g, i: (b, q_blk0 + i, q_col0 // (r * dq) + g)),
                  pl.BlockSpec((None, kv_rows, dq), lambda b, g, i: (b, kv_blk0, k_col0 // dq + g)),
                  pl.BlockSpec((None, kv_rows, LANE),
                               lambda b, g, i: (b, kv_blk0, v_col0 // LANE + v_stride * g)),
                  pl.BlockSpec((None, tq, r * LANE),
                               lambda b, g, i: (b, q_blk0 + i, g_col0 // (r * LANE) + g))],
        out_specs=pl.BlockSpec((None, tq, r * LANE), lambda b, g, i: (b, q_blk0 + i, g)),
        out_shape=jax.ShapeDtypeStruct((batch, seq, w_out), BF16),
        compiler_params=_params(("parallel", "parallel", "parallel")),
        name="flash_ctx" if ctx_only else "flash_latent",
    )(*(a.reshape(batch, seq, a.shape[-1]) for a in (q, k, v, gates)))
    return out.reshape(batch * seq, w_out)


def _attention(q, k, v, gates, tq, **kw):
    lat = _flash(q, k, v, gates, ctx_only=False, tq=tq, **kw)
    ctx = _flash(q, k, v, gates, ctx_only=True, tq=tq, **kw)
    return lat, ctx


def _natten_kernel(q_ref, k0_ref, k1_ref, k2_ref, kc_ref, v0_ref, v1_ref, v2_ref, vc_ref, g_ref,
                   bias_ref, o_ref, *, scale):
    q = (q_ref[...].astype(F32) * scale).astype(BF16)
    k = jnp.concatenate([k0_ref[...], k1_ref[...], k2_ref[...], kc_ref[...]], axis=0)
    v = jnp.concatenate([v0_ref[...], v1_ref[...], v2_ref[...], vc_ref[...]], axis=0)
    s = lax.dot_general(q, k, (((1,), (1,)), ((), ())), preferred_element_type=F32) + bias_ref[...]
    pr = jnp.exp(s - jnp.max(s, axis=-1, keepdims=True))
    l = jnp.sum(pr, axis=-1, keepdims=True)
    o = jnp.dot(pr.astype(BF16), v, preferred_element_type=F32) / l
    o_ref[...] = (o * _silu(g_ref[...].astype(F32))).astype(o_ref.dtype)


def _na_bias_table(rpb, n_ctx):
    nq, nk = NA_QROWS * GRID_W, NA_KROWS * GRID_W
    qy, qx = np.divmod(np.arange(nq), GRID_W)
    ky, kx = np.divmod(np.arange(nk), GRID_W)
    c0 = np.clip(qx - NA_WIN_W // 2, 0, GRID_W - NA_WIN_W)
    col_ok = (kx[None, :] >= c0[:, None]) & (kx[None, :] < c0[:, None] + NA_WIN_W)
    dx = np.clip(kx[None, :] - qx[:, None] + NA_WIN_W - 1, 0, 2 * NA_WIN_W - 2)
    tabs = []
    for q_off, first_key in ((0, None), (NA_WIN_H // 2, "q"), (NA_WIN_H, NA_KROWS - NA_WIN_H)):
        r0 = qy if first_key == "q" else np.full_like(qy, 0 if first_key is None else first_key)
        row_ok = (ky[None, :] >= r0[:, None]) & (ky[None, :] < r0[:, None] + NA_WIN_H)
        dy = np.clip(ky[None, :] - (qy[:, None] + q_off) + NA_WIN_H - 1, 0, 2 * NA_WIN_H - 2)
        tabs.append(jnp.where(jnp.asarray(row_ok & col_ok)[None], rpb[:, dy, dx], NEG_BIG))
    loc = jnp.stack(tabs, axis=1)
    return jnp.concatenate([loc, jnp.zeros(loc.shape[:3] + (n_ctx,), F32)], axis=-1)


def _natten(p, rpb, batch, n_lat, seq):
    n_ctx = seq - n_lat
    tq = NA_QROWS * GRID_W
    n_blk = n_lat // tq
    assert n_blk >= 3 and n_lat % tq == 0 and n_ctx % tq == 0
    rb = seq // tq
    cb = seq // n_ctx
    bias = _na_bias_table(rpb, n_ctx)
    nk = NA_KROWS * GRID_W + n_ctx

    def kblk(i):
        return jnp.clip(i - 1, 0, n_blk - 3)

    def kspec(col0, j):
        return pl.BlockSpec((tq, LANE), lambda b, h, i: (b * rb + kblk(i) + j, col0 // LANE + h))

    def cspec(col0):
        return pl.BlockSpec((n_ctx, LANE), lambda b, h, i: (b * cb + n_lat // n_ctx, col0 // LANE + h))

    def qspec(col0):
        return pl.BlockSpec((tq, LANE), lambda b, h, i: (b * rb + i, col0 // LANE + h))

    def btype(i):
        return jnp.where(i == 0, 0, jnp.where(i == n_blk - 1, 2, 1))

    return pl.pallas_call(
        functools.partial(_natten_kernel, scale=HEAD_DIM ** -0.5),
        grid=(batch, NA_HEADS, n_blk),
        in_specs=[qspec(P_NQ), kspec(P_NK, 0), kspec(P_NK, 1), kspec(P_NK, 2), cspec(P_NK),
                  kspec(P_NV, 0), kspec(P_NV, 1), kspec(P_NV, 2), cspec(P_NV), qspec(P_NG),
                  pl.BlockSpec((None, None, tq, nk), lambda b, h, i: (h, btype(i), 0, 0))],
        out_specs=pl.BlockSpec((tq, LANE), lambda b, h, i: (b * rb + i, h)),
        out_shape=jax.ShapeDtypeStruct((p.shape[0], NA_HEADS * LANE), BF16),
        compiler_params=_params(("parallel", "parallel", "parallel")),
        name="natten_latent",
    )(p, p, p, p, p, p, p, p, p, p, bias)


def _merge_kernel(o0_ref, o1_ref, o2_ref, o3_ref, w0_ref, w1_ref, w2_ref, w3_ref,
                  m0_ref, m1_ref, m2_ref, m3_ref, y_ref):
    acc = None
    for o_ref, w_ref, m_ref in ((o0_ref, w0_ref, m0_ref), (o1_ref, w1_ref, m1_ref),
                                (o2_ref, w2_ref, m2_ref), (o3_ref, w3_ref, m3_ref)):
        gate = 1.0 / (1.0 + jnp.exp(-m_ref[...].astype(F32)))
        term = gate * jnp.dot(o_ref[...], w_ref[...], preferred_element_type=F32)
        acc = term if acc is None else acc + term
    y_ref[...] = acc.astype(y_ref.dtype)


def _merge(outs, weights, p, seq):
    t = p.shape[0]
    d = weights[0].shape[1]
    tm = _pick(seq, (768, 512, 256))
    tn = 512
    o_specs = [pl.BlockSpec((tm, o.shape[1]), lambda i, j: (i, 0)) for o in outs]
    w_specs = [pl.BlockSpec((w.shape[0], tn), lambda i, j: (0, j)) for w in weights]
    m_specs = [pl.BlockSpec((tm, tn), lambda i, j, b=b: (i, (P_MIX + b * d) // tn + j))
               for b in range(N_BRANCH)]
    return pl.pallas_call(
        _merge_kernel,
        grid=(t // tm, d // tn),
        in_specs=o_specs + w_specs + m_specs,
        out_specs=pl.BlockSpec((tm, tn), lambda i, j: (i, j)),
        out_shape=jax.ShapeDtypeStruct((t, d), BF16),
        compiler_params=_params(("parallel", "parallel")),
        name="branch_merge",
    )(*outs, *weights, p, p, p, p)


def _out_kernel(y_ref, w_ref, x_ref, nw_ref, gate_ref, o_ref, *, tm, tiles_per_batch, n_lat):
    z = jnp.dot(y_ref[...], w_ref[...], preferred_element_type=F32)
    zn = z * lax.rsqrt(jnp.mean(z * z, axis=-1, keepdims=True) + EPS) * nw_ref[...]
    is_ctx = _is_ctx_rows(pl.program_id(0), tm, tiles_per_batch, n_lat)
    gate = jnp.where(is_ctx, gate_ref[1:2, :], gate_ref[0:1, :])
    o_ref[...] = x_ref[...] + gate * zn


def _out_projection(y, w_out, xa, norm_w, gate, n_lat, seq):
    t, d = xa.shape
    tm = _pick(seq, (384, 256, 128))
    tpb = seq // tm
    kern = functools.partial(_out_kernel, tm=tm, tiles_per_batch=tpb, n_lat=n_lat)
    return pl.pallas_call(
        kern,
        grid=(t // tm,),
        in_specs=[pl.BlockSpec((tm, d), lambda i: (i, 0)),
                  pl.BlockSpec((d, d), lambda i: (0, 0)),
                  pl.BlockSpec((tm, d), lambda i: (i, 0)),
                  pl.BlockSpec((1, d), lambda i: (0, 0)),
                  pl.BlockSpec((None, 2, d), lambda i: (i // tpb, 0, 0))],
        out_specs=pl.BlockSpec((tm, d), lambda i: (i, 0)),
        out_shape=jax.ShapeDtypeStruct((t, d), F32),
        compiler_params=_params(("parallel",)),
        name="out_proj",
    )(y, w_out, xa, norm_w.reshape(1, d), gate)


def _pairs_apart(w, n_heads, dim):
    lead = w.shape[:-1]
    return w.reshape(lead + (n_heads, dim // 2, 2)).swapaxes(-1, -2).reshape(lead + (n_heads * dim,))


def _rope_tile(w):
    lead = w.shape[:-1]
    pr = w.reshape(lead + (MLA_ROPE // 2, 2))
    zero = jnp.zeros(lead + (MLA_ROPE // 2,), w.dtype)
    return jnp.concatenate([pr[..., 0], zero, pr[..., 1], zero], axis=-1)


def _layout_w_in(w):
    o = _OFF
    k = w.shape[0]

    def seg(name, width):
        return w[:, o[name]:o[name] + width]

    cols = [seg("mix", 8192), seg("z", 2048), seg("xbc", SSM_CONV_DIM),
            _pairs_apart(seg("gq", 1024), GQA_HEADS, HEAD_DIM), seg("gg", 1024),
            seg("nq", 1024), seg("nk", 1024), seg("nv", 1024), seg("ng", 1024), seg("mg", 1024),
            seg("mqa", MLA_Q_LORA), jnp.zeros((k, MQA_PAD - MLA_Q_LORA), w.dtype),
            _pairs_apart(seg("gk", 512), GQA_KV_HEADS, HEAD_DIM), seg("gv", 512),
            seg("mkva", MLA_KV_LORA), _rope_tile(seg("mkr", MLA_ROPE))]
    main = jnp.concatenate(cols, axis=1)
    main = jnp.pad(main, ((0, 0), (0, P_WIDTH - main.shape[1]))).astype(BF16)
    dtr = seg("dtr", 2 * SSM_HEADS)
    zero = jnp.zeros((k, LANE - SSM_HEADS), w.dtype)
    side = jnp.concatenate([dtr[:, :SSM_HEADS], zero, dtr[:, SSM_HEADS:], zero], axis=1).astype(BF16)
    return main, side


def _layout_w_uq(w_uq):
    k = w_uq.shape[0]
    w = w_uq.reshape(k, MLA_HEADS, MLA_NOPE + MLA_ROPE)
    w = jnp.concatenate([w[..., :MLA_NOPE], _rope_tile(w[..., MLA_NOPE:])], axis=-1)
    w = w.reshape(k, MLA_HEADS * MLA_QK_PAD)
    return jnp.pad(w, ((0, MQA_PAD - k), (0, 0))).astype(BF16)


def _rope_tables(n_lat, n_ctx, dim):
    t = np.arange(n_lat)
    quarter = dim // 4
    freqs = ROPE_THETA ** (-jnp.arange(quarter, dtype=F32) / quarter)
    row = jnp.asarray(t // GRID_W, F32)
    col = jnp.asarray(t % GRID_W, F32)
    ang = jnp.concatenate([row[:, None] * freqs, col[:, None] * freqs], axis=-1)
    cos, sin = jnp.cos(ang), jnp.sin(ang)
    pad = 64 - dim // 2
    one, zero = jnp.ones((n_lat, pad), F32), jnp.zeros((n_lat, pad), F32)
    cos_t = jnp.concatenate([cos, one, cos, one], axis=-1)
    sin_t = jnp.concatenate([-sin, zero, sin, zero], axis=-1)
    cos_t = jnp.concatenate([cos_t, jnp.ones((n_ctx, LANE), F32)], axis=0)
    sin_t = jnp.concatenate([sin_t, jnp.zeros((n_ctx, LANE), F32)], axis=0)
    return cos_t, sin_t


def _rows_select(lat, ctx, batch, n_lat, seq):
    w = lat.shape[-1]
    return jnp.concatenate([lat.reshape(batch, seq, w)[:, :n_lat],
                            ctx.reshape(batch, seq, w)[:, n_lat:]], axis=1).reshape(batch * seq, w)


def _layer(xa, cc, rope_g, rope_m, lp, batch, n_lat, seq):
    d = xa.shape[1]
    mod = _modulation(cc, lp["ada_w"], lp["ada_b"])

    def per_row(v):
        return jnp.stack([v[:batch], jnp.broadcast_to(v[batch:batch + 1], (batch, d))], axis=1)

    shift, scale, gate = (per_row(mod[:, k * d:(k + 1) * d]) for k in range(3))
    w_main, w_side = _layout_w_in(lp["w_in"])
    p, dtr = _in_projection(xa, lp["norm_pre"], scale, shift, w_main, w_side, n_lat, seq)

    u = _conv_silu(p, lp["conv_w"], lp["conv_b"], n_lat, seq)
    y = _ssd(u, dtr, lp["a_log"], lp["dt_bias"], batch, n_lat, seq)
    o_ssm = _ssm_out(y, u, p, lp["d_skip"], lp["ssm_norm"], seq)

    common = dict(batch=batch, n_lat=n_lat, seq=seq)
    qg = _head_norm_rope(p, P_GQ, GQA_HEADS, _pairs_apart(lp["gqa_q_norm"], 1, HEAD_DIM), *rope_g, seq)
    kg = _head_norm_rope(p, P_GK, GQA_KV_HEADS, _pairs_apart(lp["gqa_k_norm"], 1, HEAD_DIM), *rope_g, seq)
    o_gqa = _rows_select(*_attention(
        qg, kg, p, p, 256, q_col0=0, k_col0=0, v_col0=P_GV, g_col0=P_GG, v_stride=1,
        n_kv_heads=GQA_KV_HEADS, r=GQA_HEADS // GQA_KV_HEADS, dq=HEAD_DIM, scale=HEAD_DIM ** -0.5,
        **common), **common)

    na_lat = _natten(p, lp["na_rpb"], **common)
    na_ctx = _flash(p, p, p, p, q_col0=P_NQ, k_col0=P_NK, v_col0=P_NV, g_col0=P_NG, v_stride=1,
                    n_kv_heads=NA_HEADS, r=1, dq=HEAD_DIM, scale=HEAD_DIM ** -0.5, ctx_only=True,
                    tq=256, **common)
    o_na = _rows_select(na_lat, na_ctx, **common)

    q_norm = jnp.pad(lp["mla_q_norm"], (0, MQA_PAD - MLA_Q_LORA)).reshape(1, MQA_PAD)
    qm = _mla_q(p, q_norm, _layout_w_uq(lp["w_uq"]), *rope_m, seq)
    km, vm = _mla_kv(p, lp["mla_kv_norm"], lp["w_ukv"].astype(BF16), *rope_m, seq)
    o_mla = _rows_select(*_attention(
        qm, km, vm, p, 512, q_col0=0, k_col0=0, v_col0=0, g_col0=P_MG, v_stride=1,
        n_kv_heads=MLA_HEADS, r=1, dq=MLA_QK_PAD, scale=(MLA_NOPE + MLA_ROPE) ** -0.5,
        **common), **common)

    weights = [lp[n].astype(BF16) for n in ("w_o_ssm", "w_o_gqa", "w_o_na", "w_o_mla")]
    ymix = _merge([o_ssm, o_gqa, o_na, o_mla], weights, p, seq)
    return _out_projection(ymix, lp["w_out"].astype(BF16), xa, lp["norm_post"], gate, n_lat, seq)


def kernel(x, c, ctx, c_ctx, ada_w, ada_b, norm_pre, norm_post, w_in, conv_w, conv_b, a_log, dt_bias,
           d_skip, ssm_norm, w_o_ssm, gqa_q_norm, gqa_k_norm, w_o_gqa, na_rpb, w_o_na, mla_q_norm,
           w_uq, mla_kv_norm, w_ukv, w_o_mla, w_out):
    batch, n_lat, d = x.shape
    n_ctx = ctx.shape[1]
    seq = n_lat + n_ctx
    stacked = dict(ada_w=ada_w, ada_b=ada_b, norm_pre=norm_pre, norm_post=norm_post, w_in=w_in,
                   conv_w=conv_w, conv_b=conv_b, a_log=a_log, dt_bias=dt_bias, d_skip=d_skip,
                   ssm_norm=ssm_norm, w_o_ssm=w_o_ssm, gqa_q_norm=gqa_q_norm, gqa_k_norm=gqa_k_norm,
                   w_o_gqa=w_o_gqa, na_rpb=na_rpb, w_o_na=w_o_na, mla_q_norm=mla_q_norm, w_uq=w_uq,
                   mla_kv_norm=mla_kv_norm, w_ukv=w_ukv, w_o_mla=w_o_mla, w_out=w_out)
    xa = jnp.concatenate([x, ctx], axis=1).reshape(batch * seq, d)
    cc = jnp.concatenate([c, c_ctx[None, :], jnp.zeros((8 - batch - 1, d), c.dtype)], axis=0)
    rope_g = _rope_tables(n_lat, n_ctx, HEAD_DIM)
    rope_m = _rope_tables(n_lat, n_ctx, MLA_ROPE)
    for layer in range(ada_w.shape[0]):
        lp = {k: v[layer] for k, v in stacked.items()}
        xa = _layer(xa, cc, rope_g, rope_m, lp, batch, n_lat, seq)
    return xa.reshape(batch, seq, d)[:, :n_lat]
```

```python
import functools
import math

import jax
import jax.numpy as jnp
import numpy as np
from jax import lax
from jax.experimental import pallas as pl
from jax.experimental.pallas import tpu as pltpu

F32 = jnp.float32
BF16 = jnp.bfloat16

GRID_W = 64
EPS = 1e-6
ROPE_THETA = 10000.0

SSM_HEADS = 32
SSM_HEAD_DIM = 64
SSM_INNER = SSM_HEADS * SSM_HEAD_DIM
SSM_GROUPS = 4
SSM_STATE = 128
SSM_CONV = 5
SSM_CHUNK = 128
SSM_CONV_DIM = SSM_INNER + 2 * SSM_GROUPS * SSM_STATE

GQA_HEADS = 8
GQA_KV_HEADS = 4
HEAD_DIM = 128
NA_HEADS = 8
NA_WIN_H = 8
NA_WIN_W = 16
NA_QROWS = 4
NA_KROWS = NA_QROWS + NA_WIN_H

MLA_HEADS = 8
MLA_Q_LORA = 768
MLA_KV_LORA = 512
MLA_NOPE = 128
MLA_ROPE = 64
MLA_QK_PAD = 256

N_BRANCH = 4
LANE = 128
VMEM_LIMIT = 56 * 1024 * 1024
NEG_BIG = -1e30
FLASH_TK = 256

_SIZES = (SSM_INNER, SSM_CONV_DIM, 2 * SSM_HEADS, 1024, 512, 512, 1024, 1024, 1024, 1024, 1024,
          MLA_Q_LORA, MLA_KV_LORA, MLA_ROPE, 1024, N_BRANCH * 2048)
_OFF = dict(zip(("z", "xbc", "dtr", "gq", "gk", "gv", "gg", "nq", "nk", "nv", "ng",
                 "mqa", "mkva", "mkr", "mg", "mix"), np.cumsum((0,) + _SIZES[:-1]).tolist()))

P_MIX, P_Z, P_XBC = 0, 8192, 10240
P_GQ, P_GG, P_NQ, P_NK, P_NV, P_NG, P_MG, P_MQA = (13312, 14336, 15360, 16384, 17408, 18432,
                                                    19456, 20480)
P_GK, P_GV, P_MKVA, P_MKR = 21504, 22016, 22528, 23040
P_WIDTH = 23552
MQA_PAD = 1024


def _pick(n, candidates):
    for c in candidates:
        if n % c == 0:
            return c
    raise ValueError(f"no tile for {n} among {candidates}")


def _params(sem):
    return pltpu.CompilerParams(dimension_semantics=sem, vmem_limit_bytes=VMEM_LIMIT)


def _silu(v):
    return v * (1.0 / (1.0 + jnp.exp(-v)))


def _rope(v, cos, sin):
    return v * cos + pltpu.roll(v, 64, axis=1) * sin


def _mod_kernel(c_ref, w_ref, b_ref, o_ref):
    h = _silu(c_ref[...]).astype(BF16)
    o_ref[...] = jnp.dot(h, w_ref[...].astype(BF16), preferred_element_type=F32) + b_ref[...]


def _modulation(cc, ada_w, ada_b):
    rows, d = cc.shape
    n = ada_w.shape[1]
    tn = _pick(n, (512, 256, 128))
    return pl.pallas_call(
        _mod_kernel,
        grid=(n // tn,),
        in_specs=[pl.BlockSpec((rows, d), lambda j: (0, 0)),
                  pl.BlockSpec((d, tn), lambda j: (0, j)),
                  pl.BlockSpec((1, tn), lambda j: (0, j))],
        out_specs=pl.BlockSpec((rows, tn), lambda j: (0, j)),
        out_shape=jax.ShapeDtypeStruct((rows, n), F32),
        compiler_params=_params(("parallel",)),
        name="adaln_mod",
    )(cc, ada_w, ada_b.reshape(1, n))


def _is_ctx_rows(i, tm, tiles_per_batch, n_lat):
    row = (i % tiles_per_batch) * tm + lax.broadcasted_iota(jnp.int32, (tm, 1), 0)
    return row >= n_lat


def _inproj_kernel(x_ref, nw_ref, sc_ref, sh_ref, w_ref, ws_ref, o_ref, os_ref, h_ref, *,
                   tm, tiles_per_batch, n_lat):
    i = pl.program_id(0)

    @pl.when(pl.program_id(1) == 0)
    def _():
        x = x_ref[...]
        y = x * lax.rsqrt(jnp.mean(x * x, axis=-1, keepdims=True) + EPS) * nw_ref[...]
        is_ctx = _is_ctx_rows(i, tm, tiles_per_batch, n_lat)
        sc = jnp.where(is_ctx, sc_ref[1:2, :], sc_ref[0:1, :])
        sh = jnp.where(is_ctx, sh_ref[1:2, :], sh_ref[0:1, :])
        h = (y * (1.0 + sc) + sh).astype(BF16)
        h_ref[...] = h
        os_ref[...] = jnp.dot(h, ws_ref[...], preferred_element_type=F32)

    o_ref[...] = jnp.dot(h_ref[...], w_ref[...], preferred_element_type=F32).astype(o_ref.dtype)


def _in_projection(xa, norm_w, scale, shift, w_p, w_side, n_lat, seq):
    t, d = xa.shape
    tm = _pick(seq, (768, 512, 256))
    tn = 512
    tpb = seq // tm
    kern = functools.partial(_inproj_kernel, tm=tm, tiles_per_batch=tpb, n_lat=n_lat)
    return pl.pallas_call(
        kern,
        grid=(t // tm, P_WIDTH // tn),
        in_specs=[pl.BlockSpec((tm, d), lambda i, j: (i, 0)),
                  pl.BlockSpec((1, d), lambda i, j: (0, 0)),
                  pl.BlockSpec((None, 2, d), lambda i, j: (i // tpb, 0, 0)),
                  pl.BlockSpec((None, 2, d), lambda i, j: (i // tpb, 0, 0)),
                  pl.BlockSpec((d, tn), lambda i, j: (0, j)),
                  pl.BlockSpec((d, 2 * LANE), lambda i, j: (0, 0))],
        out_specs=[pl.BlockSpec((tm, tn), lambda i, j: (i, j)),
                   pl.BlockSpec((tm, 2 * LANE), lambda i, j: (i, 0))],
        out_shape=[jax.ShapeDtypeStruct((t, P_WIDTH), BF16),
                   jax.ShapeDtypeStruct((t, 2 * LANE), F32)],
        scratch_shapes=[pltpu.VMEM((tm, d), BF16)],
        compiler_params=_params(("parallel", "arbitrary")),
        name="in_proj",
    )(xa, norm_w.reshape(1, d), scale, shift, w_p, w_side)


def _conv_kernel(u_ref, prev_ref, next_ref, w_ref, b_ref, o_ref, ext_ref, *,
                 tm, tiles_per_batch, lat_tiles):
    ib = pl.program_id(0) % tiles_per_batch
    first = jnp.logical_or(ib == 0, ib == lat_tiles)
    last = jnp.logical_or(ib == lat_tiles - 1, ib == tiles_per_batch - 1)
    ext_ref[0:8, :] = jnp.where(first, 0.0, prev_ref[...].astype(F32))
    ext_ref[8:8 + tm, :] = u_ref[...].astype(F32)
    ext_ref[8 + tm:16 + tm, :] = jnp.where(last, 0.0, next_ref[...].astype(F32))
    acc = b_ref[...] + w_ref[0:1, :] * ext_ref[pl.ds(8 - SSM_CONV // 2, tm), :]
    for k in range(1, SSM_CONV):
        acc = acc + w_ref[k:k + 1, :] * ext_ref[pl.ds(8 - SSM_CONV // 2 + k, tm), :]
    o_ref[...] = _silu(acc).astype(o_ref.dtype)


def _conv_silu(p, conv_w, conv_b, n_lat, seq):
    t = p.shape[0]
    tm = 256
    tc = 1024
    tpb = seq // tm
    n_row8 = t // 8
    col0 = P_XBC // tc
    kern = functools.partial(_conv_kernel, tm=tm, tiles_per_batch=tpb, lat_tiles=n_lat // tm)
    return pl.pallas_call(
        kern,
        grid=(t // tm, SSM_CONV_DIM // tc),
        in_specs=[pl.BlockSpec((tm, tc), lambda i, j: (i, col0 + j)),
                  pl.BlockSpec((8, tc), lambda i, j: (jnp.maximum(i * (tm // 8) - 1, 0), col0 + j)),
                  pl.BlockSpec((8, tc), lambda i, j: (jnp.minimum((i + 1) * (tm // 8), n_row8 - 1),
                                                      col0 + j)),
                  pl.BlockSpec((8, tc), lambda i, j: (0, j)),
                  pl.BlockSpec((1, tc), lambda i, j: (0, j))],
        out_specs=pl.BlockSpec((tm, tc), lambda i, j: (i, j)),
        out_shape=jax.ShapeDtypeStruct((t, SSM_CONV_DIM), BF16),
        scratch_shapes=[pltpu.VMEM((tm + 16, tc), F32)],
        compiler_params=_params(("parallel", "parallel")),
        name="ssm_conv",
    )(p, p, p, jnp.pad(conv_w, ((0, 8 - SSM_CONV), (0, 0))), conv_b.reshape(1, -1))


def _split_dot(a_bf16, v):
    v1 = v.astype(BF16)
    r1 = v - v1.astype(F32)
    v2 = r1.astype(BF16)
    v3 = (r1 - v2.astype(F32)).astype(BF16)
    dot = functools.partial(jnp.dot, preferred_element_type=F32)
    return dot(a_bf16, v1) + dot(a_bf16, v2) + dot(a_bf16, v3)


def _ssd_kernel(u_ref, dtr_ref, alog_ref, bias_ref, eh_ref, y_ref, st_ref):
    cl = SSM_CHUNK
    fwd = pl.program_id(1) == 0

    @pl.when(pl.program_id(2) == 0)
    def _():
        st_ref[...] = jnp.zeros_like(st_ref)

    raw = dtr_ref[...] + bias_ref[...]
    dt = jnp.maximum(raw, 0.0) + jnp.log(1.0 + jnp.exp(-jnp.abs(raw)))
    da = dt * (-jnp.exp(alog_ref[...]))
    r = lax.broadcasted_iota(jnp.int32, (cl, cl), 0)
    c = lax.broadcasted_iota(jnp.int32, (cl, cl), 1)
    tri = jnp.where(fwd, r - c, c - r) >= 0
    acs = _split_dot(jnp.where(tri, 1.0, 0.0).astype(BF16), da)
    total = jnp.where(fwd, acs[cl - 1:cl, :], acs[0:1, :])
    e_acs = jnp.exp(acs)
    w_end = dt * jnp.exp(total - acs)
    acs_t = acs.T
    dt_t = dt.T

    def hi_lo(v):
        hi = v.astype(BF16)
        return hi, (v - hi.astype(F32)).astype(BF16)

    stack = jnp.concatenate(hi_lo(w_end) + hi_lo(e_acs), axis=0)
    ex = jnp.dot(stack, eh_ref[...], preferred_element_type=F32)
    w_exp = ex[0:cl] + ex[cl:2 * cl]
    e_exp = ex[2 * cl:3 * cl] + ex[3 * cl:4 * cl]
    dec = jnp.where(fwd, e_exp[cl - 1:cl, :], e_exp[0:1, :])
    xw = (u_ref[:, 0:SSM_INNER].astype(F32) * w_exp).astype(BF16)
    lane = lax.broadcasted_iota(jnp.int32, (cl, LANE), 1)
    gw = SSM_INNER // SSM_GROUPS
    hpg = SSM_HEADS // SSM_GROUPS
    for g in range(SSM_GROUPS):
        b_g = u_ref[:, SSM_INNER + g * SSM_STATE:SSM_INNER + (g + 1) * SSM_STATE]
        c_off = SSM_INNER + SSM_GROUPS * SSM_STATE
        c_g = u_ref[:, c_off + g * SSM_STATE:c_off + (g + 1) * SSM_STATE]
        cb = lax.dot_general(c_g, b_g, (((1,), (1,)), ((), ())), preferred_element_type=F32)
        st_g = st_ref[:, g * gw:(g + 1) * gw]
        y_off = jnp.dot(c_g, st_g.astype(BF16), preferred_element_type=F32)
        ys = []
        for k in range(hpg // 2):
            ms = []
            for h in (g * hpg + 2 * k, g * hpg + 2 * k + 1):
                seg = acs[:, h:h + 1] - acs_t[h:h + 1, :]
                dec_h = jnp.exp(jnp.where(tri, seg, NEG_BIG))
                ms.append((dec_h * cb * dt_t[h:h + 1, :]).astype(BF16))
            x_p = u_ref[:, g * gw + k * LANE:g * gw + (k + 1) * LANE]
            zero = jnp.zeros_like(x_p)
            rhs = jnp.concatenate([jnp.where(lane < SSM_HEAD_DIM, x_p, zero),
                                   jnp.where(lane >= SSM_HEAD_DIM, x_p, zero)], axis=0)
            ys.append(jnp.dot(jnp.concatenate(ms, axis=1), rhs, preferred_element_type=F32))
        y_g = jnp.concatenate(ys, axis=1) + y_off * e_exp[:, g * gw:(g + 1) * gw]
        y_ref[:, g * gw:(g + 1) * gw] = y_g.astype(y_ref.dtype)
        upd = lax.dot_general(b_g, xw[:, g * gw:(g + 1) * gw], (((0,), (0,)), ((), ())),
                              preferred_element_type=F32)
        st_ref[:, g * gw:(g + 1) * gw] = st_g * dec[:, g * gw:(g + 1) * gw] + upd


def _ssd(u, dtr, a_log, dt_bias, batch, n_lat, seq):
    cl = SSM_CHUNK
    nch, nlat, nctx = seq // cl, n_lat // cl, (seq - n_lat) // cl

    def chunk(d, c):
        f = jnp.where(c < nctx, nlat + c, c - nctx)
        b = jnp.where(c < nctx, nlat + nctx - 1 - c, nlat - 1 - (c - nctx))
        return jnp.where(d == 0, f, b)

    def pad_heads(v):
        return jnp.pad(v, ((0, 0), (0, LANE - SSM_HEADS))).reshape(2, 1, LANE)

    eh = np.zeros((LANE, SSM_INNER), np.float32)
    for h in range(SSM_HEADS):
        eh[h, h * SSM_HEAD_DIM:(h + 1) * SSM_HEAD_DIM] = 1.0
    return pl.pallas_call(
        _ssd_kernel,
        grid=(batch, 2, nch),
        in_specs=[pl.BlockSpec((cl, SSM_CONV_DIM), lambda b, d, c: (b * nch + chunk(d, c), 0)),
                  pl.BlockSpec((cl, LANE), lambda b, d, c: (b * nch + chunk(d, c), d)),
                  pl.BlockSpec((None, 1, LANE), lambda b, d, c: (d, 0, 0)),
                  pl.BlockSpec((None, 1, LANE), lambda b, d, c: (d, 0, 0)),
                  pl.BlockSpec((LANE, SSM_INNER), lambda b, d, c: (0, 0))],
        out_specs=pl.BlockSpec((None, cl, SSM_INNER), lambda b, d, c: (d, b * nch + chunk(d, c), 0)),
        out_shape=jax.ShapeDtypeStruct((2, u.shape[0], SSM_INNER), BF16),
        scratch_shapes=[pltpu.VMEM((SSM_STATE, SSM_INNER), F32)],
        compiler_params=_params(("parallel", "parallel", "arbitrary")),
        name="ssd_scan",
    )(u, dtr, pad_heads(a_log), pad_heads(dt_bias), jnp.asarray(eh, BF16))


def _ssm_out_kernel(yf_ref, yb_ref, x_ref, z_ref, skip_ref, nw_ref, o_ref):
    z = z_ref[...].astype(F32)
    g = (yf_ref[...].astype(F32) + yb_ref[...].astype(F32)
         + skip_ref[...] * x_ref[...].astype(F32)) * _silu(z)
    gw = SSM_INNER // SSM_GROUPS
    for k in range(SSM_GROUPS):
        gk = g[:, k * gw:(k + 1) * gw]
        gk = gk * lax.rsqrt(jnp.mean(gk * gk, axis=-1, keepdims=True) + EPS)
        o_ref[:, k * gw:(k + 1) * gw] = (gk * nw_ref[:, k * gw:(k + 1) * gw]).astype(o_ref.dtype)


def _ssm_out(y, u, p, d_skip, ssm_norm, seq):
    t = u.shape[0]
    tm = _pick(seq, (384, 256, 128))
    w = SSM_INNER
    return pl.pallas_call(
        _ssm_out_kernel,
        grid=(t // tm,),
        in_specs=[pl.BlockSpec((None, tm, w), lambda i: (0, i, 0)),
                  pl.BlockSpec((None, tm, w), lambda i: (1, i, 0)),
                  pl.BlockSpec((tm, w), lambda i: (i, 0)),
                  pl.BlockSpec((tm, w), lambda i: (i, P_Z // w)),
                  pl.BlockSpec((1, w), lambda i: (0, 0)),
                  pl.BlockSpec((1, w), lambda i: (0, 0))],
        out_specs=pl.BlockSpec((tm, w), lambda i: (i, 0)),
        out_shape=jax.ShapeDtypeStruct((t, w), BF16),
        compiler_params=_params(("parallel",)),
        name="ssm_gated_norm",
    )(y, y, u, p, jnp.repeat(d_skip, SSM_HEAD_DIM).reshape(1, w), ssm_norm.reshape(1, w))


def _head_norm_rope_kernel(x_ref, nw_ref, cos_ref, sin_ref, o_ref):
    x = x_ref[...].astype(F32)
    y = x * lax.rsqrt(jnp.mean(x * x, axis=-1, keepdims=True) + EPS) * nw_ref[...]
    o_ref[...] = _rope(y, cos_ref[...], sin_ref[...]).astype(o_ref.dtype)


def _head_norm_rope(p, col0, n_heads, norm_w, cos, sin, seq):
    t = p.shape[0]
    tm = _pick(seq, (768, 512, 256))
    tpb = seq // tm
    return pl.pallas_call(
        _head_norm_rope_kernel,
        grid=(t // tm, n_heads),
        in_specs=[pl.BlockSpec((tm, LANE), lambda i, h: (i, col0 // LANE + h)),
                  pl.BlockSpec((1, LANE), lambda i, h: (0, 0)),
                  pl.BlockSpec((tm, LANE), lambda i, h: (i % tpb, 0)),
                  pl.BlockSpec((tm, LANE), lambda i, h: (i % tpb, 0))],
        out_specs=pl.BlockSpec((tm, LANE), lambda i, h: (i, h)),
        out_shape=jax.ShapeDtypeStruct((t, n_heads * LANE), BF16),
        compiler_params=_params(("parallel", "parallel")),
        name="head_norm_rope",
    )(p, norm_w.reshape(1, LANE), cos, sin)


def _mla_q_kernel(x_ref, nw_ref, w_ref, cos_ref, sin_ref, o_ref, h_ref):
    @pl.when(pl.program_id(1) == 0)
    def _():
        x = x_ref[...].astype(F32)
        ms = jnp.sum(x * x, axis=-1, keepdims=True) * (1.0 / MLA_Q_LORA)
        h_ref[...] = (x * lax.rsqrt(ms + EPS) * nw_ref[...]).astype(BF16)

    res = jnp.dot(h_ref[...], w_ref[...], preferred_element_type=F32)
    o_ref[:, 0:LANE] = res[:, 0:LANE].astype(o_ref.dtype)
    o_ref[:, LANE:] = _rope(res[:, LANE:], cos_ref[...], sin_ref[...]).astype(o_ref.dtype)


def _mla_q(p, q_norm, w_uq, cos, sin, seq):
    t = p.shape[0]
    tm = _pick(seq, (768, 512, 256))
    tpb = seq // tm
    kdim = MQA_PAD
    return pl.pallas_call(
        _mla_q_kernel,
        grid=(t // tm, MLA_HEADS),
        in_specs=[pl.BlockSpec((tm, kdim), lambda i, h: (i, P_MQA // kdim)),
                  pl.BlockSpec((1, kdim), lambda i, h: (0, 0)),
                  pl.BlockSpec((kdim, MLA_QK_PAD), lambda i, h: (0, h)),
                  pl.BlockSpec((tm, LANE), lambda i, h: (i % tpb, 0)),
                  pl.BlockSpec((tm, LANE), lambda i, h: (i % tpb, 0))],
        out_specs=pl.BlockSpec((tm, MLA_QK_PAD), lambda i, h: (i, h)),
        out_shape=jax.ShapeDtypeStruct((t, MLA_HEADS * MLA_QK_PAD), BF16),
        scratch_shapes=[pltpu.VMEM((tm, kdim), BF16)],
        compiler_params=_params(("parallel", "arbitrary")),
        name="mla_q_up",
    )(p, q_norm, w_uq, cos, sin)


def _mla_kv_kernel(x_ref, kr_ref, nw_ref, w_ref, cos_ref, sin_ref, k_ref, v_ref, h_ref):
    @pl.when(pl.program_id(1) == 0)
    def _():
        x = x_ref[...].astype(F32)
        h_ref[...] = (x * lax.rsqrt(jnp.mean(x * x, axis=-1, keepdims=True) + EPS)
                      * nw_ref[...]).astype(BF16)

    res = jnp.dot(h_ref[...], w_ref[...], preferred_element_type=F32)
    k_ref[:, 0:LANE] = res[:, 0:LANE].astype(k_ref.dtype)
    k_ref[:, LANE:] = _rope(kr_ref[...].astype(F32), cos_ref[...], sin_ref[...]).astype(k_ref.dtype)
    v_ref[...] = res[:, LANE:].astype(v_ref.dtype)


def _mla_kv(p, kv_norm, w_ukv, cos, sin, seq):
    t = p.shape[0]
    tm = _pick(seq, (768, 512, 256))
    tpb = seq // tm
    kdim = MLA_KV_LORA
    return pl.pallas_call(
        _mla_kv_kernel,
        grid=(t // tm, MLA_HEADS),
        in_specs=[pl.BlockSpec((tm, kdim), lambda i, h: (i, P_MKVA // kdim)),
                  pl.BlockSpec((tm, LANE), lambda i, h: (i, P_MKR // LANE)),
                  pl.BlockSpec((1, kdim), lambda i, h: (0, 0)),
                  pl.BlockSpec((kdim, 2 * LANE), lambda i, h: (0, h)),
                  pl.BlockSpec((tm, LANE), lambda i, h: (i % tpb, 0)),
                  pl.BlockSpec((tm, LANE), lambda i, h: (i % tpb, 0))],
        out_specs=[pl.BlockSpec((tm, MLA_QK_PAD), lambda i, h: (i, h)),
                   pl.BlockSpec((tm, LANE), lambda i, h: (i, h))],
        out_shape=[jax.ShapeDtypeStruct((t, MLA_HEADS * MLA_QK_PAD), BF16),
                   jax.ShapeDtypeStruct((t, MLA_HEADS * LANE), BF16)],
        scratch_shapes=[pltpu.VMEM((tm, kdim), BF16)],
        compiler_params=_params(("parallel", "arbitrary")),
        name="mla_kv_up",
    )(p, p, kv_norm.reshape(1, kdim), w_ukv, cos, sin)


def _flash_kernel(q_ref, k_ref, v_ref, g_ref, o_ref, *, r, tq, dq, tk, n_keys, scale):
    q = jnp.concatenate([q_ref[:, j * dq:(j + 1) * dq] for j in range(r)], axis=0)
    q = (q.astype(F32) * (scale * math.log2(math.e))).astype(BF16)
    m_rows = r * tq
    m = jnp.full((m_rows, 1), NEG_BIG, F32)
    l = jnp.zeros((m_rows, LANE), F32)
    acc = jnp.zeros((m_rows, LANE), F32)
    for c in range(n_keys // tk):
        k_c = k_ref[c * tk:(c + 1) * tk, :]
        v_c = v_ref[c * tk:(c + 1) * tk, :]
        s = lax.dot_general(q, k_c, (((1,), (1,)), ((), ())), preferred_element_type=F32)
        m_new = jnp.maximum(m, jnp.max(s, axis=-1, keepdims=True))
        alpha = jnp.exp2(m - m_new)
        pr = jnp.exp2(s - m_new)
        part = pr[:, 0:LANE]
        for j in range(1, tk // LANE):
            part = part + pr[:, j * LANE:(j + 1) * LANE]
        l = alpha * l + part
        acc = alpha * acc + jnp.dot(pr.astype(BF16), v_c, preferred_element_type=F32)
        m = m_new
    o = acc / jnp.sum(l, axis=-1, keepdims=True)
    for j in range(r):
        gate = _silu(g_ref[:, j * LANE:(j + 1) * LANE].astype(F32))
        o_ref[:, j * LANE:(j + 1) * LANE] = (o[j * tq:(j + 1) * tq] * gate).astype(o_ref.dtype)


def _flash(q, k, v, gates, *, q_col0, k_col0, v_col0, g_col0, v_stride, n_kv_heads, r, dq, scale,
           batch, n_lat, seq, ctx_only, tq):
    n_ctx = seq - n_lat
    tk = FLASH_TK
    if ctx_only:
        tq = n_ctx
        q_blk0, n_q, kv_rows, kv_blk0 = n_lat // tq, 1, n_ctx, n_lat // n_ctx
    else:
        q_blk0, n_q, kv_rows, kv_blk0 = 0, n_lat // tq, seq, 0
    assert kv_rows % tk == 0
    kern = functools.partial(_flash_kernel, r=r, tq=tq, dq=dq, tk=tk, n_keys=kv_rows, scale=scale)
    w_out = n_kv_heads * r * LANE
    out = pl.pallas_call(
        kern,
        grid=(batch, n_kv_heads, n_q),
        in_specs=[pl.BlockSpec((None, tq, r * dq), lambda b, g, i: (b, q_blk0 + i, q_col0 // (r * dq) + g)),
                  pl.BlockSpec((None, kv_rows, dq), lambda b, g, i: (b, kv_blk0, k_col0 // dq + g)),
                  pl.BlockSpec((None, kv_rows, LANE),
                               lambda b, g, i: (b, kv_blk0, v_col0 // LANE + v_stride * g)),
                  pl.BlockSpec((None, tq, r * LANE),
                               lambda b, g, i: (b, q_blk0 + i, g_col0 // (r * LANE) + g))],
        out_specs=pl.BlockSpec((None, tq, r * LANE), lambda b, g, i: (b, q_blk0 + i, g)),
        out_shape=jax.ShapeDtypeStruct((batch, seq, w_out), BF16),
        compiler_params=_params(("parallel", "parallel", "parallel")),
        name="flash_ctx" if ctx_only else "flash_latent",
    )(*(a.reshape(batch, seq, a.shape[-1]) for a in (q, k, v, gates)))
    return out.reshape(batch * seq, w_out)


def _attention(q, k, v, gates, tq, **kw):
    lat = _flash(q, k, v, gates, ctx_only=False, tq=tq, **kw)
    ctx = _flash(q, k, v, gates, ctx_only=True, tq=tq, **kw)
    return lat, ctx


def _natten_kernel(q_ref, k0_ref, k1_ref, k2_ref, kc_ref, v0_ref, v1_ref, v2_ref, vc_ref, g_ref,
                   bias_ref, o_ref, *, scale):
    q = (q_ref[...].astype(F32) * scale).astype(BF16)
    k = jnp.concatenate([k0_ref[...], k1_ref[...], k2_ref[...], kc_ref[...]], axis=0)
    v = jnp.concatenate([v0_ref[...], v1_ref[...], v2_ref[...], vc_ref[...]], axis=0)
    s = lax.dot_general(q, k, (((1,), (1,)), ((), ())), preferred_element_type=F32)
    n_loc = bias_ref.shape[-1]
    s = jnp.concatenate([s[:, :n_loc] + bias_ref[...], s[:, n_loc:]], axis=1)
    pr = jnp.exp(s - jnp.max(s, axis=-1, keepdims=True))
    l = jnp.sum(pr, axis=-1, keepdims=True)
    o = jnp.dot(pr.astype(BF16), v, preferred_element_type=F32) / l
    o_ref[...] = (o * _silu(g_ref[...].astype(F32))).astype(o_ref.dtype)


def _na_bias_table(rpb):
    qx = np.arange(GRID_W)
    c0 = np.clip(qx - NA_WIN_W // 2, 0, GRID_W - NA_WIN_W)
    col_ok = (qx[None, :] >= c0[:, None]) & (qx[None, :] < c0[:, None] + NA_WIN_W)
    dx = qx[None, :] - qx[:, None] + NA_WIN_W - 1
    pick = np.zeros((2 * NA_WIN_W - 1, GRID_W * GRID_W), np.float32)
    qi, ki = np.nonzero(col_ok)
    pick[dx[qi, ki], qi * GRID_W + ki] = 1.0
    by_dx = jnp.einsum("...d,dn->...n", rpb, pick, precision=lax.Precision.HIGHEST)
    qy, ky = np.arange(NA_QROWS), np.arange(NA_KROWS)
    dys, oks = [], []
    for q_off, first_key in ((0, None), (NA_WIN_H // 2, "q"), (NA_WIN_H, NA_KROWS - NA_WIN_H)):
        r0 = qy if first_key == "q" else np.full_like(qy, 0 if first_key is None else first_key)
        oks.append((ky[None, :] >= r0[:, None]) & (ky[None, :] < r0[:, None] + NA_WIN_H))
        dys.append(np.clip(ky[None, :] - (qy[:, None] + q_off) + NA_WIN_H - 1, 0, 2 * NA_WIN_H - 2))
    rows = jnp.take(by_dx, np.stack(dys).reshape(-1), axis=-2)
    lead = rows.shape[:-2]
    nl = len(lead)
    rows = rows.reshape(lead + (3, NA_QROWS, NA_KROWS, GRID_W, GRID_W))
    rows = rows.transpose(tuple(range(nl)) + (nl, nl + 1, nl + 3, nl + 2, nl + 4))
    ok = np.stack(oks)[:, :, None, :, None] & col_ok[None, None, :, None, :]
    tab = jnp.where(jnp.asarray(ok), rows, NEG_BIG)
    return tab.reshape(lead + (3, NA_QROWS * GRID_W, NA_KROWS * GRID_W))


def _natten(p, bias, batch, n_lat, seq):
    n_ctx = seq - n_lat
    tq = NA_QROWS * GRID_W
    n_blk = n_lat // tq
    assert n_blk >= 3 and n_lat % tq == 0 and n_ctx % tq == 0
    rb = seq // tq
    cb = seq // n_ctx
    nk = NA_KROWS * GRID_W

    def kblk(i):
        return jnp.clip(i - 1, 0, n_blk - 3)

    def kspec(col0, j):
        return pl.BlockSpec((tq, LANE), lambda b, h, i: (b * rb + kblk(i) + j, col0 // LANE + h))

    def cspec(col0):
        return pl.BlockSpec((n_ctx, LANE), lambda b, h, i: (b * cb + n_lat // n_ctx, col0 // LANE + h))

    def qspec(col0):
        return pl.BlockSpec((tq, LANE), lambda b, h, i: (b * rb + i, col0 // LANE + h))

    def btype(i):
        return jnp.where(i == 0, 0, jnp.where(i == n_blk - 1, 2, 1))

    return pl.pallas_call(
        functools.partial(_natten_kernel, scale=HEAD_DIM ** -0.5),
        grid=(batch, NA_HEADS, n_blk),
        in_specs=[qspec(P_NQ), kspec(P_NK, 0), kspec(P_NK, 1), kspec(P_NK, 2), cspec(P_NK),
                  kspec(P_NV, 0), kspec(P_NV, 1), kspec(P_NV, 2), cspec(P_NV), qspec(P_NG),
                  pl.BlockSpec((None, None, tq, nk), lambda b, h, i: (h, btype(i), 0, 0))],
        out_specs=pl.BlockSpec((tq, LANE), lambda b, h, i: (b * rb + i, h)),
        out_shape=jax.ShapeDtypeStruct((p.shape[0], NA_HEADS * LANE), BF16),
        compiler_params=_params(("parallel", "parallel", "parallel")),
        name="natten_latent",
    )(p, p, p, p, p, p, p, p, p, p, bias)


def _merge_kernel(o0_ref, o1_ref, o2_ref, o3_ref, w0_ref, w1_ref, w2_ref, w3_ref,
                  m0_ref, m1_ref, m2_ref, m3_ref, y_ref):
    acc = None
    for o_ref, w_ref, m_ref in ((o0_ref, w0_ref, m0_ref), (o1_ref, w1_ref, m1_ref),
                                (o2_ref, w2_ref, m2_ref), (o3_ref, w3_ref, m3_ref)):
        gate = 1.0 / (1.0 + jnp.exp(-m_ref[...].astype(F32)))
        term = gate * jnp.dot(o_ref[...], w_ref[...], preferred_element_type=F32)
        acc = term if acc is None else acc + term
    y_ref[...] = acc.astype(y_ref.dtype)


def _merge(outs, weights, p, seq):
    t = p.shape[0]
    d = weights[0].shape[1]
    tm = _pick(seq, (768, 512, 256))
    tn = 512
    o_specs = [pl.BlockSpec((tm, o.shape[1]), lambda i, j: (i, 0)) for o in outs]
    w_specs = [pl.BlockSpec((w.shape[0], tn), lambda i, j: (0, j)) for w in weights]
    m_specs = [pl.BlockSpec((tm, tn), lambda i, j, b=b: (i, (P_MIX + b * d) // tn + j))
               for b in range(N_BRANCH)]
    return pl.pallas_call(
        _merge_kernel,
        grid=(t // tm, d // tn),
        in_specs=o_specs + w_specs + m_specs,
        out_specs=pl.BlockSpec((tm, tn), lambda i, j: (i, j)),
        out_shape=jax.ShapeDtypeStruct((t, d), BF16),
        compiler_params=_params(("parallel", "parallel")),
        name="branch_merge",
    )(*outs, *weights, p, p, p, p)


def _out_kernel(y_ref, w_ref, x_ref, nw_ref, gate_ref, o_ref, *, tm, tiles_per_batch, n_lat):
    z = jnp.dot(y_ref[...], w_ref[...], preferred_element_type=F32)
    zn = z * lax.rsqrt(jnp.mean(z * z, axis=-1, keepdims=True) + EPS) * nw_ref[...]
    is_ctx = _is_ctx_rows(pl.program_id(0), tm, tiles_per_batch, n_lat)
    gate = jnp.where(is_ctx, gate_ref[1:2, :], gate_ref[0:1, :])
    o_ref[...] = x_ref[...] + gate * zn


def _out_projection(y, w_out, xa, norm_w, gate, n_lat, seq):
    t, d = xa.shape
    tm = _pick(seq, (384, 256, 128))
    tpb = seq // tm
    kern = functools.partial(_out_kernel, tm=tm, tiles_per_batch=tpb, n_lat=n_lat)
    return pl.pallas_call(
        kern,
        grid=(t // tm,),
        in_specs=[pl.BlockSpec((tm, d), lambda i: (i, 0)),
                  pl.BlockSpec((d, d), lambda i: (0, 0)),
                  pl.BlockSpec((tm, d), lambda i: (i, 0)),
                  pl.BlockSpec((1, d), lambda i: (0, 0)),
                  pl.BlockSpec((None, 2, d), lambda i: (i // tpb, 0, 0))],
        out_specs=pl.BlockSpec((tm, d), lambda i: (i, 0)),
        out_shape=jax.ShapeDtypeStruct((t, d), F32),
        compiler_params=_params(("parallel",)),
        name="out_proj",
    )(y, w_out, xa, norm_w.reshape(1, d), gate)


def _pairs_apart(w, n_heads, dim):
    lead = w.shape[:-1]
    return w.reshape(lead + (n_heads, dim // 2, 2)).swapaxes(-1, -2).reshape(lead + (n_heads * dim,))


def _rope_tile(w):
    lead = w.shape[:-1]
    pr = w.reshape(lead + (MLA_ROPE // 2, 2))
    zero = jnp.zeros(lead + (MLA_ROPE // 2,), w.dtype)
    return jnp.concatenate([pr[..., 0], zero, pr[..., 1], zero], axis=-1)


def _layout_w_in(w):
    o = _OFF
    k = w.shape[0]

    def seg(name, width):
        return w[:, o[name]:o[name] + width]

    cols = [seg("mix", 8192), seg("z", 2048), seg("xbc", SSM_CONV_DIM),
            _pairs_apart(seg("gq", 1024), GQA_HEADS, HEAD_DIM), seg("gg", 1024),
            seg("nq", 1024), seg("nk", 1024), seg("nv", 1024), seg("ng", 1024), seg("mg", 1024),
            seg("mqa", MLA_Q_LORA), jnp.zeros((k, MQA_PAD - MLA_Q_LORA), w.dtype),
            _pairs_apart(seg("gk", 512), GQA_KV_HEADS, HEAD_DIM), seg("gv", 512),
            seg("mkva", MLA_KV_LORA), _rope_tile(seg("mkr", MLA_ROPE))]
    main = jnp.concatenate(cols, axis=1)
    main = jnp.pad(main, ((0, 0), (0, P_WIDTH - main.shape[1]))).astype(BF16)
    dtr = seg("dtr", 2 * SSM_HEADS)
    zero = jnp.zeros((k, LANE - SSM_HEADS), w.dtype)
    side = jnp.concatenate([dtr[:, :SSM_HEADS], zero, dtr[:, SSM_HEADS:], zero], axis=1).astype(BF16)
    return main, side


def _layout_w_uq(w_uq):
    k = w_uq.shape[0]
    w = w_uq.reshape(k, MLA_HEADS, MLA_NOPE + MLA_ROPE)
    w = jnp.concatenate([w[..., :MLA_NOPE], _rope_tile(w[..., MLA_NOPE:])], axis=-1)
    w = w.reshape(k, MLA_HEADS * MLA_QK_PAD)
    return jnp.pad(w, ((0, MQA_PAD - k), (0, 0))).astype(BF16)


def _rope_tables(n_lat, n_ctx, dim):
    t = np.arange(n_lat)
    quarter = dim // 4
    freqs = ROPE_THETA ** (-jnp.arange(quarter, dtype=F32) / quarter)
    row = jnp.asarray(t // GRID_W, F32)
    col = jnp.asarray(t % GRID_W, F32)
    ang = jnp.concatenate([row[:, None] * freqs, col[:, None] * freqs], axis=-1)
    cos, sin = jnp.cos(ang), jnp.sin(ang)
    pad = 64 - dim // 2
    one, zero = jnp.ones((n_lat, pad), F32), jnp.zeros((n_lat, pad), F32)
    cos_t = jnp.concatenate([cos, one, cos, one], axis=-1)
    sin_t = jnp.concatenate([-sin, zero, sin, zero], axis=-1)
    cos_t = jnp.concatenate([cos_t, jnp.ones((n_ctx, LANE), F32)], axis=0)
    sin_t = jnp.concatenate([sin_t, jnp.zeros((n_ctx, LANE), F32)], axis=0)
    return cos_t, sin_t


def _rows_select(lat, ctx, batch, n_lat, seq):
    w = lat.shape[-1]
    return jnp.concatenate([lat.reshape(batch, seq, w)[:, :n_lat],
                            ctx.reshape(batch, seq, w)[:, n_lat:]], axis=1).reshape(batch * seq, w)


def _layer(xa, cc, rope_g, rope_m, lp, batch, n_lat, seq):
    d = xa.shape[1]
    mod = _modulation(cc, lp["ada_w"], lp["ada_b"])

    def per_row(v):
        return jnp.stack([v[:batch], jnp.broadcast_to(v[batch:batch + 1], (batch, d))], axis=1)

    shift, scale, gate = (per_row(mod[:, k * d:(k + 1) * d]) for k in range(3))
    w_main, w_side = _layout_w_in(lp["w_in"])
    p, dtr = _in_projection(xa, lp["norm_pre"], scale, shift, w_main, w_side, n_lat, seq)

    u = _conv_silu(p, lp["conv_w"], lp["conv_b"], n_lat, seq)
    y = _ssd(u, dtr, lp["a_log"], lp["dt_bias"], batch, n_lat, seq)
    o_ssm = _ssm_out(y, u, p, lp["d_skip"], lp["ssm_norm"], seq)

    common = dict(batch=batch, n_lat=n_lat, seq=seq)
    qg = _head_norm_rope(p, P_GQ, GQA_HEADS, _pairs_apart(lp["gqa_q_norm"], 1, HEAD_DIM), *rope_g, seq)
    kg = _head_norm_rope(p, P_GK, GQA_KV_HEADS, _pairs_apart(lp["gqa_k_norm"], 1, HEAD_DIM), *rope_g, seq)
    o_gqa = _rows_select(*_attention(
        qg, kg, p, p, 256, q_col0=0, k_col0=0, v_col0=P_GV, g_col0=P_GG, v_stride=1,
        n_kv_heads=GQA_KV_HEADS, r=GQA_HEADS // GQA_KV_HEADS, dq=HEAD_DIM, scale=HEAD_DIM ** -0.5,
        **common), **common)

    na_lat = _natten(p, lp["na_bias"], **common)
    na_ctx = _flash(p, p, p, p, q_col0=P_NQ, k_col0=P_NK, v_col0=P_NV, g_col0=P_NG, v_stride=1,
                    n_kv_heads=NA_HEADS, r=1, dq=HEAD_DIM, scale=HEAD_DIM ** -0.5, ctx_only=True,
                    tq=256, **common)
    o_na = _rows_select(na_lat, na_ctx, **common)

    q_norm = jnp.pad(lp["mla_q_norm"], (0, MQA_PAD - MLA_Q_LORA)).reshape(1, MQA_PAD)
    qm = _mla_q(p, q_norm, _layout_w_uq(lp["w_uq"]), *rope_m, seq)
    km, vm = _mla_kv(p, lp["mla_kv_norm"], lp["w_ukv"].astype(BF16), *rope_m, seq)
    o_mla = _rows_select(*_attention(
        qm, km, vm, p, 512, q_col0=0, k_col0=0, v_col0=0, g_col0=P_MG, v_stride=1,
        n_kv_heads=MLA_HEADS, r=1, dq=MLA_QK_PAD, scale=(MLA_NOPE + MLA_ROPE) ** -0.5,
        **common), **common)

    weights = [lp[n].astype(BF16) for n in ("w_o_ssm", "w_o_gqa", "w_o_na", "w_o_mla")]
    ymix = _merge([o_ssm, o_gqa, o_na, o_mla], weights, p, seq)
    return _out_projection(ymix, lp["w_out"].astype(BF16), xa, lp["norm_post"], gate, n_lat, seq)


def kernel(x, c, ctx, c_ctx, ada_w, ada_b, norm_pre, norm_post, w_in, conv_w, conv_b, a_log, dt_bias,
           d_skip, ssm_norm, w_o_ssm, gqa_q_norm, gqa_k_norm, w_o_gqa, na_rpb, w_o_na, mla_q_norm,
           w_uq, mla_kv_norm, w_ukv, w_o_mla, w_out):
    batch, n_lat, d = x.shape
    n_ctx = ctx.shape[1]
    seq = n_lat + n_ctx
    stacked = dict(ada_w=ada_w, ada_b=ada_b, norm_pre=norm_pre, norm_post=norm_post, w_in=w_in,
                   conv_w=conv_w, conv_b=conv_b, a_log=a_log, dt_bias=dt_bias, d_skip=d_skip,
                   ssm_norm=ssm_norm, w_o_ssm=w_o_ssm, gqa_q_norm=gqa_q_norm, gqa_k_norm=gqa_k_norm,
                   w_o_gqa=w_o_gqa, na_bias=_na_bias_table(na_rpb), w_o_na=w_o_na,
                   mla_q_norm=mla_q_norm, w_uq=w_uq,
                   mla_kv_norm=mla_kv_norm, w_ukv=w_ukv, w_o_mla=w_o_mla, w_out=w_out)
    xa = jnp.concatenate([x, ctx], axis=1).reshape(batch * seq, d)
    cc = jnp.concatenate([c, c_ctx[None, :], jnp.zeros((8 - batch - 1, d), c.dtype)], axis=0)
    rope_g = _rope_tables(n_lat, n_ctx, HEAD_DIM)
    rope_m = _rope_tables(n_lat, n_ctx, MLA_ROPE)
    for layer in range(ada_w.shape[0]):
        lp = {k: v[layer] for k, v in stacked.items()}
        xa = _layer(xa, cc, rope_g, rope_m, lp, batch, n_lat, seq)
    return xa.reshape(batch, seq, d)[:, :n_lat]
```

```python
import functools
import math

import jax
import jax.numpy as jnp
import numpy as np
from jax import lax
from jax.experimental import pallas as pl
from jax.experimental.pallas import tpu as pltpu

F32 = jnp.float32
BF16 = jnp.bfloat16

GRID_W = 64
EPS = 1e-6
ROPE_THETA = 10000.0

SSM_HEADS = 32
SSM_HEAD_DIM = 64
SSM_INNER = SSM_HEADS * SSM_HEAD_DIM
SSM_GROUPS = 4
SSM_STATE = 128
SSM_CONV = 5
SSM_CHUNK = 128
SSM_CONV_DIM = SSM_INNER + 2 * SSM_GROUPS * SSM_STATE

GQA_HEADS = 8
GQA_KV_HEADS = 4
HEAD_DIM = 128
NA_HEADS = 8
NA_WIN_H = 8
NA_WIN_W = 16
NA_QROWS = 4
NA_KROWS = NA_QROWS + NA_WIN_H
NA_HEADS_PER_STEP = 4

MLA_HEADS = 8
MLA_Q_LORA = 768
MLA_KV_LORA = 512
MLA_NOPE = 128
MLA_ROPE = 64
MLA_QK_PAD = 256

N_BRANCH = 4
LANE = 128
VMEM_LIMIT = 56 * 1024 * 1024
NEG_BIG = -1e30
FLASH_TK = 256

_SIZES = (SSM_INNER, SSM_CONV_DIM, 2 * SSM_HEADS, 1024, 512, 512, 1024, 1024, 1024, 1024, 1024,
          MLA_Q_LORA, MLA_KV_LORA, MLA_ROPE, 1024, N_BRANCH * 2048)
_OFF = dict(zip(("z", "xbc", "dtr", "gq", "gk", "gv", "gg", "nq", "nk", "nv", "ng",
                 "mqa", "mkva", "mkr", "mg", "mix"), np.cumsum((0,) + _SIZES[:-1]).tolist()))

P_MIX, P_Z, P_XBC = 0, 8192, 10240
P_GQ, P_GG, P_NQ, P_NK, P_NV, P_NG, P_MG, P_MQA = (13312, 14336, 15360, 16384, 17408, 18432,
                                                    19456, 20480)
P_GK, P_GV, P_MKVA, P_MKR = 21504, 22016, 22528, 23040
P_WIDTH = 23552
MQA_PAD = 1024


def _pick(n, candidates):
    for c in candidates:
        if n % c == 0:
            return c
    raise ValueError(f"no tile for {n} among {candidates}")


def _params(sem):
    return pltpu.CompilerParams(dimension_semantics=sem, vmem_limit_bytes=VMEM_LIMIT)


def _silu(v):
    return v * (1.0 / (1.0 + jnp.exp(-v)))


def _rope(v, cos, sin):
    return v * cos + pltpu.roll(v, 64, axis=1) * sin


def _mod_kernel(c_ref, w_ref, b_ref, o_ref):
    h = _silu(c_ref[...]).astype(BF16)
    o_ref[...] = jnp.dot(h, w_ref[...].astype(BF16), preferred_element_type=F32) + b_ref[...]


def _modulation(cc, ada_w, ada_b):
    rows, d = cc.shape
    n = ada_w.shape[1]
    tn = _pick(n, (512, 256, 128))
    return pl.pallas_call(
        _mod_kernel,
        grid=(n // tn,),
        in_specs=[pl.BlockSpec((rows, d), lambda j: (0, 0)),
                  pl.BlockSpec((d, tn), lambda j: (0, j)),
                  pl.BlockSpec((1, tn), lambda j: (0, j))],
        out_specs=pl.BlockSpec((rows, tn), lambda j: (0, j)),
        out_shape=jax.ShapeDtypeStruct((rows, n), F32),
        compiler_params=_params(("parallel",)),
        name="adaln_mod",
    )(cc, ada_w, ada_b.reshape(1, n))


def _is_ctx_rows(i, tm, tiles_per_batch, n_lat):
    row = (i % tiles_per_batch) * tm + lax.broadcasted_iota(jnp.int32, (tm, 1), 0)
    return row >= n_lat


def _inproj_kernel(x_ref, nw_ref, sc_ref, sh_ref, w_ref, ws_ref, o_ref, os_ref, h_ref, *,
                   tm, tiles_per_batch, n_lat):
    i = pl.program_id(0)

    @pl.when(pl.program_id(1) == 0)
    def _():
        x = x_ref[...]
        y = x * lax.rsqrt(jnp.mean(x * x, axis=-1, keepdims=True) + EPS) * nw_ref[...]
        is_ctx = _is_ctx_rows(i, tm, tiles_per_batch, n_lat)
        sc = jnp.where(is_ctx, sc_ref[1:2, :], sc_ref[0:1, :])
        sh = jnp.where(is_ctx, sh_ref[1:2, :], sh_ref[0:1, :])
        h = (y * (1.0 + sc) + sh).astype(BF16)
        h_ref[...] = h
        os_ref[...] = jnp.dot(h, ws_ref[...], preferred_element_type=F32)

    o_ref[...] = jnp.dot(h_ref[...], w_ref[...], preferred_element_type=F32).astype(o_ref.dtype)


def _in_projection(xa, norm_w, scale, shift, w_p, w_side, n_lat, seq):
    t, d = xa.shape
    tm = _pick(seq, (768, 512, 256))
    tn = _pick(P_WIDTH, (1024, 512))
    tpb = seq // tm
    kern = functools.partial(_inproj_kernel, tm=tm, tiles_per_batch=tpb, n_lat=n_lat)
    return pl.pallas_call(
        kern,
        grid=(t // tm, P_WIDTH // tn),
        in_specs=[pl.BlockSpec((tm, d), lambda i, j: (i, 0)),
                  pl.BlockSpec((1, d), lambda i, j: (0, 0)),
                  pl.BlockSpec((None, 2, d), lambda i, j: (i // tpb, 0, 0)),
                  pl.BlockSpec((None, 2, d), lambda i, j: (i // tpb, 0, 0)),
                  pl.BlockSpec((d, tn), lambda i, j: (0, j)),
                  pl.BlockSpec((d, 2 * LANE), lambda i, j: (0, 0))],
        out_specs=[pl.BlockSpec((tm, tn), lambda i, j: (i, j)),
                   pl.BlockSpec((tm, 2 * LANE), lambda i, j: (i, 0))],
        out_shape=[jax.ShapeDtypeStruct((t, P_WIDTH), BF16),
                   jax.ShapeDtypeStruct((t, 2 * LANE), F32)],
        scratch_shapes=[pltpu.VMEM((tm, d), BF16)],
        compiler_params=_params(("parallel", "arbitrary")),
        name="in_proj",
    )(xa, norm_w.reshape(1, d), scale, shift, w_p, w_side)


def _conv_kernel(u_ref, prev_ref, next_ref, w_ref, b_ref, o_ref, ext_ref, *,
                 tm, tiles_per_batch, lat_tiles):
    ib = pl.program_id(0) % tiles_per_batch
    first = jnp.logical_or(ib == 0, ib == lat_tiles)
    last = jnp.logical_or(ib == lat_tiles - 1, ib == tiles_per_batch - 1)
    ext_ref[0:8, :] = jnp.where(first, 0.0, prev_ref[...].astype(F32))
    ext_ref[8:8 + tm, :] = u_ref[...].astype(F32)
    ext_ref[8 + tm:16 + tm, :] = jnp.where(last, 0.0, next_ref[...].astype(F32))
    acc = b_ref[...] + w_ref[0:1, :] * ext_ref[pl.ds(8 - SSM_CONV // 2, tm), :]
    for k in range(1, SSM_CONV):
        acc = acc + w_ref[k:k + 1, :] * ext_ref[pl.ds(8 - SSM_CONV // 2 + k, tm), :]
    o_ref[...] = _silu(acc).astype(o_ref.dtype)


def _conv_silu(p, conv_w, conv_b, n_lat, seq):
    t = p.shape[0]
    tm = 256
    tc = 1024
    tpb = seq // tm
    n_row8 = t // 8
    col0 = P_XBC // tc
    kern = functools.partial(_conv_kernel, tm=tm, tiles_per_batch=tpb, lat_tiles=n_lat // tm)
    return pl.pallas_call(
        kern,
        grid=(t // tm, SSM_CONV_DIM // tc),
        in_specs=[pl.BlockSpec((tm, tc), lambda i, j: (i, col0 + j)),
                  pl.BlockSpec((8, tc), lambda i, j: (jnp.maximum(i * (tm // 8) - 1, 0), col0 + j)),
                  pl.BlockSpec((8, tc), lambda i, j: (jnp.minimum((i + 1) * (tm // 8), n_row8 - 1),
                                                      col0 + j)),
                  pl.BlockSpec((8, tc), lambda i, j: (0, j)),
                  pl.BlockSpec((1, tc), lambda i, j: (0, j))],
        out_specs=pl.BlockSpec((tm, tc), lambda i, j: (i, j)),
        out_shape=jax.ShapeDtypeStruct((t, SSM_CONV_DIM), BF16),
        scratch_shapes=[pltpu.VMEM((tm + 16, tc), F32)],
        compiler_params=_params(("parallel", "parallel")),
        name="ssm_conv",
    )(p, p, p, jnp.pad(conv_w, ((0, 8 - SSM_CONV), (0, 0))), conv_b.reshape(1, -1))


def _split_dot(a_bf16, v):
    v1 = v.astype(BF16)
    r1 = v - v1.astype(F32)
    v2 = r1.astype(BF16)
    v3 = (r1 - v2.astype(F32)).astype(BF16)
    dot = functools.partial(jnp.dot, preferred_element_type=F32)
    return dot(a_bf16, v1) + dot(a_bf16, v2) + dot(a_bf16, v3)


def _ssd_kernel(u_ref, dtr_ref, alog_ref, bias_ref, eh_ref, y_ref, st_ref):
    cl = SSM_CHUNK
    fwd = pl.program_id(1) == 0

    @pl.when(pl.program_id(2) == 0)
    def _():
        st_ref[...] = jnp.zeros_like(st_ref)

    raw = dtr_ref[...] + bias_ref[...]
    dt = jnp.maximum(raw, 0.0) + jnp.log(1.0 + jnp.exp(-jnp.abs(raw)))
    da = dt * (-jnp.exp(alog_ref[...]))
    r = lax.broadcasted_iota(jnp.int32, (cl, cl), 0)
    c = lax.broadcasted_iota(jnp.int32, (cl, cl), 1)
    tri = jnp.where(fwd, r - c, c - r) >= 0
    acs = _split_dot(jnp.where(tri, 1.0, 0.0).astype(BF16), da)
    total = jnp.where(fwd, acs[cl - 1:cl, :], acs[0:1, :])
    e_acs = jnp.exp(acs)
    w_end = dt * jnp.exp(total - acs)
    acs_t = acs.T
    dt_t = dt.T

    def hi_lo(v):
        hi = v.astype(BF16)
        return hi, (v - hi.astype(F32)).astype(BF16)

    stack = jnp.concatenate(hi_lo(w_end) + hi_lo(e_acs), axis=0)
    ex = jnp.dot(stack, eh_ref[...], preferred_element_type=F32)
    w_exp = ex[0:cl] + ex[cl:2 * cl]
    e_exp = ex[2 * cl:3 * cl] + ex[3 * cl:4 * cl]
    dec = jnp.where(fwd, e_exp[cl - 1:cl, :], e_exp[0:1, :])
    xw = (u_ref[:, 0:SSM_INNER].astype(F32) * w_exp).astype(BF16)
    lane = lax.broadcasted_iota(jnp.int32, (cl, LANE), 1)
    gw = SSM_INNER // SSM_GROUPS
    hpg = SSM_HEADS // SSM_GROUPS
    for g in range(SSM_GROUPS):
        b_g = u_ref[:, SSM_INNER + g * SSM_STATE:SSM_INNER + (g + 1) * SSM_STATE]
        c_off = SSM_INNER + SSM_GROUPS * SSM_STATE
        c_g = u_ref[:, c_off + g * SSM_STATE:c_off + (g + 1) * SSM_STATE]
        cb = lax.dot_general(c_g, b_g, (((1,), (1,)), ((), ())), preferred_element_type=F32)
        st_g = st_ref[:, g * gw:(g + 1) * gw]
        y_off = jnp.dot(c_g, st_g.astype(BF16), preferred_element_type=F32)
        ys = []
        for k in range(hpg // 2):
            ms = []
            for h in (g * hpg + 2 * k, g * hpg + 2 * k + 1):
                seg = acs[:, h:h + 1] - acs_t[h:h + 1, :]
                dec_h = jnp.exp(jnp.where(tri, seg, NEG_BIG))
                ms.append((dec_h * cb * dt_t[h:h + 1, :]).astype(BF16))
            x_p = u_ref[:, g * gw + k * LANE:g * gw + (k + 1) * LANE]
            zero = jnp.zeros_like(x_p)
            rhs = jnp.concatenate([jnp.where(lane < SSM_HEAD_DIM, x_p, zero),
                                   jnp.where(lane >= SSM_HEAD_DIM, x_p, zero)], axis=0)
            ys.append(jnp.dot(jnp.concatenate(ms, axis=1), rhs, preferred_element_type=F32))
        y_g = jnp.concatenate(ys, axis=1) + y_off * e_exp[:, g * gw:(g + 1) * gw]
        y_ref[:, g * gw:(g + 1) * gw] = y_g.astype(y_ref.dtype)
        upd = lax.dot_general(b_g, xw[:, g * gw:(g + 1) * gw], (((0,), (0,)), ((), ())),
                              preferred_element_type=F32)
        st_ref[:, g * gw:(g + 1) * gw] = st_g * dec[:, g * gw:(g + 1) * gw] + upd


def _ssd(u, dtr, a_log, dt_bias, batch, n_lat, seq):
    cl = SSM_CHUNK
    nch, nlat, nctx = seq // cl, n_lat // cl, (seq - n_lat) // cl

    def chunk(d, c):
        f = jnp.where(c < nctx, nlat + c, c - nctx)
        b = jnp.where(c < nctx, nlat + nctx - 1 - c, nlat - 1 - (c - nctx))
        return jnp.where(d == 0, f, b)

    def pad_heads(v):
        return jnp.pad(v, ((0, 0), (0, LANE - SSM_HEADS))).reshape(2, 1, LANE)

    eh = np.zeros((LANE, SSM_INNER), np.float32)
    for h in range(SSM_HEADS):
        eh[h, h * SSM_HEAD_DIM:(h + 1) * SSM_HEAD_DIM] = 1.0
    return pl.pallas_call(
        _ssd_kernel,
        grid=(batch, 2, nch),
        in_specs=[pl.BlockSpec((cl, SSM_CONV_DIM), lambda b, d, c: (b * nch + chunk(d, c), 0)),
                  pl.BlockSpec((cl, LANE), lambda b, d, c: (b * nch + chunk(d, c), d)),
                  pl.BlockSpec((None, 1, LANE), lambda b, d, c: (d, 0, 0)),
                  pl.BlockSpec((None, 1, LANE), lambda b, d, c: (d, 0, 0)),
                  pl.BlockSpec((LANE, SSM_INNER), lambda b, d, c: (0, 0))],
        out_specs=pl.BlockSpec((None, cl, SSM_INNER), lambda b, d, c: (d, b * nch + chunk(d, c), 0)),
        out_shape=jax.ShapeDtypeStruct((2, u.shape[0], SSM_INNER), BF16),
        scratch_shapes=[pltpu.VMEM((SSM_STATE, SSM_INNER), F32)],
        compiler_params=_params(("parallel", "parallel", "arbitrary")),
        name="ssd_scan",
    )(u, dtr, pad_heads(a_log), pad_heads(dt_bias), jnp.asarray(eh, BF16))


def _ssm_out_kernel(yf_ref, yb_ref, x_ref, z_ref, skip_ref, nw_ref, o_ref):
    z = z_ref[...].astype(F32)
    g = (yf_ref[...].astype(F32) + yb_ref[...].astype(F32)
         + skip_ref[...] * x_ref[...].astype(F32)) * _silu(z)
    gw = SSM_INNER // SSM_GROUPS
    for k in range(SSM_GROUPS):
        gk = g[:, k * gw:(k + 1) * gw]
        gk = gk * lax.rsqrt(jnp.mean(gk * gk, axis=-1, keepdims=True) + EPS)
        o_ref[:, k * gw:(k + 1) * gw] = (gk * nw_ref[:, k * gw:(k + 1) * gw]).astype(o_ref.dtype)


def _ssm_out(y, u, p, d_skip, ssm_norm, seq):
    t = u.shape[0]
    tm = _pick(seq, (384, 256, 128))
    w = SSM_INNER
    return pl.pallas_call(
        _ssm_out_kernel,
        grid=(t // tm,),
        in_specs=[pl.BlockSpec((None, tm, w), lambda i: (0, i, 0)),
                  pl.BlockSpec((None, tm, w), lambda i: (1, i, 0)),
                  pl.BlockSpec((tm, w), lambda i: (i, 0)),
                  pl.BlockSpec((tm, w), lambda i: (i, P_Z // w)),
                  pl.BlockSpec((1, w), lambda i: (0, 0)),
                  pl.BlockSpec((1, w), lambda i: (0, 0))],
        out_specs=pl.BlockSpec((tm, w), lambda i: (i, 0)),
        out_shape=jax.ShapeDtypeStruct((t, w), BF16),
        compiler_params=_params(("parallel",)),
        name="ssm_gated_norm",
    )(y, y, u, p, jnp.repeat(d_skip, SSM_HEAD_DIM).reshape(1, w), ssm_norm.reshape(1, w))


def _head_norm_rope_kernel(x_ref, nw_ref, cos_ref, sin_ref, o_ref, *, n_heads):
    for h in range(n_heads):
        sl = slice(h * LANE, (h + 1) * LANE)
        x = x_ref[:, sl].astype(F32)
        y = x * lax.rsqrt(jnp.mean(x * x, axis=-1, keepdims=True) + EPS) * nw_ref[...]
        o_ref[:, sl] = _rope(y, cos_ref[...], sin_ref[...]).astype(o_ref.dtype)


def _head_norm_rope(p, col0, n_heads, norm_w, cos, sin, seq):
    t = p.shape[0]
    tm = _pick(seq, (768, 512, 256))
    tpb = seq // tm
    w = n_heads * LANE
    return pl.pallas_call(
        functools.partial(_head_norm_rope_kernel, n_heads=n_heads),
        grid=(t // tm,),
        in_specs=[pl.BlockSpec((tm, w), lambda i: (i, col0 // w)),
                  pl.BlockSpec((1, LANE), lambda i: (0, 0)),
                  pl.BlockSpec((tm, LANE), lambda i: (i % tpb, 0)),
                  pl.BlockSpec((tm, LANE), lambda i: (i % tpb, 0))],
        out_specs=pl.BlockSpec((tm, w), lambda i: (i, 0)),
        out_shape=jax.ShapeDtypeStruct((t, w), BF16),
        compiler_params=_params(("parallel",)),
        name="head_norm_rope",
    )(p, norm_w.reshape(1, LANE), cos, sin)


def _mla_q_kernel(x_ref, nw_ref, w_ref, cos_ref, sin_ref, o_ref):
    x = x_ref[...].astype(F32)
    ms = jnp.sum(x * x, axis=-1, keepdims=True) * (1.0 / MLA_Q_LORA)
    h = (x * lax.rsqrt(ms + EPS) * nw_ref[...]).astype(BF16)
    for hd in range(MLA_HEADS):
        c0 = hd * MLA_QK_PAD
        res = jnp.dot(h, w_ref[:, c0:c0 + MLA_QK_PAD], preferred_element_type=F32)
        o_ref[:, c0:c0 + LANE] = res[:, 0:LANE].astype(o_ref.dtype)
        o_ref[:, c0 + LANE:c0 + MLA_QK_PAD] = _rope(res[:, LANE:], cos_ref[...],
                                                    sin_ref[...]).astype(o_ref.dtype)


def _mla_q(p, q_norm, w_uq, cos, sin, seq):
    t = p.shape[0]
    tm = _pick(seq, (768, 512, 256))
    tpb = seq // tm
    kdim = MQA_PAD
    wo = MLA_HEADS * MLA_QK_PAD
    return pl.pallas_call(
        _mla_q_kernel,
        grid=(t // tm,),
        in_specs=[pl.BlockSpec((tm, kdim), lambda i: (i, P_MQA // kdim)),
                  pl.BlockSpec((1, kdim), lambda i: (0, 0)),
                  pl.BlockSpec((kdim, wo), lambda i: (0, 0)),
                  pl.BlockSpec((tm, LANE), lambda i: (i % tpb, 0)),
                  pl.BlockSpec((tm, LANE), lambda i: (i % tpb, 0))],
        out_specs=pl.BlockSpec((tm, wo), lambda i: (i, 0)),
        out_shape=jax.ShapeDtypeStruct((t, wo), BF16),
        compiler_params=_params(("parallel",)),
        name="mla_q_up",
    )(p, q_norm, w_uq, cos, sin)


def _mla_kv_kernel(x_ref, kr_ref, nw_ref, w_ref, cos_ref, sin_ref, k_ref, v_ref):
    x = x_ref[...].astype(F32)
    h = (x * lax.rsqrt(jnp.mean(x * x, axis=-1, keepdims=True) + EPS) * nw_ref[...]).astype(BF16)
    k_rope = _rope(kr_ref[...].astype(F32), cos_ref[...], sin_ref[...]).astype(k_ref.dtype)
    for hd in range(MLA_HEADS):
        c0 = hd * MLA_QK_PAD
        res = jnp.dot(h, w_ref[:, c0:c0 + 2 * LANE], preferred_element_type=F32)
        k_ref[:, c0:c0 + LANE] = res[:, 0:LANE].astype(k_ref.dtype)
        k_ref[:, c0 + LANE:c0 + MLA_QK_PAD] = k_rope
        v_ref[:, hd * LANE:(hd + 1) * LANE] = res[:, LANE:].astype(v_ref.dtype)


def _mla_kv(p, kv_norm, w_ukv, cos, sin, seq):
    t = p.shape[0]
    tm = _pick(seq, (768, 512, 256))
    tpb = seq // tm
    kdim = MLA_KV_LORA
    wk = MLA_HEADS * MLA_QK_PAD
    return pl.pallas_call(
        _mla_kv_kernel,
        grid=(t // tm,),
        in_specs=[pl.BlockSpec((tm, kdim), lambda i: (i, P_MKVA // kdim)),
                  pl.BlockSpec((tm, LANE), lambda i: (i, P_MKR // LANE)),
                  pl.BlockSpec((1, kdim), lambda i: (0, 0)),
                  pl.BlockSpec((kdim, wk), lambda i: (0, 0)),
                  pl.BlockSpec((tm, LANE), lambda i: (i % tpb, 0)),
                  pl.BlockSpec((tm, LANE), lambda i: (i % tpb, 0))],
        out_specs=[pl.BlockSpec((tm, wk), lambda i: (i, 0)),
                   pl.BlockSpec((tm, MLA_HEADS * LANE), lambda i: (i, 0))],
        out_shape=[jax.ShapeDtypeStruct((t, wk), BF16),
                   jax.ShapeDtypeStruct((t, MLA_HEADS * LANE), BF16)],
        compiler_params=_params(("parallel",)),
        name="mla_kv_up",
    )(p, p, kv_norm.reshape(1, kdim), w_ukv, cos, sin)


def _flash_kernel(q_ref, k_ref, v_ref, g_ref, *rest, r, tq, dq, tk, n_keys, scale):
    o_ref = rest[-1]
    q = jnp.concatenate([q_ref[:, j * dq:(j + 1) * dq] for j in range(r)], axis=0)
    q = (q.astype(F32) * (scale * math.log2(math.e))).astype(BF16)
    m_rows = r * tq
    m = jnp.full((m_rows, 1), NEG_BIG, F32)
    l = jnp.zeros((m_rows, LANE), F32)
    acc = jnp.zeros((m_rows, LANE), F32)
    for c in range(n_keys // tk):
        k_c = k_ref[c * tk:(c + 1) * tk, :]
        v_c = v_ref[c * tk:(c + 1) * tk, :]
        s = lax.dot_general(q, k_c, (((1,), (1,)), ((), ())), preferred_element_type=F32)
        m_new = jnp.maximum(m, jnp.max(s, axis=-1, keepdims=True))
        alpha = jnp.exp2(m - m_new)
        pr = jnp.exp2(s - m_new)
        part = pr[:, 0:LANE]
        for j in range(1, tk // LANE):
            part = part + pr[:, j * LANE:(j + 1) * LANE]
        l = alpha * l + part
        acc = alpha * acc + jnp.dot(pr.astype(BF16), v_c, preferred_element_type=F32)
        m = m_new
    o = acc / jnp.sum(l, axis=-1, keepdims=True)
    for j in range(r):
        gate = _silu(g_ref[:, j * LANE:(j + 1) * LANE].astype(F32))
        o_ref[:, j * LANE:(j + 1) * LANE] = (o[j * tq:(j + 1) * tq] * gate).astype(o_ref.dtype)


def _flash(q, k, v, gates, *, q_col0, k_col0, v_col0, g_col0, v_stride, n_kv_heads, r, dq, scale,
           batch, n_lat, seq, ctx_only, tq, into=None):
    n_ctx = seq - n_lat
    tk = FLASH_TK
    if ctx_only:
        tq = n_ctx
        q_blk0, n_q, kv_rows, kv_blk0 = n_lat // tq, 1, n_ctx, n_lat // n_ctx
    else:
        q_blk0, n_q, kv_rows, kv_blk0 = 0, n_lat // tq, seq, 0
    assert kv_rows % tk == 0
    kern = functools.partial(_flash_kernel, r=r, tq=tq, dq=dq, tk=tk, n_keys=kv_rows, scale=scale)
    w_out = n_kv_heads * r * LANE
    args = [a.reshape(batch, seq, a.shape[-1]) for a in (q, k, v, gates)]
    in_specs = [pl.BlockSpec((None, tq, r * dq), lambda b, g, i: (b, q_blk0 + i, q_col0 // (r * dq) + g)),
                pl.BlockSpec((None, kv_rows, dq), lambda b, g, i: (b, kv_blk0, k_col0 // dq + g)),
                pl.BlockSpec((None, kv_rows, LANE),
                             lambda b, g, i: (b, kv_blk0, v_col0 // LANE + v_stride * g)),
                pl.BlockSpec((None, tq, r * LANE),
                             lambda b, g, i: (b, q_blk0 + i, g_col0 // (r * LANE) + g))]
    aliases = {}
    if into is not None:
        args.append(into.reshape(batch, seq, w_out))
        in_specs.append(pl.BlockSpec(memory_space=pl.ANY))
        aliases = {len(args) - 1: 0}
    out = pl.pallas_call(
        kern,
        grid=(batch, n_kv_heads, n_q),
        in_specs=in_specs,
        out_specs=pl.BlockSpec((None, tq, r * LANE), lambda b, g, i: (b, q_blk0 + i, g)),
        out_shape=jax.ShapeDtypeStruct((batch, seq, w_out), BF16),
        input_output_aliases=aliases,
        compiler_params=_params(("parallel", "parallel", "parallel")),
        name="flash_ctx" if ctx_only else "flash_latent",
    )(*args)
    return out.reshape(batch * seq, w_out)


def _attention(q, k, v, gates, tq, **kw):
    lat = _flash(q, k, v, gates, ctx_only=False, tq=tq, **kw)
    return _flash(q, k, v, gates, ctx_only=True, tq=tq, into=lat, **kw)


def _natten_kernel(q_ref, k0_ref, k1_ref, k2_ref, kc_ref, v0_ref, v1_ref, v2_ref, vc_ref, g_ref,
                   bias_ref, o_ref, *, scale, n_heads):
    n_loc = bias_ref.shape[-1]
    for h in range(n_heads):
        sl = slice(h * LANE, (h + 1) * LANE)
        q = (q_ref[:, sl].astype(F32) * (scale * math.log2(math.e))).astype(BF16)
        k = jnp.concatenate([k0_ref[:, sl], k1_ref[:, sl], k2_ref[:, sl], kc_ref[:, sl]], axis=0)
        v = jnp.concatenate([v0_ref[:, sl], v1_ref[:, sl], v2_ref[:, sl], vc_ref[:, sl]], axis=0)
        s = lax.dot_general(q, k, (((1,), (1,)), ((), ())), preferred_element_type=F32)
        s = jnp.concatenate([s[:, :n_loc] + bias_ref[h], s[:, n_loc:]], axis=1)
        pr = jnp.exp2(s - jnp.max(s, axis=-1, keepdims=True))
        l = jnp.sum(pr, axis=-1, keepdims=True)
        o = jnp.dot(pr.astype(BF16), v, preferred_element_type=F32) / l
        o_ref[:, sl] = (o * _silu(g_ref[:, sl].astype(F32))).astype(o_ref.dtype)


def _na_bias_table(rpb):
    qx = np.arange(GRID_W)
    c0 = np.clip(qx - NA_WIN_W // 2, 0, GRID_W - NA_WIN_W)
    col_ok = (qx[None, :] >= c0[:, None]) & (qx[None, :] < c0[:, None] + NA_WIN_W)
    dx = qx[None, :] - qx[:, None] + NA_WIN_W - 1
    pick = np.zeros((2 * NA_WIN_W - 1, GRID_W * GRID_W), np.float32)
    qi, ki = np.nonzero(col_ok)
    pick[dx[qi, ki], qi * GRID_W + ki] = 1.0
    by_dx = jnp.einsum("...d,dn->...n", rpb, pick, precision=lax.Precision.HIGHEST)
    qy, ky = np.arange(NA_QROWS), np.arange(NA_KROWS)
    dys, oks = [], []
    for q_off, first_key in ((0, None), (NA_WIN_H // 2, "q"), (NA_WIN_H, NA_KROWS - NA_WIN_H)):
        r0 = qy if first_key == "q" else np.full_like(qy, 0 if first_key is None else first_key)
        oks.append((ky[None, :] >= r0[:, None]) & (ky[None, :] < r0[:, None] + NA_WIN_H))
        dys.append(np.clip(ky[None, :] - (qy[:, None] + q_off) + NA_WIN_H - 1, 0, 2 * NA_WIN_H - 2))
    rows = jnp.take(by_dx, np.stack(dys).reshape(-1), axis=-2)
    lead = rows.shape[:-2]
    nl = len(lead)
    rows = rows.reshape(lead + (3, NA_QROWS, NA_KROWS, GRID_W, GRID_W))
    rows = rows.transpose(tuple(range(nl)) + (nl, nl + 1, nl + 3, nl + 2, nl + 4))
    ok = np.stack(oks)[:, :, None, :, None] & col_ok[None, None, :, None, :]
    tab = jnp.where(jnp.asarray(ok), rows * math.log2(math.e), NEG_BIG)
    return tab.reshape(lead + (3, NA_QROWS * GRID_W, NA_KROWS * GRID_W))


def _natten(p, bias, batch, n_lat, seq):
    n_ctx = seq - n_lat
    tq = NA_QROWS * GRID_W
    n_blk = n_lat // tq
    assert n_blk >= 3 and n_lat % tq == 0 and n_ctx % tq == 0
    rb = seq // tq
    cb = seq // n_ctx
    nk = NA_KROWS * GRID_W
    hps = NA_HEADS_PER_STEP
    hw = hps * LANE

    def kblk(i):
        return jnp.clip(i - 1, 0, n_blk - 3)

    def kspec(col0, j):
        return pl.BlockSpec((tq, hw), lambda b, h, i: (b * rb + kblk(i) + j, col0 // hw + h))

    def cspec(col0):
        return pl.BlockSpec((n_ctx, hw), lambda b, h, i: (b * cb + n_lat // n_ctx, col0 // hw + h))

    def qspec(col0):
        return pl.BlockSpec((tq, hw), lambda b, h, i: (b * rb + i, col0 // hw + h))

    def btype(i):
        return jnp.where(i == 0, 0, jnp.where(i == n_blk - 1, 2, 1))

    return pl.pallas_call(
        functools.partial(_natten_kernel, scale=HEAD_DIM ** -0.5, n_heads=hps),
        grid=(batch, NA_HEADS // hps, n_blk),
        in_specs=[qspec(P_NQ), kspec(P_NK, 0), kspec(P_NK, 1), kspec(P_NK, 2), cspec(P_NK),
                  kspec(P_NV, 0), kspec(P_NV, 1), kspec(P_NV, 2), cspec(P_NV), qspec(P_NG),
                  pl.BlockSpec((hps, None, tq, nk), lambda b, h, i: (h, btype(i), 0, 0))],
        out_specs=pl.BlockSpec((tq, hw), lambda b, h, i: (b * rb + i, h)),
        out_shape=jax.ShapeDtypeStruct((p.shape[0], NA_HEADS * LANE), BF16),
        compiler_params=_params(("parallel", "parallel", "parallel")),
        name="natten_latent",
    )(p, p, p, p, p, p, p, p, p, p, bias)


def _merge_kernel(o0_ref, o1_ref, o2_ref, o3_ref, w0_ref, w1_ref, w2_ref, w3_ref,
                  m0_ref, m1_ref, m2_ref, m3_ref, y_ref):
    acc = None
    for o_ref, w_ref, m_ref in ((o0_ref, w0_ref, m0_ref), (o1_ref, w1_ref, m1_ref),
                                (o2_ref, w2_ref, m2_ref), (o3_ref, w3_ref, m3_ref)):
        gate = 1.0 / (1.0 + jnp.exp(-m_ref[...].astype(F32)))
        term = gate * jnp.dot(o_ref[...], w_ref[...], preferred_element_type=F32)
        acc = term if acc is None else acc + term
    y_ref[...] = acc.astype(y_ref.dtype)


def _merge(outs, weights, p, seq):
    t = p.shape[0]
    d = weights[0].shape[1]
    tm = _pick(seq, (768, 512, 256))
    tn = 512
    o_specs = [pl.BlockSpec((tm, o.shape[1]), lambda i, j: (i, 0)) for o in outs]
    w_specs = [pl.BlockSpec((w.shape[0], tn), lambda i, j: (0, j)) for w in weights]
    m_specs = [pl.BlockSpec((tm, tn), lambda i, j, b=b: (i, (P_MIX + b * d) // tn + j))
               for b in range(N_BRANCH)]
    return pl.pallas_call(
        _merge_kernel,
        grid=(t // tm, d // tn),
        in_specs=o_specs + w_specs + m_specs,
        out_specs=pl.BlockSpec((tm, tn), lambda i, j: (i, j)),
        out_shape=jax.ShapeDtypeStruct((t, d), BF16),
        compiler_params=_params(("parallel", "parallel")),
        name="branch_merge",
    )(*outs, *weights, p, p, p, p)


def _out_kernel(y_ref, w_ref, x_ref, nw_ref, gate_ref, o_ref, *, tm, tiles_per_batch, n_lat):
    z = jnp.dot(y_ref[...], w_ref[...], preferred_element_type=F32)
    zn = z * lax.rsqrt(jnp.mean(z * z, axis=-1, keepdims=True) + EPS) * nw_ref[...]
    is_ctx = _is_ctx_rows(pl.program_id(0), tm, tiles_per_batch, n_lat)
    gate = jnp.where(is_ctx, gate_ref[1:2, :], gate_ref[0:1, :])
    o_ref[...] = x_ref[...] + gate * zn


def _out_projection(y, w_out, xa, norm_w, gate, n_lat, seq):
    t, d = xa.shape
    tm = _pick(seq, (384, 256, 128))
    tpb = seq // tm
    kern = functools.partial(_out_kernel, tm=tm, tiles_per_batch=tpb, n_lat=n_lat)
    return pl.pallas_call(
        kern,
        grid=(t // tm,),
        in_specs=[pl.BlockSpec((tm, d), lambda i: (i, 0)),
                  pl.BlockSpec((d, d), lambda i: (0, 0)),
                  pl.BlockSpec((tm, d), lambda i: (i, 0)),
                  pl.BlockSpec((1, d), lambda i: (0, 0)),
                  pl.BlockSpec((None, 2, d), lambda i: (i // tpb, 0, 0))],
        out_specs=pl.BlockSpec((tm, d), lambda i: (i, 0)),
        out_shape=jax.ShapeDtypeStruct((t, d), F32),
        compiler_params=_params(("parallel",)),
        name="out_proj",
    )(y, w_out, xa, norm_w.reshape(1, d), gate)


def _pairs_apart(w, n_heads, dim):
    lead = w.shape[:-1]
    return w.reshape(lead + (n_heads, dim // 2, 2)).swapaxes(-1, -2).reshape(lead + (n_heads * dim,))


def _rope_tile(w):
    lead = w.shape[:-1]
    pr = w.reshape(lead + (MLA_ROPE // 2, 2))
    zero = jnp.zeros(lead + (MLA_ROPE // 2,), w.dtype)
    return jnp.concatenate([pr[..., 0], zero, pr[..., 1], zero], axis=-1)


def _layout_w_in(w):
    o = _OFF
    k = w.shape[0]

    def seg(name, width):
        return w[:, o[name]:o[name] + width]

    cols = [seg("mix", 8192), seg("z", 2048), seg("xbc", SSM_CONV_DIM),
            _pairs_apart(seg("gq", 1024), GQA_HEADS, HEAD_DIM), seg("gg", 1024),
            seg("nq", 1024), seg("nk", 1024), seg("nv", 1024), seg("ng", 1024), seg("mg", 1024),
            seg("mqa", MLA_Q_LORA), jnp.zeros((k, MQA_PAD - MLA_Q_LORA), w.dtype),
            _pairs_apart(seg("gk", 512), GQA_KV_HEADS, HEAD_DIM), seg("gv", 512),
            seg("mkva", MLA_KV_LORA), _rope_tile(seg("mkr", MLA_ROPE))]
    main = jnp.concatenate(cols, axis=1)
    main = jnp.pad(main, ((0, 0), (0, P_WIDTH - main.shape[1]))).astype(BF16)
    dtr = seg("dtr", 2 * SSM_HEADS)
    zero = jnp.zeros((k, LANE - SSM_HEADS), w.dtype)
    side = jnp.concatenate([dtr[:, :SSM_HEADS], zero, dtr[:, SSM_HEADS:], zero], axis=1).astype(BF16)
    return main, side


def _layout_w_uq(w_uq):
    k = w_uq.shape[0]
    w = w_uq.reshape(k, MLA_HEADS, MLA_NOPE + MLA_ROPE)
    w = jnp.concatenate([w[..., :MLA_NOPE], _rope_tile(w[..., MLA_NOPE:])], axis=-1)
    w = w.reshape(k, MLA_HEADS * MLA_QK_PAD)
    return jnp.pad(w, ((0, MQA_PAD - k), (0, 0))).astype(BF16)


def _rope_tables(n_lat, n_ctx, dim):
    t = np.arange(n_lat)
    quarter = dim // 4
    freqs = ROPE_THETA ** (-jnp.arange(quarter, dtype=F32) / quarter)
    row = jnp.asarray(t // GRID_W, F32)
    col = jnp.asarray(t % GRID_W, F32)
    ang = jnp.concatenate([row[:, None] * freqs, col[:, None] * freqs], axis=-1)
    cos, sin = jnp.cos(ang), jnp.sin(ang)
    pad = 64 - dim // 2
    one, zero = jnp.ones((n_lat, pad), F32), jnp.zeros((n_lat, pad), F32)
    cos_t = jnp.concatenate([cos, one, cos, one], axis=-1)
    sin_t = jnp.concatenate([-sin, zero, sin, zero], axis=-1)
    cos_t = jnp.concatenate([cos_t, jnp.ones((n_ctx, LANE), F32)], axis=0)
    sin_t = jnp.concatenate([sin_t, jnp.zeros((n_ctx, LANE), F32)], axis=0)
    return cos_t, sin_t


def _layer(xa, cc, rope_g, rope_m, lp, batch, n_lat, seq):
    d = xa.shape[1]
    mod = _modulation(cc, lp["ada_w"], lp["ada_b"])

    def per_row(v):
        return jnp.stack([v[:batch], jnp.broadcast_to(v[batch:batch + 1], (batch, d))], axis=1)

    shift, scale, gate = (per_row(mod[:, k * d:(k + 1) * d]) for k in range(3))
    w_main, w_side = _layout_w_in(lp["w_in"])
    p, dtr = _in_projection(xa, lp["norm_pre"], scale, shift, w_main, w_side, n_lat, seq)

    u = _conv_silu(p, lp["conv_w"], lp["conv_b"], n_lat, seq)
    y = _ssd(u, dtr, lp["a_log"], lp["dt_bias"], batch, n_lat, seq)
    o_ssm = _ssm_out(y, u, p, lp["d_skip"], lp["ssm_norm"], seq)

    common = dict(batch=batch, n_lat=n_lat, seq=seq)
    qg = _head_norm_rope(p, P_GQ, GQA_HEADS, _pairs_apart(lp["gqa_q_norm"], 1, HEAD_DIM), *rope_g, seq)
    kg = _head_norm_rope(p, P_GK, GQA_KV_HEADS, _pairs_apart(lp["gqa_k_norm"], 1, HEAD_DIM), *rope_g, seq)
    o_gqa = _attention(
        qg, kg, p, p, 256, q_col0=0, k_col0=0, v_col0=P_GV, g_col0=P_GG, v_stride=1,
        n_kv_heads=GQA_KV_HEADS, r=GQA_HEADS // GQA_KV_HEADS, dq=HEAD_DIM, scale=HEAD_DIM ** -0.5,
        **common)

    na_lat = _natten(p, lp["na_bias"], **common)
    o_na = _flash(p, p, p, p, q_col0=P_NQ, k_col0=P_NK, v_col0=P_NV, g_col0=P_NG, v_stride=1,
                  n_kv_heads=NA_HEADS, r=1, dq=HEAD_DIM, scale=HEAD_DIM ** -0.5, ctx_only=True,
                  tq=256, into=na_lat, **common)

    q_norm = jnp.pad(lp["mla_q_norm"], (0, MQA_PAD - MLA_Q_LORA)).reshape(1, MQA_PAD)
    qm = _mla_q(p, q_norm, _layout_w_uq(lp["w_uq"]), *rope_m, seq)
    km, vm = _mla_kv(p, lp["mla_kv_norm"], lp["w_ukv"].astype(BF16), *rope_m, seq)
    o_mla = _attention(
        qm, km, vm, p, 512, q_col0=0, k_col0=0, v_col0=0, g_col0=P_MG, v_stride=1,
        n_kv_heads=MLA_HEADS, r=1, dq=MLA_QK_PAD, scale=(MLA_NOPE + MLA_ROPE) ** -0.5,
        **common)

    weights = [lp[n].astype(BF16) for n in ("w_o_ssm", "w_o_gqa", "w_o_na", "w_o_mla")]
    ymix = _merge([o_ssm, o_gqa, o_na, o_mla], weights, p, seq)
    return _out_projection(ymix, lp["w_out"].astype(BF16), xa, lp["norm_post"], gate, n_lat, seq)


def kernel(x, c, ctx, c_ctx, ada_w, ada_b, norm_pre, norm_post, w_in, conv_w, conv_b, a_log, dt_bias,
           d_skip, ssm_norm, w_o_ssm, gqa_q_norm, gqa_k_norm, w_o_gqa, na_rpb, w_o_na, mla_q_norm,
           w_uq, mla_kv_norm, w_ukv, w_o_mla, w_out):
    batch, n_lat, d = x.shape
    n_ctx = ctx.shape[1]
    seq = n_lat + n_ctx
    stacked = dict(ada_w=ada_w, ada_b=ada_b, norm_pre=norm_pre, norm_post=norm_post, w_in=w_in,
                   conv_w=conv_w, conv_b=conv_b, a_log=a_log, dt_bias=dt_bias, d_skip=d_skip,
                   ssm_norm=ssm_norm, w_o_ssm=w_o_ssm, gqa_q_norm=gqa_q_norm, gqa_k_norm=gqa_k_norm,
                   w_o_gqa=w_o_gqa, na_bias=_na_bias_table(na_rpb), w_o_na=w_o_na,
                   mla_q_norm=mla_q_norm, w_uq=w_uq,
                   mla_kv_norm=mla_kv_norm, w_ukv=w_ukv, w_o_mla=w_o_mla, w_out=w_out)
    xa = jnp.concatenate([x, ctx], axis=1).reshape(batch * seq, d)
    cc = jnp.concatenate([c, c_ctx[None, :], jnp.zeros((8 - batch - 1, d), c.dtype)], axis=0)
    rope_g = _rope_tables(n_lat, n_ctx, HEAD_DIM)
    rope_m = _rope_tables(n_lat, n_ctx, MLA_ROPE)
    for layer in range(ada_w.shape[0]):
        lp = {k: v[layer] for k, v in stacked.items()}
        xa = _layer(xa, cc, rope_g, rope_m, lp, batch, n_lat, seq)
    return xa.reshape(batch, seq, d)[:, :n_lat]
```

```python
import functools
import math

import jax
import jax.numpy as jnp
import numpy as np
from jax import lax
from jax.experimental import pallas as pl
from jax.experimental.pallas import tpu as pltpu

F32 = jnp.float32
BF16 = jnp.bfloat16

GRID_W = 64
EPS = 1e-6
ROPE_THETA = 10000.0

SSM_HEADS = 32
SSM_HEAD_DIM = 64
SSM_INNER = SSM_HEADS * SSM_HEAD_DIM
SSM_GROUPS = 4
SSM_STATE = 128
SSM_CONV = 5
SSM_CHUNK = 128
SSM_CONV_DIM = SSM_INNER + 2 * SSM_GROUPS * SSM_STATE

GQA_HEADS = 8
GQA_KV_HEADS = 4
HEAD_DIM = 128
NA_HEADS = 8
NA_WIN_H = 8
NA_WIN_W = 16
NA_QROWS = 4
NA_KROWS = NA_QROWS + NA_WIN_H
NA_HEADS_PER_STEP = 4

MLA_HEADS = 8
MLA_Q_LORA = 768
MLA_KV_LORA = 512
MLA_NOPE = 128
MLA_ROPE = 64
MLA_QK_PAD = 256

N_BRANCH = 4
LANE = 128
VMEM_LIMIT = 56 * 1024 * 1024
NEG_BIG = -1e30
FLASH_TK = 256

_SIZES = (SSM_INNER, SSM_CONV_DIM, 2 * SSM_HEADS, 1024, 512, 512, 1024, 1024, 1024, 1024, 1024,
          MLA_Q_LORA, MLA_KV_LORA, MLA_ROPE, 1024, N_BRANCH * 2048)
_OFF = dict(zip(("z", "xbc", "dtr", "gq", "gk", "gv", "gg", "nq", "nk", "nv", "ng",
                 "mqa", "mkva", "mkr", "mg", "mix"), np.cumsum((0,) + _SIZES[:-1]).tolist()))

P_MIX, P_Z, P_XBC = 0, 8192, 10240
P_GQ, P_GG, P_NQ, P_NK, P_NV, P_NG, P_MG, P_MQA = (13312, 14336, 15360, 16384, 17408, 18432,
                                                    19456, 20480)
P_GK, P_GV, P_MKVA, P_MKR = 21504, 22016, 22528, 23040
P_WIDTH = 23552
MQA_PAD = 1024


def _pick(n, candidates):
    for c in candidates:
        if n % c == 0:
            return c
    raise ValueError(f"no tile for {n} among {candidates}")


def _params(sem):
    return pltpu.CompilerParams(dimension_semantics=sem, vmem_limit_bytes=VMEM_LIMIT)


def _silu(v):
    return v * (1.0 / (1.0 + jnp.exp(-v)))


def _rope(v, cos, sin):
    return v * cos + pltpu.roll(v, 64, axis=1) * sin


def _mod_kernel(c_ref, w_ref, b_ref, o_ref):
    h = _silu(c_ref[...]).astype(BF16)
    o_ref[...] = jnp.dot(h, w_ref[...].astype(BF16), preferred_element_type=F32) + b_ref[...]


def _modulation(cc, ada_w, ada_b):
    rows, d = cc.shape
    n = ada_w.shape[1]
    tn = _pick(n, (512, 256, 128))
    return pl.pallas_call(
        _mod_kernel,
        grid=(n // tn,),
        in_specs=[pl.BlockSpec((rows, d), lambda j: (0, 0)),
                  pl.BlockSpec((d, tn), lambda j: (0, j)),
                  pl.BlockSpec((1, tn), lambda j: (0, j))],
        out_specs=pl.BlockSpec((rows, tn), lambda j: (0, j)),
        out_shape=jax.ShapeDtypeStruct((rows, n), F32),
        compiler_params=_params(("parallel",)),
        name="adaln_mod",
    )(cc, ada_w, ada_b.reshape(1, n))


def _is_ctx_rows(i, tm, tiles_per_batch, n_lat):
    row = (i % tiles_per_batch) * tm + lax.broadcasted_iota(jnp.int32, (tm, 1), 0)
    return row >= n_lat


def _inproj_kernel(x_ref, nw_ref, sc_ref, sh_ref, w_ref, ws_ref, o_ref, os_ref, h_ref, *,
                   tm, tiles_per_batch, n_lat):
    i = pl.program_id(0)

    @pl.when(pl.program_id(1) == 0)
    def _():
        x = x_ref[...]
        y = x * lax.rsqrt(jnp.mean(x * x, axis=-1, keepdims=True) + EPS) * nw_ref[...]
        is_ctx = _is_ctx_rows(i, tm, tiles_per_batch, n_lat)
        sc = jnp.where(is_ctx, sc_ref[1:2, :], sc_ref[0:1, :])
        sh = jnp.where(is_ctx, sh_ref[1:2, :], sh_ref[0:1, :])
        h = (y * (1.0 + sc) + sh).astype(BF16)
        h_ref[...] = h
        os_ref[...] = jnp.dot(h, ws_ref[...], preferred_element_type=F32)

    o_ref[...] = jnp.dot(h_ref[...], w_ref[...], preferred_element_type=F32).astype(o_ref.dtype)


def _in_projection(xa, norm_w, scale, shift, w_p, w_side, n_lat, seq):
    t, d = xa.shape
    tm = _pick(seq, (768, 512, 256))
    tn = _pick(P_WIDTH, (1024, 512))
    tpb = seq // tm
    kern = functools.partial(_inproj_kernel, tm=tm, tiles_per_batch=tpb, n_lat=n_lat)
    return pl.pallas_call(
        kern,
        grid=(t // tm, P_WIDTH // tn),
        in_specs=[pl.BlockSpec((tm, d), lambda i, j: (i, 0)),
                  pl.BlockSpec((1, d), lambda i, j: (0, 0)),
                  pl.BlockSpec((None, 2, d), lambda i, j: (i // tpb, 0, 0)),
                  pl.BlockSpec((None, 2, d), lambda i, j: (i // tpb, 0, 0)),
                  pl.BlockSpec((d, tn), lambda i, j: (0, j)),
                  pl.BlockSpec((d, 2 * LANE), lambda i, j: (0, 0))],
        out_specs=[pl.BlockSpec((tm, tn), lambda i, j: (i, j)),
                   pl.BlockSpec((tm, 2 * LANE), lambda i, j: (i, 0))],
        out_shape=[jax.ShapeDtypeStruct((t, P_WIDTH), BF16),
                   jax.ShapeDtypeStruct((t, 2 * LANE), F32)],
        scratch_shapes=[pltpu.VMEM((tm, d), BF16)],
        compiler_params=_params(("parallel", "arbitrary")),
        name="in_proj",
    )(xa, norm_w.reshape(1, d), scale, shift, w_p, w_side)


def _conv_kernel(u_ref, prev_ref, next_ref, w_ref, b_ref, s_ref, o_ref, *,
                 tm, tiles_per_batch, lat_tiles):
    ib = pl.program_id(0) % tiles_per_batch
    first = jnp.logical_or(ib == 0, ib == lat_tiles)
    last = jnp.logical_or(ib == lat_tiles - 1, ib == tiles_per_batch - 1)
    half = SSM_CONV // 2
    u = u_ref[...]
    acc = b_ref[...] + w_ref[half:half + 1, :] * u.astype(F32)
    for idx, k in enumerate(k for k in range(SSM_CONV) if k != half):
        acc = acc + w_ref[k:k + 1, :] * jnp.dot(s_ref[idx], u, preferred_element_type=F32)
    prev = jnp.where(first, 0.0, prev_ref[...].astype(F32))
    nxt = jnp.where(last, 0.0, next_ref[...].astype(F32))
    row = lax.broadcasted_iota(jnp.int32, (8, 1), 0)
    top = jnp.zeros_like(prev)
    bot = jnp.zeros_like(nxt)
    for k in range(half):
        reach = half - k
        top = top + w_ref[k:k + 1, :] * jnp.where(row < reach, pltpu.roll(prev, reach, axis=0), 0.0)
        kk = SSM_CONV - 1 - k
        bot = bot + w_ref[kk:kk + 1, :] * jnp.where(row >= 8 - reach,
                                                     pltpu.roll(nxt, 8 - reach, axis=0), 0.0)
    y = jnp.concatenate([acc[0:8] + top, acc[8:tm - 8], acc[tm - 8:tm] + bot], axis=0)
    o_ref[...] = _silu(y).astype(o_ref.dtype)


def _conv_silu(p, conv_w, conv_b, n_lat, seq):
    t = p.shape[0]
    tm = 256
    tc = 1024
    tpb = seq // tm
    n_row8 = t // 8
    col0 = P_XBC // tc
    kern = functools.partial(_conv_kernel, tm=tm, tiles_per_batch=tpb, lat_tiles=n_lat // tm)
    half = SSM_CONV // 2
    shifts = np.stack([np.eye(tm, k=k - half, dtype=np.float32) for k in range(SSM_CONV) if k != half])
    return pl.pallas_call(
        kern,
        grid=(t // tm, SSM_CONV_DIM // tc),
        in_specs=[pl.BlockSpec((tm, tc), lambda i, j: (i, col0 + j)),
                  pl.BlockSpec((8, tc), lambda i, j: (jnp.maximum(i * (tm // 8) - 1, 0), col0 + j)),
                  pl.BlockSpec((8, tc), lambda i, j: (jnp.minimum((i + 1) * (tm // 8), n_row8 - 1),
                                                      col0 + j)),
                  pl.BlockSpec((8, tc), lambda i, j: (0, j)),
                  pl.BlockSpec((1, tc), lambda i, j: (0, j)),
                  pl.BlockSpec((SSM_CONV - 1, tm, tm), lambda i, j: (0, 0, 0))],
        out_specs=pl.BlockSpec((tm, tc), lambda i, j: (i, j)),
        out_shape=jax.ShapeDtypeStruct((t, SSM_CONV_DIM), BF16),
        compiler_params=_params(("parallel", "parallel")),
        name="ssm_conv",
    )(p, p, p, jnp.pad(conv_w, ((0, 8 - SSM_CONV), (0, 0))), conv_b.reshape(1, -1),
      jnp.asarray(shifts, BF16))


def _split_dot(a_bf16, v):
    v1 = v.astype(BF16)
    r1 = v - v1.astype(F32)
    v2 = r1.astype(BF16)
    v3 = (r1 - v2.astype(F32)).astype(BF16)
    return jnp.dot(jnp.concatenate([a_bf16, a_bf16, a_bf16], axis=1),
                   jnp.concatenate([v1, v2, v3], axis=0), preferred_element_type=F32)


def _ssd_kernel(u_ref, dtr_ref, alog_ref, bias_ref, eh_ref, y_ref, st_ref):
    cl = SSM_CHUNK
    fwd = pl.program_id(1) == 0

    @pl.when(pl.program_id(2) == 0)
    def _():
        st_ref[...] = jnp.zeros_like(st_ref)

    raw = dtr_ref[...] + bias_ref[...]
    dt = jnp.maximum(raw, 0.0) + jnp.log(1.0 + jnp.exp(-jnp.abs(raw)))
    da = dt * (-jnp.exp(alog_ref[...]))
    r = lax.broadcasted_iota(jnp.int32, (cl, cl), 0)
    c = lax.broadcasted_iota(jnp.int32, (cl, cl), 1)
    tri = jnp.where(fwd, r - c, c - r) >= 0
    acs = _split_dot(jnp.where(tri, 1.0, 0.0).astype(BF16), da)
    total = jnp.where(fwd, acs[cl - 1:cl, :], acs[0:1, :])
    e_acs = jnp.exp(acs)
    w_end = dt * jnp.exp(total - acs)
    acs_t = acs.T
    dt_t = dt.T

    def hi_lo(v):
        hi = v.astype(BF16)
        return jnp.concatenate([hi, (v - hi.astype(F32)).astype(BF16)], axis=1)

    ex = jnp.dot(jnp.concatenate([hi_lo(w_end), hi_lo(e_acs)], axis=0), eh_ref[...],
                 preferred_element_type=F32)
    w_exp = ex[0:cl]
    e_exp = ex[cl:2 * cl]
    dec = jnp.where(fwd, e_exp[cl - 1:cl, :], e_exp[0:1, :])
    xw = (u_ref[:, 0:SSM_INNER].astype(F32) * w_exp).astype(BF16)
    lane = lax.broadcasted_iota(jnp.int32, (cl, LANE), 1)
    gw = SSM_INNER // SSM_GROUPS
    hpg = SSM_HEADS // SSM_GROUPS
    for g in range(SSM_GROUPS):
        b_g = u_ref[:, SSM_INNER + g * SSM_STATE:SSM_INNER + (g + 1) * SSM_STATE]
        c_off = SSM_INNER + SSM_GROUPS * SSM_STATE
        c_g = u_ref[:, c_off + g * SSM_STATE:c_off + (g + 1) * SSM_STATE]
        cb = lax.dot_general(c_g, b_g, (((1,), (1,)), ((), ())), preferred_element_type=F32)
        st_g = st_ref[:, g * gw:(g + 1) * gw]
        y_off = jnp.dot(c_g, st_g.astype(BF16), preferred_element_type=F32)
        ys = []
        for k in range(hpg // 2):
            ms = []
            for h in (g * hpg + 2 * k, g * hpg + 2 * k + 1):
                seg = acs[:, h:h + 1] - acs_t[h:h + 1, :]
                dec_h = jnp.exp(jnp.where(tri, seg, NEG_BIG))
                ms.append((dec_h * cb * dt_t[h:h + 1, :]).astype(BF16))
            x_p = u_ref[:, g * gw + k * LANE:g * gw + (k + 1) * LANE]
            zero = jnp.zeros_like(x_p)
            rhs = jnp.concatenate([jnp.where(lane < SSM_HEAD_DIM, x_p, zero),
                                   jnp.where(lane >= SSM_HEAD_DIM, x_p, zero)], axis=0)
            ys.append(jnp.dot(jnp.concatenate(ms, axis=1), rhs, preferred_element_type=F32))
        y_g = jnp.concatenate(ys, axis=1) + y_off * e_exp[:, g * gw:(g + 1) * gw]
        y_ref[:, g * gw:(g + 1) * gw] = y_g.astype(y_ref.dtype)
        upd = lax.dot_general(b_g, xw[:, g * gw:(g + 1) * gw], (((0,), (0,)), ((), ())),
                              preferred_element_type=F32)
        st_ref[:, g * gw:(g + 1) * gw] = st_g * dec[:, g * gw:(g + 1) * gw] + upd


def _ssd(u, dtr, a_log, dt_bias, batch, n_lat, seq):
    cl = SSM_CHUNK
    nch, nlat, nctx = seq // cl, n_lat // cl, (seq - n_lat) // cl

    def chunk(d, c):
        f = jnp.where(c < nctx, nlat + c, c - nctx)
        b = jnp.where(c < nctx, nlat + nctx - 1 - c, nlat - 1 - (c - nctx))
        return jnp.where(d == 0, f, b)

    def pad_heads(v):
        return jnp.pad(v, ((0, 0), (0, LANE - SSM_HEADS))).reshape(2, 1, LANE)

    eh = np.zeros((2 * LANE, SSM_INNER), np.float32)
    for h in range(SSM_HEADS):
        eh[h, h * SSM_HEAD_DIM:(h + 1) * SSM_HEAD_DIM] = 1.0
        eh[LANE + h, h * SSM_HEAD_DIM:(h + 1) * SSM_HEAD_DIM] = 1.0
    return pl.pallas_call(
        _ssd_kernel,
        grid=(batch, 2, nch),
        in_specs=[pl.BlockSpec((cl, SSM_CONV_DIM), lambda b, d, c: (b * nch + chunk(d, c), 0)),
                  pl.BlockSpec((cl, LANE), lambda b, d, c: (b * nch + chunk(d, c), d)),
                  pl.BlockSpec((None, 1, LANE), lambda b, d, c: (d, 0, 0)),
                  pl.BlockSpec((None, 1, LANE), lambda b, d, c: (d, 0, 0)),
                  pl.BlockSpec((2 * LANE, SSM_INNER), lambda b, d, c: (0, 0))],
        out_specs=pl.BlockSpec((None, cl, SSM_INNER), lambda b, d, c: (d, b * nch + chunk(d, c), 0)),
        out_shape=jax.ShapeDtypeStruct((2, u.shape[0], SSM_INNER), BF16),
        scratch_shapes=[pltpu.VMEM((SSM_STATE, SSM_INNER), F32)],
        compiler_params=_params(("parallel", "parallel", "arbitrary")),
        name="ssd_scan",
    )(u, dtr, pad_heads(a_log), pad_heads(dt_bias), jnp.asarray(eh, BF16))


def _ssm_out_kernel(yf_ref, yb_ref, x_ref, z_ref, skip_ref, nw_ref, o_ref):
    z = z_ref[...].astype(F32)
    g = (yf_ref[...].astype(F32) + yb_ref[...].astype(F32)
         + skip_ref[...] * x_ref[...].astype(F32)) * _silu(z)
    gw = SSM_INNER // SSM_GROUPS
    for k in range(SSM_GROUPS):
        gk = g[:, k * gw:(k + 1) * gw]
        gk = gk * lax.rsqrt(jnp.mean(gk * gk, axis=-1, keepdims=True) + EPS)
        o_ref[:, k * gw:(k + 1) * gw] = (gk * nw_ref[:, k * gw:(k + 1) * gw]).astype(o_ref.dtype)


def _ssm_out(y, u, p, d_skip, ssm_norm, seq):
    t = u.shape[0]
    tm = _pick(seq, (384, 256, 128))
    w = SSM_INNER
    return pl.pallas_call(
        _ssm_out_kernel,
        grid=(t // tm,),
        in_specs=[pl.BlockSpec((None, tm, w), lambda i: (0, i, 0)),
                  pl.BlockSpec((None, tm, w), lambda i: (1, i, 0)),
                  pl.BlockSpec((tm, w), lambda i: (i, 0)),
                  pl.BlockSpec((tm, w), lambda i: (i, P_Z // w)),
                  pl.BlockSpec((1, w), lambda i: (0, 0)),
                  pl.BlockSpec((1, w), lambda i: (0, 0))],
        out_specs=pl.BlockSpec((tm, w), lambda i: (i, 0)),
        out_shape=jax.ShapeDtypeStruct((t, w), BF16),
        compiler_params=_params(("parallel",)),
        name="ssm_gated_norm",
    )(y, y, u, p, jnp.repeat(d_skip, SSM_HEAD_DIM).reshape(1, w), ssm_norm.reshape(1, w))


def _head_norm_rope_kernel(x_ref, nw_ref, cos_ref, sin_ref, o_ref, *, n_heads):
    for h in range(n_heads):
        sl = slice(h * LANE, (h + 1) * LANE)
        x = x_ref[:, sl].astype(F32)
        y = x * lax.rsqrt(jnp.mean(x * x, axis=-1, keepdims=True) + EPS) * nw_ref[...]
        o_ref[:, sl] = _rope(y, cos_ref[...], sin_ref[...]).astype(o_ref.dtype)


def _head_norm_rope(p, col0, n_heads, norm_w, cos, sin, seq):
    t = p.shape[0]
    tm = _pick(seq, (768, 512, 256))
    tpb = seq // tm
    w = n_heads * LANE
    return pl.pallas_call(
        functools.partial(_head_norm_rope_kernel, n_heads=n_heads),
        grid=(t // tm,),
        in_specs=[pl.BlockSpec((tm, w), lambda i: (i, col0 // w)),
                  pl.BlockSpec((1, LANE), lambda i: (0, 0)),
                  pl.BlockSpec((tm, LANE), lambda i: (i % tpb, 0)),
                  pl.BlockSpec((tm, LANE), lambda i: (i % tpb, 0))],
        out_specs=pl.BlockSpec((tm, w), lambda i: (i, 0)),
        out_shape=jax.ShapeDtypeStruct((t, w), BF16),
        compiler_params=_params(("parallel",)),
        name="head_norm_rope",
    )(p, norm_w.reshape(1, LANE), cos, sin)


def _mla_q_kernel(x_ref, nw_ref, w_ref, cos_ref, sin_ref, o_ref):
    x = x_ref[...].astype(F32)
    ms = jnp.sum(x * x, axis=-1, keepdims=True) * (1.0 / MLA_Q_LORA)
    h = (x * lax.rsqrt(ms + EPS) * nw_ref[...]).astype(BF16)
    for hd in range(MLA_HEADS):
        c0 = hd * MLA_QK_PAD
        res = jnp.dot(h, w_ref[:, c0:c0 + MLA_QK_PAD], preferred_element_type=F32)
        o_ref[:, c0:c0 + LANE] = res[:, 0:LANE].astype(o_ref.dtype)
        o_ref[:, c0 + LANE:c0 + MLA_QK_PAD] = _rope(res[:, LANE:], cos_ref[...],
                                                    sin_ref[...]).astype(o_ref.dtype)


def _mla_q(p, q_norm, w_uq, cos, sin, seq):
    t = p.shape[0]
    tm = _pick(seq, (768, 512, 256))
    tpb = seq // tm
    kdim = MQA_PAD
    wo = MLA_HEADS * MLA_QK_PAD
    return pl.pallas_call(
        _mla_q_kernel,
        grid=(t // tm,),
        in_specs=[pl.BlockSpec((tm, kdim), lambda i: (i, P_MQA // kdim)),
                  pl.BlockSpec((1, kdim), lambda i: (0, 0)),
                  pl.BlockSpec((kdim, wo), lambda i: (0, 0)),
                  pl.BlockSpec((tm, LANE), lambda i: (i % tpb, 0)),
                  pl.BlockSpec((tm, LANE), lambda i: (i % tpb, 0))],
        out_specs=pl.BlockSpec((tm, wo), lambda i: (i, 0)),
        out_shape=jax.ShapeDtypeStruct((t, wo), BF16),
        compiler_params=_params(("parallel",)),
        name="mla_q_up",
    )(p, q_norm, w_uq, cos, sin)


def _mla_kv_kernel(x_ref, kr_ref, nw_ref, w_ref, cos_ref, sin_ref, k_ref, v_ref):
    x = x_ref[...].astype(F32)
    h = (x * lax.rsqrt(jnp.mean(x * x, axis=-1, keepdims=True) + EPS) * nw_ref[...]).astype(BF16)
    k_rope = _rope(kr_ref[...].astype(F32), cos_ref[...], sin_ref[...]).astype(k_ref.dtype)
    for hd in range(MLA_HEADS):
        c0 = hd * MLA_QK_PAD
        res = jnp.dot(h, w_ref[:, c0:c0 + 2 * LANE], preferred_element_type=F32)
        k_ref[:, c0:c0 + LANE] = res[:, 0:LANE].astype(k_ref.dtype)
        k_ref[:, c0 + LANE:c0 + MLA_QK_PAD] = k_rope
        v_ref[:, hd * LANE:(hd + 1) * LANE] = res[:, LANE:].astype(v_ref.dtype)


def _mla_kv(p, kv_norm, w_ukv, cos, sin, seq):
    t = p.shape[0]
    tm = _pick(seq, (768, 512, 256))
    tpb = seq // tm
    kdim = MLA_KV_LORA
    wk = MLA_HEADS * MLA_QK_PAD
    return pl.pallas_call(
        _mla_kv_kernel,
        grid=(t // tm,),
        in_specs=[pl.BlockSpec((tm, kdim), lambda i: (i, P_MKVA // kdim)),
                  pl.BlockSpec((tm, LANE), lambda i: (i, P_MKR // LANE)),
                  pl.BlockSpec((1, kdim), lambda i: (0, 0)),
                  pl.BlockSpec((kdim, wk), lambda i: (0, 0)),
                  pl.BlockSpec((tm, LANE), lambda i: (i % tpb, 0)),
                  pl.BlockSpec((tm, LANE), lambda i: (i % tpb, 0))],
        out_specs=[pl.BlockSpec((tm, wk), lambda i: (i, 0)),
                   pl.BlockSpec((tm, MLA_HEADS * LANE), lambda i: (i, 0))],
        out_shape=[jax.ShapeDtypeStruct((t, wk), BF16),
                   jax.ShapeDtypeStruct((t, MLA_HEADS * LANE), BF16)],
        compiler_params=_params(("parallel",)),
        name="mla_kv_up",
    )(p, p, kv_norm.reshape(1, kdim), w_ukv, cos, sin)


def _flash_kernel(q_ref, k_ref, v_ref, g_ref, *rest, r, tq, dq, tk, n_keys, scale):
    o_ref = rest[-1]
    q = jnp.concatenate([q_ref[:, j * dq:(j + 1) * dq] for j in range(r)], axis=0)
    q = (q.astype(F32) * (scale * math.log2(math.e))).astype(BF16)
    m_rows = r * tq
    m = jnp.full((m_rows, 1), NEG_BIG, F32)
    l = jnp.zeros((m_rows, LANE), F32)
    acc = jnp.zeros((m_rows, LANE), F32)
    for c in range(n_keys // tk):
        k_c = k_ref[c * tk:(c + 1) * tk, :]
        v_c = v_ref[c * tk:(c + 1) * tk, :]
        s = lax.dot_general(q, k_c, (((1,), (1,)), ((), ())), preferred_element_type=F32)
        m_new = jnp.maximum(m, jnp.max(s, axis=-1, keepdims=True))
        alpha = jnp.exp2(m - m_new)
        pr = jnp.exp2(s - m_new)
        part = pr[:, 0:LANE]
        for j in range(1, tk // LANE):
            part = part + pr[:, j * LANE:(j + 1) * LANE]
        l = alpha * l + part
        acc = alpha * acc + jnp.dot(pr.astype(BF16), v_c, preferred_element_type=F32)
        m = m_new
    o = acc / jnp.sum(l, axis=-1, keepdims=True)
    for j in range(r):
        gate = _silu(g_ref[:, j * LANE:(j + 1) * LANE].astype(F32))
        o_ref[:, j * LANE:(j + 1) * LANE] = (o[j * tq:(j + 1) * tq] * gate).astype(o_ref.dtype)


def _flash(q, k, v, gates, *, q_col0, k_col0, v_col0, g_col0, v_stride, n_kv_heads, r, dq, scale,
           batch, n_lat, seq, ctx_only, tq, into=None):
    n_ctx = seq - n_lat
    tk = FLASH_TK
    if ctx_only:
        tq = n_ctx
        q_blk0, n_q, kv_rows, kv_blk0 = n_lat // tq, 1, n_ctx, n_lat // n_ctx
    else:
        q_blk0, n_q, kv_rows, kv_blk0 = 0, n_lat // tq, seq, 0
    assert kv_rows % tk == 0
    kern = functools.partial(_flash_kernel, r=r, tq=tq, dq=dq, tk=tk, n_keys=kv_rows, scale=scale)
    w_out = n_kv_heads * r * LANE
    args = [a.reshape(batch, seq, a.shape[-1]) for a in (q, k, v, gates)]
    in_specs = [pl.BlockSpec((None, tq, r * dq), lambda b, g, i: (b, q_blk0 + i, q_col0 // (r * dq) + g)),
                pl.BlockSpec((None, kv_rows, dq), lambda b, g, i: (b, kv_blk0, k_col0 // dq + g)),
                pl.BlockSpec((None, kv_rows, LANE),
                             lambda b, g, i: (b, kv_blk0, v_col0 // LANE + v_stride * g)),
                pl.BlockSpec((None, tq, r * LANE),
                             lambda b, g, i: (b, q_blk0 + i, g_col0 // (r * LANE) + g))]
    aliases = {}
    if into is not None:
        args.append(into.reshape(batch, seq, w_out))
        in_specs.append(pl.BlockSpec(memory_space=pl.ANY))
        aliases = {len(args) - 1: 0}
    out = pl.pallas_call(
        kern,
        grid=(batch, n_kv_heads, n_q),
        in_specs=in_specs,
        out_specs=pl.BlockSpec((None, tq, r * LANE), lambda b, g, i: (b, q_blk0 + i, g)),
        out_shape=jax.ShapeDtypeStruct((batch, seq, w_out), BF16),
        input_output_aliases=aliases,
        compiler_params=_params(("parallel", "parallel", "parallel")),
        name="flash_ctx" if ctx_only else "flash_latent",
    )(*args)
    return out.reshape(batch * seq, w_out)


def _attention(q, k, v, gates, tq, **kw):
    lat = _flash(q, k, v, gates, ctx_only=False, tq=tq, **kw)
    return _flash(q, k, v, gates, ctx_only=True, tq=tq, into=lat, **kw)


def _natten_kernel(q_ref, k0_ref, k1_ref, k2_ref, kc_ref, v0_ref, v1_ref, v2_ref, vc_ref, g_ref,
                   bias_ref, o_ref, *, scale, n_heads):
    n_loc = bias_ref.shape[-1]
    for h in range(n_heads):
        sl = slice(h * LANE, (h + 1) * LANE)
        q = (q_ref[:, sl].astype(F32) * (scale * math.log2(math.e))).astype(BF16)
        k = jnp.concatenate([k0_ref[:, sl], k1_ref[:, sl], k2_ref[:, sl], kc_ref[:, sl]], axis=0)
        v = jnp.concatenate([v0_ref[:, sl], v1_ref[:, sl], v2_ref[:, sl], vc_ref[:, sl]], axis=0)
        s = lax.dot_general(q, k, (((1,), (1,)), ((), ())), preferred_element_type=F32)
        s = jnp.concatenate([s[:, :n_loc] + bias_ref[h], s[:, n_loc:]], axis=1)
        pr = jnp.exp2(s - jnp.max(s, axis=-1, keepdims=True))
        l = jnp.sum(pr, axis=-1, keepdims=True)
        o = jnp.dot(pr.astype(BF16), v, preferred_element_type=F32) / l
        o_ref[:, sl] = (o * _silu(g_ref[:, sl].astype(F32))).astype(o_ref.dtype)


def _na_bias_table(rpb):
    qx = np.arange(GRID_W)
    c0 = np.clip(qx - NA_WIN_W // 2, 0, GRID_W - NA_WIN_W)
    col_ok = (qx[None, :] >= c0[:, None]) & (qx[None, :] < c0[:, None] + NA_WIN_W)
    dx = qx[None, :] - qx[:, None] + NA_WIN_W - 1
    pick = np.zeros((2 * NA_WIN_W - 1, GRID_W * GRID_W), np.float32)
    qi, ki = np.nonzero(col_ok)
    pick[dx[qi, ki], qi * GRID_W + ki] = 1.0
    by_dx = jnp.einsum("...d,dn->...n", rpb, pick, precision=lax.Precision.HIGHEST)
    qy, ky = np.arange(NA_QROWS), np.arange(NA_KROWS)
    dys, oks = [], []
    for q_off, first_key in ((0, None), (NA_WIN_H // 2, "q"), (NA_WIN_H, NA_KROWS - NA_WIN_H)):
        r0 = qy if first_key == "q" else np.full_like(qy, 0 if first_key is None else first_key)
        oks.append((ky[None, :] >= r0[:, None]) & (ky[None, :] < r0[:, None] + NA_WIN_H))
        dys.append(np.clip(ky[None, :] - (qy[:, None] + q_off) + NA_WIN_H - 1, 0, 2 * NA_WIN_H - 2))
    rows = jnp.take(by_dx, np.stack(dys).reshape(-1), axis=-2)
    lead = rows.shape[:-2]
    nl = len(lead)
    rows = rows.reshape(lead + (3, NA_QROWS, NA_KROWS, GRID_W, GRID_W))
    rows = rows.transpose(tuple(range(nl)) + (nl, nl + 1, nl + 3, nl + 2, nl + 4))
    ok = np.stack(oks)[:, :, None, :, None] & col_ok[None, None, :, None, :]
    tab = jnp.where(jnp.asarray(ok), rows * math.log2(math.e), NEG_BIG)
    return tab.reshape(lead + (3, NA_QROWS * GRID_W, NA_KROWS * GRID_W))


def _natten(p, bias, batch, n_lat, seq):
    n_ctx = seq - n_lat
    tq = NA_QROWS * GRID_W
    n_blk = n_lat // tq
    assert n_blk >= 3 and n_lat % tq == 0 and n_ctx % tq == 0
    rb = seq // tq
    cb = seq // n_ctx
    nk = NA_KROWS * GRID_W
    hps = NA_HEADS_PER_STEP
    hw = hps * LANE

    def kblk(i):
        return jnp.clip(i - 1, 0, n_blk - 3)

    def kspec(col0, j):
        return pl.BlockSpec((tq, hw), lambda b, h, i: (b * rb + kblk(i) + j, col0 // hw + h))

    def cspec(col0):
        return pl.BlockSpec((n_ctx, hw), lambda b, h, i: (b * cb + n_lat // n_ctx, col0 // hw + h))

    def qspec(col0):
        return pl.BlockSpec((tq, hw), lambda b, h, i: (b * rb + i, col0 // hw + h))

    def btype(i):
        return jnp.where(i == 0, 0, jnp.where(i == n_blk - 1, 2, 1))

    return pl.pallas_call(
        functools.partial(_natten_kernel, scale=HEAD_DIM ** -0.5, n_heads=hps),
        grid=(batch, NA_HEADS // hps, n_blk),
        in_specs=[qspec(P_NQ), kspec(P_NK, 0), kspec(P_NK, 1), kspec(P_NK, 2), cspec(P_NK),
                  kspec(P_NV, 0), kspec(P_NV, 1), kspec(P_NV, 2), cspec(P_NV), qspec(P_NG),
                  pl.BlockSpec((hps, None, tq, nk), lambda b, h, i: (h, btype(i), 0, 0))],
        out_specs=pl.BlockSpec((tq, hw), lambda b, h, i: (b * rb + i, h)),
        out_shape=jax.ShapeDtypeStruct((p.shape[0], NA_HEADS * LANE), BF16),
        compiler_params=_params(("parallel", "parallel", "parallel")),
        name="natten_latent",
    )(p, p, p, p, p, p, p, p, p, p, bias)


def _merge_kernel(o0_ref, o1_ref, o2_ref, o3_ref, w0_ref, w1_ref, w2_ref, w3_ref,
                  m0_ref, m1_ref, m2_ref, m3_ref, y_ref):
    acc = None
    for o_ref, w_ref, m_ref in ((o0_ref, w0_ref, m0_ref), (o1_ref, w1_ref, m1_ref),
                                (o2_ref, w2_ref, m2_ref), (o3_ref, w3_ref, m3_ref)):
        gate = 1.0 / (1.0 + jnp.exp(-m_ref[...].astype(F32)))
        term = gate * jnp.dot(o_ref[...], w_ref[...], preferred_element_type=F32)
        acc = term if acc is None else acc + term
    y_ref[...] = acc.astype(y_ref.dtype)


def _merge(outs, weights, p, seq):
    t = p.shape[0]
    d = weights[0].shape[1]
    tm = _pick(seq, (768, 512, 256))
    tn = 512
    o_specs = [pl.BlockSpec((tm, o.shape[1]), lambda i, j: (i, 0)) for o in outs]
    w_specs = [pl.BlockSpec((w.shape[0], tn), lambda i, j: (0, j)) for w in weights]
    m_specs = [pl.BlockSpec((tm, tn), lambda i, j, b=b: (i, (P_MIX + b * d) // tn + j))
               for b in range(N_BRANCH)]
    return pl.pallas_call(
        _merge_kernel,
        grid=(t // tm, d // tn),
        in_specs=o_specs + w_specs + m_specs,
        out_specs=pl.BlockSpec((tm, tn), lambda i, j: (i, j)),
        out_shape=jax.ShapeDtypeStruct((t, d), BF16),
        compiler_params=_params(("parallel", "parallel")),
        name="branch_merge",
    )(*outs, *weights, p, p, p, p)


def _out_kernel(y_ref, w_ref, x_ref, nw_ref, gate_ref, o_ref, *, tm, tiles_per_batch, n_lat):
    z = jnp.dot(y_ref[...], w_ref[...], preferred_element_type=F32)
    zn = z * lax.rsqrt(jnp.mean(z * z, axis=-1, keepdims=True) + EPS) * nw_ref[...]
    is_ctx = _is_ctx_rows(pl.program_id(0), tm, tiles_per_batch, n_lat)
    gate = jnp.where(is_ctx, gate_ref[1:2, :], gate_ref[0:1, :])
    o_ref[...] = x_ref[...] + gate * zn


def _out_projection(y, w_out, xa, norm_w, gate, n_lat, seq):
    t, d = xa.shape
    tm = _pick(seq, (768, 384, 256, 128))
    tpb = seq // tm
    kern = functools.partial(_out_kernel, tm=tm, tiles_per_batch=tpb, n_lat=n_lat)
    return pl.pallas_call(
        kern,
        grid=(t // tm,),
        in_specs=[pl.BlockSpec((tm, d), lambda i: (i, 0)),
                  pl.BlockSpec((d, d), lambda i: (0, 0)),
                  pl.BlockSpec((tm, d), lambda i: (i, 0)),
                  pl.BlockSpec((1, d), lambda i: (0, 0)),
                  pl.BlockSpec((None, 2, d), lambda i: (i // tpb, 0, 0))],
        out_specs=pl.BlockSpec((tm, d), lambda i: (i, 0)),
        out_shape=jax.ShapeDtypeStruct((t, d), F32),
        compiler_params=_params(("parallel",)),
        name="out_proj",
    )(y, w_out, xa, norm_w.reshape(1, d), gate)


def _pairs_apart(w, n_heads, dim):
    lead = w.shape[:-1]
    return w.reshape(lead + (n_heads, dim // 2, 2)).swapaxes(-1, -2).reshape(lead + (n_heads * dim,))


def _rope_tile(w):
    lead = w.shape[:-1]
    pr = w.reshape(lead + (MLA_ROPE // 2, 2))
    zero = jnp.zeros(lead + (MLA_ROPE // 2,), w.dtype)
    return jnp.concatenate([pr[..., 0], zero, pr[..., 1], zero], axis=-1)


def _layout_w_in(w):
    o = _OFF
    k = w.shape[0]

    def seg(name, width):
        return w[:, o[name]:o[name] + width]

    cols = [seg("mix", 8192), seg("z", 2048), seg("xbc", SSM_CONV_DIM),
            _pairs_apart(seg("gq", 1024), GQA_HEADS, HEAD_DIM), seg("gg", 1024),
            seg("nq", 1024), seg("nk", 1024), seg("nv", 1024), seg("ng", 1024), seg("mg", 1024),
            seg("mqa", MLA_Q_LORA), jnp.zeros((k, MQA_PAD - MLA_Q_LORA), w.dtype),
            _pairs_apart(seg("gk", 512), GQA_KV_HEADS, HEAD_DIM), seg("gv", 512),
            seg("mkva", MLA_KV_LORA), _rope_tile(seg("mkr", MLA_ROPE))]
    main = jnp.concatenate(cols, axis=1)
    main = jnp.pad(main, ((0, 0), (0, P_WIDTH - main.shape[1]))).astype(BF16)
    dtr = seg("dtr", 2 * SSM_HEADS)
    zero = jnp.zeros((k, LANE - SSM_HEADS), w.dtype)
    side = jnp.concatenate([dtr[:, :SSM_HEADS], zero, dtr[:, SSM_HEADS:], zero], axis=1).astype(BF16)
    return main, side


def _layout_w_uq(w_uq):
    k = w_uq.shape[0]
    w = w_uq.reshape(k, MLA_HEADS, MLA_NOPE + MLA_ROPE)
    w = jnp.concatenate([w[..., :MLA_NOPE], _rope_tile(w[..., MLA_NOPE:])], axis=-1)
    w = w.reshape(k, MLA_HEADS * MLA_QK_PAD)
    return jnp.pad(w, ((0, MQA_PAD - k), (0, 0))).astype(BF16)


def _rope_tables(n_lat, n_ctx, dim):
    t = np.arange(n_lat)
    quarter = dim // 4
    freqs = ROPE_THETA ** (-jnp.arange(quarter, dtype=F32) / quarter)
    row = jnp.asarray(t // GRID_W, F32)
    col = jnp.asarray(t % GRID_W, F32)
    ang = jnp.concatenate([row[:, None] * freqs, col[:, None] * freqs], axis=-1)
    cos, sin = jnp.cos(ang), jnp.sin(ang)
    pad = 64 - dim // 2
    one, zero = jnp.ones((n_lat, pad), F32), jnp.zeros((n_lat, pad), F32)
    cos_t = jnp.concatenate([cos, one, cos, one], axis=-1)
    sin_t = jnp.concatenate([-sin, zero, sin, zero], axis=-1)
    cos_t = jnp.concatenate([cos_t, jnp.ones((n_ctx, LANE), F32)], axis=0)
    sin_t = jnp.concatenate([sin_t, jnp.zeros((n_ctx, LANE), F32)], axis=0)
    return cos_t, sin_t


def _layer(xa, cc, rope_g, rope_m, lp, batch, n_lat, seq):
    d = xa.shape[1]
    mod = _modulation(cc, lp["ada_w"], lp["ada_b"])

    def per_row(v):
        return jnp.stack([v[:batch], jnp.broadcast_to(v[batch:batch + 1], (batch, d))], axis=1)

    shift, scale, gate = (per_row(mod[:, k * d:(k + 1) * d]) for k in range(3))
    w_main, w_side = _layout_w_in(lp["w_in"])
    p, dtr = _in_projection(xa, lp["norm_pre"], scale, shift, w_main, w_side, n_lat, seq)

    u = _conv_silu(p, lp["conv_w"], lp["conv_b"], n_lat, seq)
    y = _ssd(u, dtr, lp["a_log"], lp["dt_bias"], batch, n_lat, seq)
    o_ssm = _ssm_out(y, u, p, lp["d_skip"], lp["ssm_norm"], seq)

    common = dict(batch=batch, n_lat=n_lat, seq=seq)
    qg = _head_norm_rope(p, P_GQ, GQA_HEADS, _pairs_apart(lp["gqa_q_norm"], 1, HEAD_DIM), *rope_g, seq)
    kg = _head_norm_rope(p, P_GK, GQA_KV_HEADS, _pairs_apart(lp["gqa_k_norm"], 1, HEAD_DIM), *rope_g, seq)
    o_gqa = _attention(
        qg, kg, p, p, 256, q_col0=0, k_col0=0, v_col0=P_GV, g_col0=P_GG, v_stride=1,
        n_kv_heads=GQA_KV_HEADS, r=GQA_HEADS // GQA_KV_HEADS, dq=HEAD_DIM, scale=HEAD_DIM ** -0.5,
        **common)

    na_lat = _natten(p, lp["na_bias"], **common)
    o_na = _flash(p, p, p, p, q_col0=P_NQ, k_col0=P_NK, v_col0=P_NV, g_col0=P_NG, v_stride=1,
                  n_kv_heads=NA_HEADS, r=1, dq=HEAD_DIM, scale=HEAD_DIM ** -0.5, ctx_only=True,
                  tq=256, into=na_lat, **common)

    q_norm = jnp.pad(lp["mla_q_norm"], (0, MQA_PAD - MLA_Q_LORA)).reshape(1, MQA_PAD)
    qm = _mla_q(p, q_norm, _layout_w_uq(lp["w_uq"]), *rope_m, seq)
    km, vm = _mla_kv(p, lp["mla_kv_norm"], lp["w_ukv"].astype(BF16), *rope_m, seq)
    o_mla = _attention(
        qm, km, vm, p, 512, q_col0=0, k_col0=0, v_col0=0, g_col0=P_MG, v_stride=1,
        n_kv_heads=MLA_HEADS, r=1, dq=MLA_QK_PAD, scale=(MLA_NOPE + MLA_ROPE) ** -0.5,
        **common)

    weights = [lp[n].astype(BF16) for n in ("w_o_ssm", "w_o_gqa", "w_o_na", "w_o_mla")]
    ymix = _merge([o_ssm, o_gqa, o_na, o_mla], weights, p, seq)
    return _out_projection(ymix, lp["w_out"].astype(BF16), xa, lp["norm_post"], gate, n_lat, seq)


def kernel(x, c, ctx, c_ctx, ada_w, ada_b, norm_pre, norm_post, w_in, conv_w, conv_b, a_log, dt_bias,
           d_skip, ssm_norm, w_o_ssm, gqa_q_norm, gqa_k_norm, w_o_gqa, na_rpb, w_o_na, mla_q_norm,
           w_uq, mla_kv_norm, w_ukv, w_o_mla, w_out):
    batch, n_lat, d = x.shape
    n_ctx = ctx.shape[1]
    seq = n_lat + n_ctx
    stacked = dict(ada_w=ada_w, ada_b=ada_b, norm_pre=norm_pre, norm_post=norm_post, w_in=w_in,
                   conv_w=conv_w, conv_b=conv_b, a_log=a_log, dt_bias=dt_bias, d_skip=d_skip,
                   ssm_norm=ssm_norm, w_o_ssm=w_o_ssm, gqa_q_norm=gqa_q_norm, gqa_k_norm=gqa_k_norm,
                   w_o_gqa=w_o_gqa, na_bias=_na_bias_table(na_rpb), w_o_na=w_o_na,
                   mla_q_norm=mla_q_norm, w_uq=w_uq,
                   mla_kv_norm=mla_kv_norm, w_ukv=w_ukv, w_o_mla=w_o_mla, w_out=w_out)
    xa = jnp.concatenate([x, ctx], axis=1).reshape(batch * seq, d)
    cc = jnp.concatenate([c, c_ctx[None, :], jnp.zeros((8 - batch - 1, d), c.dtype)], axis=0)
    rope_g = _rope_tables(n_lat, n_ctx, HEAD_DIM)
    rope_m = _rope_tables(n_lat, n_ctx, MLA_ROPE)
    for layer in range(ada_w.shape[0]):
        lp = {k: v[layer] for k, v in stacked.items()}
        xa = _layer(xa, cc, rope_g, rope_m, lp, batch, n_lat, seq)
    return xa.reshape(batch, seq, d)[:, :n_lat]
```

```python
import functools
import math

import jax
import jax.numpy as jnp
import numpy as np
from jax import lax
from jax.experimental import pallas as pl
from jax.experimental.pallas import tpu as pltpu

F32 = jnp.float32
BF16 = jnp.bfloat16

GRID_W = 64
EPS = 1e-6
ROPE_THETA = 10000.0

SSM_HEADS = 32
SSM_HEAD_DIM = 64
SSM_INNER = SSM_HEADS * SSM_HEAD_DIM
SSM_GROUPS = 4
SSM_STATE = 128
SSM_CONV = 5
SSM_CHUNK = 128
SSM_CONV_DIM = SSM_INNER + 2 * SSM_GROUPS * SSM_STATE

GQA_HEADS = 8
GQA_KV_HEADS = 4
HEAD_DIM = 128
NA_HEADS = 8
NA_WIN_H = 8
NA_WIN_W = 16
NA_QROWS = 4
NA_KROWS = NA_QROWS + NA_WIN_H
NA_HEADS_PER_STEP = 8

MLA_HEADS = 8
MLA_Q_LORA = 768
MLA_KV_LORA = 512
MLA_NOPE = 128
MLA_ROPE = 64
MLA_QK_PAD = 256

N_BRANCH = 4
LANE = 128
VMEM_LIMIT = 56 * 1024 * 1024
NEG_BIG = -1e30
FLASH_TK = 256
GQA_TQ = 512
MLA_TQ = 1024

_SIZES = (SSM_INNER, SSM_CONV_DIM, 2 * SSM_HEADS, 1024, 512, 512, 1024, 1024, 1024, 1024, 1024,
          MLA_Q_LORA, MLA_KV_LORA, MLA_ROPE, 1024, N_BRANCH * 2048)
_OFF = dict(zip(("z", "xbc", "dtr", "gq", "gk", "gv", "gg", "nq", "nk", "nv", "ng",
                 "mqa", "mkva", "mkr", "mg", "mix"), np.cumsum((0,) + _SIZES[:-1]).tolist()))

P_MIX, P_Z, P_XBC = 0, 8192, 10240
P_GQ, P_GG, P_NQ, P_NK, P_NV, P_NG, P_MG, P_MQA = (13312, 14336, 15360, 16384, 17408, 18432,
                                                    19456, 20480)
P_GK, P_GV, P_MKVA, P_MKR = 21504, 22016, 22528, 23040
P_WIDTH = 23552
MQA_PAD = 1024


def _pick(n, candidates):
    for c in candidates:
        if n % c == 0:
            return c
    raise ValueError(f"no tile for {n} among {candidates}")


def _params(sem):
    return pltpu.CompilerParams(dimension_semantics=sem, vmem_limit_bytes=VMEM_LIMIT)


def _silu(v):
    return v * (1.0 / (1.0 + jnp.exp(-v)))


def _rope(v, cos, sin):
    return v * cos + pltpu.roll(v, 64, axis=1) * sin


def _mod_kernel(c_ref, w_ref, b_ref, o_ref):
    h = _silu(c_ref[...]).astype(BF16)
    o_ref[...] = jnp.dot(h, w_ref[...].astype(BF16), preferred_element_type=F32) + b_ref[...]


def _modulation(cc, ada_w, ada_b):
    rows, d = cc.shape
    n = ada_w.shape[1]
    tn = _pick(n, (512, 256, 128))
    return pl.pallas_call(
        _mod_kernel,
        grid=(n // tn,),
        in_specs=[pl.BlockSpec((rows, d), lambda j: (0, 0)),
                  pl.BlockSpec((d, tn), lambda j: (0, j)),
                  pl.BlockSpec((1, tn), lambda j: (0, j))],
        out_specs=pl.BlockSpec((rows, tn), lambda j: (0, j)),
        out_shape=jax.ShapeDtypeStruct((rows, n), F32),
        compiler_params=_params(("parallel",)),
        name="adaln_mod",
    )(cc, ada_w, ada_b.reshape(1, n))


def _is_ctx_rows(i, tm, tiles_per_batch, n_lat):
    row = (i % tiles_per_batch) * tm + lax.broadcasted_iota(jnp.int32, (tm, 1), 0)
    return row >= n_lat


def _inproj_kernel(x_ref, nw_ref, sc_ref, sh_ref, w_ref, ws_ref, o_ref, os_ref, h_ref, *,
                   tm, tiles_per_batch, n_lat):
    i = pl.program_id(0)

    @pl.when(pl.program_id(1) == 0)
    def _():
        x = x_ref[...]
        y = x * lax.rsqrt(jnp.mean(x * x, axis=-1, keepdims=True) + EPS) * nw_ref[...]
        is_ctx = _is_ctx_rows(i, tm, tiles_per_batch, n_lat)
        sc = jnp.where(is_ctx, sc_ref[1:2, :], sc_ref[0:1, :])
        sh = jnp.where(is_ctx, sh_ref[1:2, :], sh_ref[0:1, :])
        h = (y * (1.0 + sc) + sh).astype(BF16)
        h_ref[...] = h
        os_ref[...] = jnp.dot(h, ws_ref[...], preferred_element_type=F32)

    o_ref[...] = jnp.dot(h_ref[...], w_ref[...], preferred_element_type=F32).astype(o_ref.dtype)


def _in_projection(xa, norm_w, scale, shift, w_p, w_side, n_lat, seq):
    t, d = xa.shape
    tm = _pick(seq, (768, 512, 256))
    tn = _pick(P_WIDTH, (1024, 512))
    tpb = seq // tm
    kern = functools.partial(_inproj_kernel, tm=tm, tiles_per_batch=tpb, n_lat=n_lat)
    return pl.pallas_call(
        kern,
        grid=(t // tm, P_WIDTH // tn),
        in_specs=[pl.BlockSpec((tm, d), lambda i, j: (i, 0)),
                  pl.BlockSpec((1, d), lambda i, j: (0, 0)),
                  pl.BlockSpec((None, 2, d), lambda i, j: (i // tpb, 0, 0)),
                  pl.BlockSpec((None, 2, d), lambda i, j: (i // tpb, 0, 0)),
                  pl.BlockSpec((d, tn), lambda i, j: (0, j)),
                  pl.BlockSpec((d, 2 * LANE), lambda i, j: (0, 0))],
        out_specs=[pl.BlockSpec((tm, tn), lambda i, j: (i, j)),
                   pl.BlockSpec((tm, 2 * LANE), lambda i, j: (i, 0))],
        out_shape=[jax.ShapeDtypeStruct((t, P_WIDTH), BF16),
                   jax.ShapeDtypeStruct((t, 2 * LANE), F32)],
        scratch_shapes=[pltpu.VMEM((tm, d), BF16)],
        compiler_params=_params(("parallel", "arbitrary")),
        name="in_proj",
    )(xa, norm_w.reshape(1, d), scale, shift, w_p, w_side)


def _conv_kernel(u_ref, prev_ref, next_ref, w_ref, b_ref, s_ref, o_ref, *,
                 tm, tiles_per_batch, lat_tiles):
    ib = pl.program_id(0) % tiles_per_batch
    first = jnp.logical_or(ib == 0, ib == lat_tiles)
    last = jnp.logical_or(ib == lat_tiles - 1, ib == tiles_per_batch - 1)
    half = SSM_CONV // 2
    u = u_ref[...]
    acc = b_ref[...] + w_ref[half:half + 1, :] * u.astype(F32)
    for idx, k in enumerate(k for k in range(SSM_CONV) if k != half):
        acc = acc + w_ref[k:k + 1, :] * jnp.dot(s_ref[idx], u, preferred_element_type=F32)
    prev = jnp.where(first, 0.0, prev_ref[...].astype(F32))
    nxt = jnp.where(last, 0.0, next_ref[...].astype(F32))
    row = lax.broadcasted_iota(jnp.int32, (8, 1), 0)
    top = jnp.zeros_like(prev)
    bot = jnp.zeros_like(nxt)
    for k in range(half):
        reach = half - k
        top = top + w_ref[k:k + 1, :] * jnp.where(row < reach, pltpu.roll(prev, reach, axis=0), 0.0)
        kk = SSM_CONV - 1 - k
        bot = bot + w_ref[kk:kk + 1, :] * jnp.where(row >= 8 - reach,
                                                     pltpu.roll(nxt, 8 - reach, axis=0), 0.0)
    y = jnp.concatenate([acc[0:8] + top, acc[8:tm - 8], acc[tm - 8:tm] + bot], axis=0)
    o_ref[...] = _silu(y).astype(o_ref.dtype)


def _conv_silu(p, conv_w, conv_b, n_lat, seq):
    t = p.shape[0]
    tm = 256
    tc = 1024
    tpb = seq // tm
    n_row8 = t // 8
    col0 = P_XBC // tc
    kern = functools.partial(_conv_kernel, tm=tm, tiles_per_batch=tpb, lat_tiles=n_lat // tm)
    half = SSM_CONV // 2
    shifts = np.stack([np.eye(tm, k=k - half, dtype=np.float32) for k in range(SSM_CONV) if k != half])
    return pl.pallas_call(
        kern,
        grid=(t // tm, SSM_CONV_DIM // tc),
        in_specs=[pl.BlockSpec((tm, tc), lambda i, j: (i, col0 + j)),
                  pl.BlockSpec((8, tc), lambda i, j: (jnp.maximum(i * (tm // 8) - 1, 0), col0 + j)),
                  pl.BlockSpec((8, tc), lambda i, j: (jnp.minimum((i + 1) * (tm // 8), n_row8 - 1),
                                                      col0 + j)),
                  pl.BlockSpec((8, tc), lambda i, j: (0, j)),
                  pl.BlockSpec((1, tc), lambda i, j: (0, j)),
                  pl.BlockSpec((SSM_CONV - 1, tm, tm), lambda i, j: (0, 0, 0))],
        out_specs=pl.BlockSpec((tm, tc), lambda i, j: (i, j)),
        out_shape=jax.ShapeDtypeStruct((t, SSM_CONV_DIM), BF16),
        compiler_params=_params(("parallel", "parallel")),
        name="ssm_conv",
    )(p, p, p, jnp.pad(conv_w, ((0, 8 - SSM_CONV), (0, 0))), conv_b.reshape(1, -1),
      jnp.asarray(shifts, BF16))


def _split_dot(a_bf16, v):
    v1 = v.astype(BF16)
    r1 = v - v1.astype(F32)
    v2 = r1.astype(BF16)
    v3 = (r1 - v2.astype(F32)).astype(BF16)
    return jnp.dot(jnp.concatenate([a_bf16, a_bf16, a_bf16], axis=1),
                   jnp.concatenate([v1, v2, v3], axis=0), preferred_element_type=F32)


def _ssd_kernel(uf_ref, ub_ref, dtf_ref, dtb_ref, alog_ref, bias_ref, eh_ref, yf_ref, yb_ref,
                stf_ref, stb_ref):
    @pl.when(pl.program_id(1) == 0)
    def _():
        stf_ref[...] = jnp.zeros_like(stf_ref)
        stb_ref[...] = jnp.zeros_like(stb_ref)

    _ssd_chunk(True, uf_ref, dtf_ref, alog_ref[0], bias_ref[0], eh_ref, yf_ref, stf_ref)
    _ssd_chunk(False, ub_ref, dtb_ref, alog_ref[1], bias_ref[1], eh_ref, yb_ref, stb_ref)


def _ssd_chunk(fwd, u_ref, dtr_ref, a_log, dt_bias, eh_ref, y_ref, st_ref):
    cl = SSM_CHUNK
    raw = dtr_ref[...] + dt_bias
    dt = jnp.maximum(raw, 0.0) + jnp.log(1.0 + jnp.exp(-jnp.abs(raw)))
    da = dt * (-jnp.exp(a_log))
    r = lax.broadcasted_iota(jnp.int32, (cl, cl), 0)
    c = lax.broadcasted_iota(jnp.int32, (cl, cl), 1)
    tri = (r >= c) if fwd else (r <= c)
    acs = _split_dot(jnp.where(tri, 1.0, 0.0).astype(BF16), da)
    total = acs[cl - 1:cl, :] if fwd else acs[0:1, :]
    e_acs = jnp.exp(acs)
    w_end = dt * jnp.exp(total - acs)
    acs_t = acs.T
    dt_t = dt.T

    def hi_lo(v):
        hi = v.astype(BF16)
        return jnp.concatenate([hi, (v - hi.astype(F32)).astype(BF16)], axis=1)

    ex = jnp.dot(jnp.concatenate([hi_lo(w_end), hi_lo(e_acs)], axis=0), eh_ref[...],
                 preferred_element_type=F32)
    w_exp = ex[0:cl]
    e_exp = ex[cl:2 * cl]
    dec = e_exp[cl - 1:cl, :] if fwd else e_exp[0:1, :]
    xw =(u_ref[:, 0:SSM_INNER].astype(F32) * w_exp).astype(BF16)
    lane = lax.broadcasted_iota(jnp.int32, (cl, LANE), 1)
    gw = SSM_INNER // SSM_GROUPS
    hpg = SSM_HEADS // SSM_GROUPS
    for g in range(SSM_GROUPS):
        b_g = u_ref[:, SSM_INNER + g * SSM_STATE:SSM_INNER + (g + 1) * SSM_STATE]
        c_off = SSM_INNER + SSM_GROUPS * SSM_STATE
        c_g = u_ref[:, c_off + g * SSM_STATE:c_off + (g + 1) * SSM_STATE]
        cb = lax.dot_general(c_g, b_g, (((1,), (1,)), ((), ())), preferred_element_type=F32)
        st_g = st_ref[:, g * gw:(g + 1) * gw]
        y_off = jnp.dot(c_g, st_g.astype(BF16), preferred_element_type=F32)
        ys = []
        for k in range(hpg // 2):
            ms = []
            for h in (g * hpg + 2 * k, g * hpg + 2 * k + 1):
                seg = acs[:, h:h + 1] - acs_t[h:h + 1, :]
                dec_h = jnp.exp(jnp.where(tri, seg, NEG_BIG))
                ms.append((dec_h * cb * dt_t[h:h + 1, :]).astype(BF16))
            x_p = u_ref[:, g * gw + k * LANE:g * gw + (k + 1) * LANE]
            zero = jnp.zeros_like(x_p)
            rhs = jnp.concatenate([jnp.where(lane < SSM_HEAD_DIM, x_p, zero),
                                   jnp.where(lane >= SSM_HEAD_DIM, x_p, zero)], axis=0)
            ys.append(jnp.dot(jnp.concatenate(ms, axis=1), rhs, preferred_element_type=F32))
        y_g = jnp.concatenate(ys, axis=1) + y_off * e_exp[:, g * gw:(g + 1) * gw]
        y_ref[:, g * gw:(g + 1) * gw] = y_g.astype(y_ref.dtype)
        upd = lax.dot_general(b_g, xw[:, g * gw:(g + 1) * gw], (((0,), (0,)), ((), ())),
                              preferred_element_type=F32)
        st_ref[:, g * gw:(g + 1) * gw] = st_g * dec[:, g * gw:(g + 1) * gw] + upd


def _ssd(u, dtr, a_log, dt_bias, batch, n_lat, seq):
    cl = SSM_CHUNK
    nch, nlat, nctx = seq // cl, n_lat // cl, (seq - n_lat) // cl

    def fchunk(c):
        return jnp.where(c < nctx, nlat + c, c - nctx)

    def bchunk(c):
        return jnp.where(c < nctx, nlat + nctx - 1 - c, nlat - 1 - (c - nctx))

    def pad_heads(v):
        return jnp.pad(v, ((0, 0), (0, LANE - SSM_HEADS))).reshape(2, 1, LANE)

    eh = np.zeros((2 * LANE, SSM_INNER), np.float32)
    for h in range(SSM_HEADS):
        eh[h, h * SSM_HEAD_DIM:(h + 1) * SSM_HEAD_DIM] = 1.0
        eh[LANE + h, h * SSM_HEAD_DIM:(h + 1) * SSM_HEAD_DIM] = 1.0
    return pl.pallas_call(
        _ssd_kernel,
        grid=(batch, nch),
        in_specs=[pl.BlockSpec((cl, SSM_CONV_DIM), lambda b, c: (b * nch + fchunk(c), 0)),
                  pl.BlockSpec((cl, SSM_CONV_DIM), lambda b, c: (b * nch + bchunk(c), 0)),
                  pl.BlockSpec((cl, LANE), lambda b, c: (b * nch + fchunk(c), 0)),
                  pl.BlockSpec((cl, LANE), lambda b, c: (b * nch + bchunk(c), 1)),
                  pl.BlockSpec((2, 1, LANE), lambda b, c: (0, 0, 0)),
                  pl.BlockSpec((2, 1, LANE), lambda b, c: (0, 0, 0)),
                  pl.BlockSpec((2 * LANE, SSM_INNER), lambda b, c: (0, 0))],
        out_specs=[pl.BlockSpec((cl, SSM_INNER), lambda b, c: (b * nch + fchunk(c), 0)),
                   pl.BlockSpec((cl, SSM_INNER), lambda b, c: (b * nch + bchunk(c), 0))],
        out_shape=[jax.ShapeDtypeStruct((u.shape[0], SSM_INNER), BF16)] * 2,
        scratch_shapes=[pltpu.VMEM((SSM_STATE, SSM_INNER), F32)] * 2,
        compiler_params=_params(("parallel", "arbitrary")),
        name="ssd_scan",
    )(u, u, dtr, dtr, pad_heads(a_log), pad_heads(dt_bias), jnp.asarray(eh, BF16))


def _ssm_out_kernel(yf_ref, yb_ref, x_ref, z_ref, skip_ref, nw_ref, o_ref):
    z = z_ref[...].astype(F32)
    g = (yf_ref[...].astype(F32) + yb_ref[...].astype(F32)
         + skip_ref[...] * x_ref[...].astype(F32)) * _silu(z)
    gw = SSM_INNER // SSM_GROUPS
    for k in range(SSM_GROUPS):
        gk = g[:, k * gw:(k + 1) * gw]
        gk = gk * lax.rsqrt(jnp.mean(gk * gk, axis=-1, keepdims=True) + EPS)
        o_ref[:, k * gw:(k + 1) * gw] = (gk * nw_ref[:, k * gw:(k + 1) * gw]).astype(o_ref.dtype)


def _ssm_out(yf, yb, u, p, d_skip, ssm_norm, seq):
    t = u.shape[0]
    tm = _pick(seq, (384, 256, 128))
    w = SSM_INNER
    return pl.pallas_call(
        _ssm_out_kernel,
        grid=(t // tm,),
        in_specs=[pl.BlockSpec((tm, w), lambda i: (i, 0)),
                  pl.BlockSpec((tm, w), lambda i: (i, 0)),
                  pl.BlockSpec((tm, w), lambda i: (i, 0)),
                  pl.BlockSpec((tm, w), lambda i: (i, P_Z // w)),
                  pl.BlockSpec((1, w), lambda i: (0, 0)),
                  pl.BlockSpec((1, w), lambda i: (0, 0))],
        out_specs=pl.BlockSpec((tm, w), lambda i: (i, 0)),
        out_shape=jax.ShapeDtypeStruct((t, w), BF16),
        compiler_params=_params(("parallel",)),
        name="ssm_gated_norm",
    )(yf, yb, u, p, jnp.repeat(d_skip, SSM_HEAD_DIM).reshape(1, w), ssm_norm.reshape(1, w))


def _head_norm_rope_kernel(x_ref, nw_ref, cos_ref, sin_ref, o_ref, *, n_heads):
    for h in range(n_heads):
        sl = slice(h * LANE, (h + 1) * LANE)
        x = x_ref[:, sl].astype(F32)
        y = x * lax.rsqrt(jnp.mean(x * x, axis=-1, keepdims=True) + EPS) * nw_ref[...]
        o_ref[:, sl] = _rope(y, cos_ref[...], sin_ref[...]).astype(o_ref.dtype)


def _head_norm_rope(p, col0, n_heads, norm_w, cos, sin, seq):
    t = p.shape[0]
    tm = _pick(seq, (768, 512, 256))
    tpb = seq // tm
    w = n_heads * LANE
    return pl.pallas_call(
        functools.partial(_head_norm_rope_kernel, n_heads=n_heads),
        grid=(t // tm,),
        in_specs=[pl.BlockSpec((tm, w), lambda i: (i, col0 // w)),
                  pl.BlockSpec((1, LANE), lambda i: (0, 0)),
                  pl.BlockSpec((tm, LANE), lambda i: (i % tpb, 0)),
                  pl.BlockSpec((tm, LANE), lambda i: (i % tpb, 0))],
        out_specs=pl.BlockSpec((tm, w), lambda i: (i, 0)),
        out_shape=jax.ShapeDtypeStruct((t, w), BF16),
        compiler_params=_params(("parallel",)),
        name="head_norm_rope",
    )(p, norm_w.reshape(1, LANE), cos, sin)


def _mla_q_kernel(x_ref, nw_ref, w_ref, cos_ref, sin_ref, o_ref):
    x = x_ref[...].astype(F32)
    ms = jnp.sum(x * x, axis=-1, keepdims=True) * (1.0 / MLA_Q_LORA)
    h = (x * lax.rsqrt(ms + EPS) * nw_ref[...]).astype(BF16)
    for hd in range(MLA_HEADS):
        c0 = hd * MLA_QK_PAD
        res = jnp.dot(h, w_ref[:, c0:c0 + MLA_QK_PAD], preferred_element_type=F32)
        o_ref[:, c0:c0 + LANE] = res[:, 0:LANE].astype(o_ref.dtype)
        o_ref[:, c0 + LANE:c0 + MLA_QK_PAD] = _rope(res[:, LANE:], cos_ref[...],
                                                    sin_ref[...]).astype(o_ref.dtype)


def _mla_q(p, q_norm, w_uq, cos, sin, seq):
    t = p.shape[0]
    tm = _pick(seq, (768, 512, 256))
    tpb = seq // tm
    kdim = MQA_PAD
    wo = MLA_HEADS * MLA_QK_PAD
    return pl.pallas_call(
        _mla_q_kernel,
        grid=(t // tm,),
        in_specs=[pl.BlockSpec((tm, kdim), lambda i: (i, P_MQA // kdim)),
                  pl.BlockSpec((1, kdim), lambda i: (0, 0)),
                  pl.BlockSpec((kdim, wo), lambda i: (0, 0)),
                  pl.BlockSpec((tm, LANE), lambda i: (i % tpb, 0)),
                  pl.BlockSpec((tm, LANE), lambda i: (i % tpb, 0))],
        out_specs=pl.BlockSpec((tm, wo), lambda i: (i, 0)),
        out_shape=jax.ShapeDtypeStruct((t, wo), BF16),
        compiler_params=_params(("parallel",)),
        name="mla_q_up",
    )(p, q_norm, w_uq, cos, sin)


def _mla_kv_kernel(x_ref, kr_ref, nw_ref, w_ref, cos_ref, sin_ref, k_ref, v_ref):
    x = x_ref[...].astype(F32)
    h = (x * lax.rsqrt(jnp.mean(x * x, axis=-1, keepdims=True) + EPS) * nw_ref[...]).astype(BF16)
    k_rope = _rope(kr_ref[...].astype(F32), cos_ref[...], sin_ref[...]).astype(k_ref.dtype)
    for hd in range(MLA_HEADS):
        c0 = hd * MLA_QK_PAD
        res = jnp.dot(h, w_ref[:, c0:c0 + 2 * LANE], preferred_element_type=F32)
        k_ref[:, c0:c0 + LANE] = res[:, 0:LANE].astype(k_ref.dtype)
        k_ref[:, c0 + LANE:c0 + MLA_QK_PAD] = k_rope
        v_ref[:, hd * LANE:(hd + 1) * LANE] = res[:, LANE:].astype(v_ref.dtype)


def _mla_kv(p, kv_norm, w_ukv, cos, sin, seq):
    t = p.shape[0]
    tm = _pick(seq, (768, 512, 256))
    tpb = seq // tm
    kdim = MLA_KV_LORA
    wk = MLA_HEADS * MLA_QK_PAD
    return pl.pallas_call(
        _mla_kv_kernel,
        grid=(t // tm,),
        in_specs=[pl.BlockSpec((tm, kdim), lambda i: (i, P_MKVA // kdim)),
                  pl.BlockSpec((tm, LANE), lambda i: (i, P_MKR // LANE)),
                  pl.BlockSpec((1, kdim), lambda i: (0, 0)),
                  pl.BlockSpec((kdim, wk), lambda i: (0, 0)),
                  pl.BlockSpec((tm, LANE), lambda i: (i % tpb, 0)),
                  pl.BlockSpec((tm, LANE), lambda i: (i % tpb, 0))],
        out_specs=[pl.BlockSpec((tm, wk), lambda i: (i, 0)),
                   pl.BlockSpec((tm, MLA_HEADS * LANE), lambda i: (i, 0))],
        out_shape=[jax.ShapeDtypeStruct((t, wk), BF16),
                   jax.ShapeDtypeStruct((t, MLA_HEADS * LANE), BF16)],
        compiler_params=_params(("parallel",)),
        name="mla_kv_up",
    )(p, p, kv_norm.reshape(1, kdim), w_ukv, cos, sin)


def _flash_kernel(q_ref, k_ref, v_ref, g_ref, *rest, r, tq, dq, tk, n_keys, scale):
    o_ref = rest[-1]
    q = jnp.concatenate([q_ref[:, j * dq:(j + 1) * dq] for j in range(r)], axis=0)
    q = (q.astype(F32) * (scale * math.log2(math.e))).astype(BF16)
    m_rows = r * tq
    m = jnp.full((m_rows, 1), NEG_BIG, F32)
    l = jnp.zeros((m_rows, LANE), F32)
    acc = jnp.zeros((m_rows, LANE), F32)
    for c in range(n_keys // tk):
        k_c = k_ref[c * tk:(c + 1) * tk, :]
        v_c = v_ref[c * tk:(c + 1) * tk, :]
        s = lax.dot_general(q, k_c, (((1,), (1,)), ((), ())), preferred_element_type=F32)
        m_new = jnp.maximum(m, jnp.max(s, axis=-1, keepdims=True))
        alpha = jnp.exp2(m - m_new)
        pr = jnp.exp2(s - m_new)
        part = pr[:, 0:LANE]
        for j in range(1, tk // LANE):
            part = part + pr[:, j * LANE:(j + 1) * LANE]
        l = alpha * l + part
        acc = alpha * acc + jnp.dot(pr.astype(BF16), v_c, preferred_element_type=F32)
        m = m_new
    o = acc / jnp.sum(l, axis=-1, keepdims=True)
    for j in range(r):
        gate = _silu(g_ref[:, j * LANE:(j + 1) * LANE].astype(F32))
        o_ref[:, j * LANE:(j + 1) * LANE] = (o[j * tq:(j + 1) * tq] * gate).astype(o_ref.dtype)


def _flash(q, k, v, gates, *, q_col0, k_col0, v_col0, g_col0, v_stride, n_kv_heads, r, dq, scale,
           batch, n_lat, seq, ctx_only, tq, into=None):
    n_ctx = seq - n_lat
    tk = FLASH_TK
    if ctx_only:
        tq = n_ctx
        q_blk0, n_q, kv_rows, kv_blk0 = n_lat // tq, 1, n_ctx, n_lat // n_ctx
    else:
        q_blk0, n_q, kv_rows, kv_blk0 = 0, n_lat // tq, seq, 0
    assert kv_rows % tk == 0
    kern = functools.partial(_flash_kernel, r=r, tq=tq, dq=dq, tk=tk, n_keys=kv_rows, scale=scale)
    w_out = n_kv_heads * r * LANE
    args = [a.reshape(batch, seq, a.shape[-1]) for a in (q, k, v, gates)]
    in_specs = [pl.BlockSpec((None, tq, r * dq), lambda b, g, i: (b, q_blk0 + i, q_col0 // (r * dq) + g)),
                pl.BlockSpec((None, kv_rows, dq), lambda b, g, i: (b, kv_blk0, k_col0 // dq + g)),
                pl.BlockSpec((None, kv_rows, LANE),
                             lambda b, g, i: (b, kv_blk0, v_col0 // LANE + v_stride * g)),
                pl.BlockSpec((None, tq, r * LANE),
                             lambda b, g, i: (b, q_blk0 + i, g_col0 // (r * LANE) + g))]
    aliases = {}
    if into is not None:
        args.append(into.reshape(batch, seq, w_out))
        in_specs.append(pl.BlockSpec(memory_space=pl.ANY))
        aliases = {len(args) - 1: 0}
    out = pl.pallas_call(
        kern,
        grid=(batch, n_kv_heads, n_q),
        in_specs=in_specs,
        out_specs=pl.BlockSpec((None, tq, r * LANE), lambda b, g, i: (b, q_blk0 + i, g)),
        out_shape=jax.ShapeDtypeStruct((batch, seq, w_out), BF16),
        input_output_aliases=aliases,
        compiler_params=_params(("parallel", "parallel", "parallel")),
        name="flash_ctx" if ctx_only else "flash_latent",
    )(*args)
    return out.reshape(batch * seq, w_out)


def _attention(q, k, v, gates, tq, **kw):
    lat = _flash(q, k, v, gates, ctx_only=False, tq=tq, **kw)
    return _flash(q, k, v, gates, ctx_only=True, tq=tq, into=lat, **kw)


def _natten_kernel(q_ref, k0_ref, k1_ref, k2_ref, kc_ref, v0_ref, v1_ref, v2_ref, vc_ref, g_ref,
                   bias_ref, o_ref, *, scale, n_heads):
    n_loc = bias_ref.shape[-1]
    for h in range(n_heads):
        sl = slice(h * LANE, (h + 1) * LANE)
        q = (q_ref[:, sl].astype(F32) * (scale * math.log2(math.e))).astype(BF16)
        k = jnp.concatenate([k0_ref[:, sl], k1_ref[:, sl], k2_ref[:, sl], kc_ref[:, sl]], axis=0)
        v = jnp.concatenate([v0_ref[:, sl], v1_ref[:, sl], v2_ref[:, sl], vc_ref[:, sl]], axis=0)
        s = lax.dot_general(q, k, (((1,), (1,)), ((), ())), preferred_element_type=F32)
        s = jnp.concatenate([s[:, :n_loc] + bias_ref[h], s[:, n_loc:]], axis=1)
        pr = jnp.exp2(s - jnp.max(s, axis=-1, keepdims=True))
        l = jnp.sum(pr, axis=-1, keepdims=True)
        o = jnp.dot(pr.astype(BF16), v, preferred_element_type=F32) / l
        o_ref[:, sl] = (o * _silu(g_ref[:, sl].astype(F32))).astype(o_ref.dtype)


def _na_bias_table(rpb):
    qx = np.arange(GRID_W)
    c0 = np.clip(qx - NA_WIN_W // 2, 0, GRID_W - NA_WIN_W)
    col_ok = (qx[None, :] >= c0[:, None]) & (qx[None, :] < c0[:, None] + NA_WIN_W)
    dx = qx[None, :] - qx[:, None] + NA_WIN_W - 1
    pick = np.zeros((2 * NA_WIN_W - 1, GRID_W * GRID_W), np.float32)
    qi, ki = np.nonzero(col_ok)
    pick[dx[qi, ki], qi * GRID_W + ki] = 1.0
    by_dx = jnp.einsum("...d,dn->...n", rpb, pick, precision=lax.Precision.HIGHEST)
    qy, ky = np.arange(NA_QROWS), np.arange(NA_KROWS)
    dys, oks = [], []
    for q_off, first_key in ((0, None), (NA_WIN_H // 2, "q"), (NA_WIN_H, NA_KROWS - NA_WIN_H)):
        r0 = qy if first_key == "q" else np.full_like(qy, 0 if first_key is None else first_key)
        oks.append((ky[None, :] >= r0[:, None]) & (ky[None, :] < r0[:, None] + NA_WIN_H))
        dys.append(np.clip(ky[None, :] - (qy[:, None] + q_off) + NA_WIN_H - 1, 0, 2 * NA_WIN_H - 2))
    rows = jnp.take(by_dx, np.stack(dys).reshape(-1), axis=-2)
    lead = rows.shape[:-2]
    nl = len(lead)
    rows = rows.reshape(lead + (3, NA_QROWS, NA_KROWS, GRID_W, GRID_W))
    rows = rows.transpose(tuple(range(nl)) + (nl, nl + 1, nl + 3, nl + 2, nl + 4))
    ok = np.stack(oks)[:, :, None, :, None] & col_ok[None, None, :, None, :]
    tab = jnp.where(jnp.asarray(ok), rows * math.log2(math.e), NEG_BIG)
    return tab.reshape(lead + (3, NA_QROWS * GRID_W, NA_KROWS * GRID_W))


def _natten(p, bias, batch, n_lat, seq):
    n_ctx = seq - n_lat
    tq = NA_QROWS * GRID_W
    n_blk = n_lat // tq
    assert n_blk >= 3 and n_lat % tq == 0 and n_ctx % tq == 0
    rb = seq // tq
    cb = seq // n_ctx
    nk = NA_KROWS * GRID_W
    hps = NA_HEADS_PER_STEP
    hw = hps * LANE

    def kblk(i):
        return jnp.clip(i - 1, 0, n_blk - 3)

    def kspec(col0, j):
        return pl.BlockSpec((tq, hw), lambda b, h, i: (b * rb + kblk(i) + j, col0 // hw + h))

    def cspec(col0):
        return pl.BlockSpec((n_ctx, hw), lambda b, h, i: (b * cb + n_lat // n_ctx, col0 // hw + h))

    def qspec(col0):
        return pl.BlockSpec((tq, hw), lambda b, h, i: (b * rb + i, col0 // hw + h))

    def btype(i):
        return jnp.where(i == 0, 0, jnp.where(i == n_blk - 1, 2, 1))

    return pl.pallas_call(
        functools.partial(_natten_kernel, scale=HEAD_DIM ** -0.5, n_heads=hps),
        grid=(batch, NA_HEADS // hps, n_blk),
        in_specs=[qspec(P_NQ), kspec(P_NK, 0), kspec(P_NK, 1), kspec(P_NK, 2), cspec(P_NK),
                  kspec(P_NV, 0), kspec(P_NV, 1), kspec(P_NV, 2), cspec(P_NV), qspec(P_NG),
                  pl.BlockSpec((hps, None, tq, nk), lambda b, h, i: (h, btype(i), 0, 0))],
        out_specs=pl.BlockSpec((tq, hw), lambda b, h, i: (b * rb + i, h)),
        out_shape=jax.ShapeDtypeStruct((p.shape[0], NA_HEADS * LANE), BF16),
        compiler_params=_params(("parallel", "parallel", "parallel")),
        name="natten_latent",
    )(p, p, p, p, p, p, p, p, p, p, bias)


def _merge_kernel(o0_ref, o1_ref, o2_ref, o3_ref, w0_ref, w1_ref, w2_ref, w3_ref,
                  m0_ref, m1_ref, m2_ref, m3_ref, y_ref):
    acc = None
    for o_ref, w_ref, m_ref in ((o0_ref, w0_ref, m0_ref), (o1_ref, w1_ref, m1_ref),
                                (o2_ref, w2_ref, m2_ref), (o3_ref, w3_ref, m3_ref)):
        gate = 1.0 / (1.0 + jnp.exp(-m_ref[...].astype(F32)))
        term = gate * jnp.dot(o_ref[...], w_ref[...], preferred_element_type=F32)
        acc = term if acc is None else acc + term
    y_ref[...] = acc.astype(y_ref.dtype)


def _merge(outs, weights, p, seq):
    t = p.shape[0]
    d = weights[0].shape[1]
    tm = _pick(seq, (768, 512, 256))
    tn = 512
    o_specs = [pl.BlockSpec((tm, o.shape[1]), lambda i, j: (i, 0)) for o in outs]
    w_specs = [pl.BlockSpec((w.shape[0], tn), lambda i, j: (0, j)) for w in weights]
    m_specs = [pl.BlockSpec((tm, tn), lambda i, j, b=b: (i, (P_MIX + b * d) // tn + j))
               for b in range(N_BRANCH)]
    return pl.pallas_call(
        _merge_kernel,
        grid=(t // tm, d // tn),
        in_specs=o_specs + w_specs + m_specs,
        out_specs=pl.BlockSpec((tm, tn), lambda i, j: (i, j)),
        out_shape=jax.ShapeDtypeStruct((t, d), BF16),
        compiler_params=_params(("parallel", "parallel")),
        name="branch_merge",
    )(*outs, *weights, p, p, p, p)


def _out_kernel(y_ref, w_ref, x_ref, nw_ref, gate_ref, o_ref, *, tm, tiles_per_batch, n_lat):
    z = jnp.dot(y_ref[...], w_ref[...], preferred_element_type=F32)
    zn = z * lax.rsqrt(jnp.mean(z * z, axis=-1, keepdims=True) + EPS) * nw_ref[...]
    is_ctx = _is_ctx_rows(pl.program_id(0), tm, tiles_per_batch, n_lat)
    gate = jnp.where(is_ctx, gate_ref[1:2, :], gate_ref[0:1, :])
    o_ref[...] = x_ref[...] + gate * zn


def _out_projection(y, w_out, xa, norm_w, gate, n_lat, seq):
    t, d = xa.shape
    tm = _pick(seq, (768, 384, 256, 128))
    tpb = seq // tm
    kern = functools.partial(_out_kernel, tm=tm, tiles_per_batch=tpb, n_lat=n_lat)
    return pl.pallas_call(
        kern,
        grid=(t // tm,),
        in_specs=[pl.BlockSpec((tm, d), lambda i: (i, 0)),
                  pl.BlockSpec((d, d), lambda i: (0, 0)),
                  pl.BlockSpec((tm, d), lambda i: (i, 0)),
                  pl.BlockSpec((1, d), lambda i: (0, 0)),
                  pl.BlockSpec((None, 2, d), lambda i: (i // tpb, 0, 0))],
        out_specs=pl.BlockSpec((tm, d), lambda i: (i, 0)),
        out_shape=jax.ShapeDtypeStruct((t, d), F32),
        compiler_params=_params(("parallel",)),
        name="out_proj",
    )(y, w_out, xa, norm_w.reshape(1, d), gate)


def _pairs_apart(w, n_heads, dim):
    lead = w.shape[:-1]
    return w.reshape(lead + (n_heads, dim // 2, 2)).swapaxes(-1, -2).reshape(lead + (n_heads * dim,))


def _rope_tile(w):
    lead = w.shape[:-1]
    pr = w.reshape(lead + (MLA_ROPE // 2, 2))
    zero = jnp.zeros(lead + (MLA_ROPE // 2,), w.dtype)
    return jnp.concatenate([pr[..., 0], zero, pr[..., 1], zero], axis=-1)


def _layout_w_in(w):
    o = _OFF
    k = w.shape[0]

    def seg(name, width):
        return w[:, o[name]:o[name] + width]

    cols = [seg("mix", 8192), seg("z", 2048), seg("xbc", SSM_CONV_DIM),
            _pairs_apart(seg("gq", 1024), GQA_HEADS, HEAD_DIM), seg("gg", 1024),
            seg("nq", 1024), seg("nk", 1024), seg("nv", 1024), seg("ng", 1024), seg("mg", 1024),
            seg("mqa", MLA_Q_LORA), jnp.zeros((k, MQA_PAD - MLA_Q_LORA), w.dtype),
            _pairs_apart(seg("gk", 512), GQA_KV_HEADS, HEAD_DIM), seg("gv", 512),
            seg("mkva", MLA_KV_LORA), _rope_tile(seg("mkr", MLA_ROPE))]
    main = jnp.concatenate(cols, axis=1)
    main = jnp.pad(main, ((0, 0), (0, P_WIDTH - main.shape[1]))).astype(BF16)
    dtr = seg("dtr", 2 * SSM_HEADS)
    zero = jnp.zeros((k, LANE - SSM_HEADS), w.dtype)
    side = jnp.concatenate([dtr[:, :SSM_HEADS], zero, dtr[:, SSM_HEADS:], zero], axis=1).astype(BF16)
    return main, side


def _layout_w_uq(w_uq):
    k = w_uq.shape[0]
    w = w_uq.reshape(k, MLA_HEADS, MLA_NOPE + MLA_ROPE)
    w = jnp.concatenate([w[..., :MLA_NOPE], _rope_tile(w[..., MLA_NOPE:])], axis=-1)
    w = w.reshape(k, MLA_HEADS * MLA_QK_PAD)
    return jnp.pad(w, ((0, MQA_PAD - k), (0, 0))).astype(BF16)


def _rope_tables(n_lat, n_ctx, dim):
    t = np.arange(n_lat)
    quarter = dim // 4
    freqs = ROPE_THETA ** (-jnp.arange(quarter, dtype=F32) / quarter)
    row = jnp.asarray(t // GRID_W, F32)
    col = jnp.asarray(t % GRID_W, F32)
    ang = jnp.concatenate([row[:, None] * freqs, col[:, None] * freqs], axis=-1)
    cos, sin = jnp.cos(ang), jnp.sin(ang)
    pad = 64 - dim // 2
    one, zero = jnp.ones((n_lat, pad), F32), jnp.zeros((n_lat, pad), F32)
    cos_t = jnp.concatenate([cos, one, cos, one], axis=-1)
    sin_t = jnp.concatenate([-sin, zero, sin, zero], axis=-1)
    cos_t = jnp.concatenate([cos_t, jnp.ones((n_ctx, LANE), F32)], axis=0)
    sin_t = jnp.concatenate([sin_t, jnp.zeros((n_ctx, LANE), F32)], axis=0)
    return cos_t, sin_t


def _layer(xa, cc, rope_g, rope_m, lp, batch, n_lat, seq):
    d = xa.shape[1]
    mod = _modulation(cc, lp["ada_w"], lp["ada_b"])

    def per_row(v):
        return jnp.stack([v[:batch], jnp.broadcast_to(v[batch:batch + 1], (batch, d))], axis=1)

    shift, scale, gate = (per_row(mod[:, k * d:(k + 1) * d]) for k in range(3))
    w_main, w_side = _layout_w_in(lp["w_in"])
    p, dtr = _in_projection(xa, lp["norm_pre"], scale, shift, w_main, w_side, n_lat, seq)

    u = _conv_silu(p, lp["conv_w"], lp["conv_b"], n_lat, seq)
    yf, yb = _ssd(u, dtr, lp["a_log"], lp["dt_bias"], batch, n_lat, seq)
    o_ssm = _ssm_out(yf, yb, u, p, lp["d_skip"], lp["ssm_norm"], seq)

    common = dict(batch=batch, n_lat=n_lat, seq=seq)
    qg = _head_norm_rope(p, P_GQ, GQA_HEADS, _pairs_apart(lp["gqa_q_norm"], 1, HEAD_DIM), *rope_g, seq)
    kg = _head_norm_rope(p, P_GK, GQA_KV_HEADS, _pairs_apart(lp["gqa_k_norm"], 1, HEAD_DIM), *rope_g, seq)
    o_gqa = _attention(
        qg, kg, p, p, GQA_TQ, q_col0=0, k_col0=0, v_col0=P_GV, g_col0=P_GG, v_stride=1,
        n_kv_heads=GQA_KV_HEADS, r=GQA_HEADS // GQA_KV_HEADS, dq=HEAD_DIM, scale=HEAD_DIM ** -0.5,
        **common)

    na_lat = _natten(p, lp["na_bias"], **common)
    o_na = _flash(p, p, p, p, q_col0=P_NQ, k_col0=P_NK, v_col0=P_NV, g_col0=P_NG, v_stride=1,
                  n_kv_heads=NA_HEADS, r=1, dq=HEAD_DIM, scale=HEAD_DIM ** -0.5, ctx_only=True,
                  tq=256, into=na_lat, **common)

    q_norm = jnp.pad(lp["mla_q_norm"], (0, MQA_PAD - MLA_Q_LORA)).reshape(1, MQA_PAD)
    qm = _mla_q(p, q_norm, _layout_w_uq(lp["w_uq"]), *rope_m, seq)
    km, vm = _mla_kv(p, lp["mla_kv_norm"], lp["w_ukv"].astype(BF16), *rope_m, seq)
    o_mla = _attention(
        qm, km, vm, p, MLA_TQ, q_col0=0, k_col0=0, v_col0=0, g_col0=P_MG, v_stride=1,
        n_kv_heads=MLA_HEADS, r=1, dq=MLA_QK_PAD, scale=(MLA_NOPE + MLA_ROPE) ** -0.5,
        **common)

    weights = [lp[n].astype(BF16) for n in ("w_o_ssm", "w_o_gqa", "w_o_na", "w_o_mla")]
    ymix = _merge([o_ssm, o_gqa, o_na, o_mla], weights, p, seq)
    return _out_projection(ymix, lp["w_out"].astype(BF16), xa, lp["norm_post"], gate, n_lat, seq)


def kernel(x, c, ctx, c_ctx, ada_w, ada_b, norm_pre, norm_post, w_in, conv_w, conv_b, a_log, dt_bias,
           d_skip, ssm_norm, w_o_ssm, gqa_q_norm, gqa_k_norm, w_o_gqa, na_rpb, w_o_na, mla_q_norm,
           w_uq, mla_kv_norm, w_ukv, w_o_mla, w_out):
    batch, n_lat, d = x.shape
    n_ctx = ctx.shape[1]
    seq = n_lat + n_ctx
    stacked = dict(ada_w=ada_w, ada_b=ada_b, norm_pre=norm_pre, norm_post=norm_post, w_in=w_in,
                   conv_w=conv_w, conv_b=conv_b, a_log=a_log, dt_bias=dt_bias, d_skip=d_skip,
                   ssm_norm=ssm_norm, w_o_ssm=w_o_ssm, gqa_q_norm=gqa_q_norm, gqa_k_norm=gqa_k_norm,
                   w_o_gqa=w_o_gqa, na_bias=_na_bias_table(na_rpb), w_o_na=w_o_na,
                   mla_q_norm=mla_q_norm, w_uq=w_uq,
                   mla_kv_norm=mla_kv_norm, w_ukv=w_ukv, w_o_mla=w_o_mla, w_out=w_out)
    xa = jnp.concatenate([x, ctx], axis=1).reshape(batch * seq, d)
    cc = jnp.concatenate([c, c_ctx[None, :], jnp.zeros((8 - batch - 1, d), c.dtype)], axis=0)
    rope_g = _rope_tables(n_lat, n_ctx, HEAD_DIM)
    rope_m = _rope_tables(n_lat, n_ctx, MLA_ROPE)
    for layer in range(ada_w.shape[0]):
        lp = {k: v[layer] for k, v in stacked.items()}
        xa = _layer(xa, cc, rope_g, rope_m, lp, batch, n_lat, seq)
    return xa.reshape(batch, seq, d)[:, :n_lat]
```

```python
import functools
import math

import jax
import jax.numpy as jnp
import numpy as np
from jax import lax
from jax.experimental import pallas as pl
from jax.experimental.pallas import tpu as pltpu

F32 = jnp.float32
BF16 = jnp.bfloat16

GRID_W = 64
EPS = 1e-6
ROPE_THETA = 10000.0

SSM_HEADS = 32
SSM_HEAD_DIM = 64
SSM_INNER = SSM_HEADS * SSM_HEAD_DIM
SSM_GROUPS = 4
SSM_STATE = 128
SSM_CONV = 5
SSM_CHUNK = 128
SSM_CONV_DIM = SSM_INNER + 2 * SSM_GROUPS * SSM_STATE

GQA_HEADS = 8
GQA_KV_HEADS = 4
HEAD_DIM = 128
NA_HEADS = 8
NA_WIN_H = 8
NA_WIN_W = 16
NA_QROWS = 4
NA_KROWS = NA_QROWS + NA_WIN_H
NA_HEADS_PER_STEP = 8

MLA_HEADS = 8
MLA_Q_LORA = 768
MLA_KV_LORA = 512
MLA_NOPE = 128
MLA_ROPE = 64
MLA_QK_PAD = 256

N_BRANCH = 4
LANE = 128
VMEM_LIMIT = 56 * 1024 * 1024
NEG_BIG = -1e30
FLASH_TK = 256
GQA_TQ = 512
MLA_TQ = 1024

_SIZES = (SSM_INNER, SSM_CONV_DIM, 2 * SSM_HEADS, 1024, 512, 512, 1024, 1024, 1024, 1024, 1024,
          MLA_Q_LORA, MLA_KV_LORA, MLA_ROPE, 1024, N_BRANCH * 2048)
_OFF = dict(zip(("z", "xbc", "dtr", "gq", "gk", "gv", "gg", "nq", "nk", "nv", "ng",
                 "mqa", "mkva", "mkr", "mg", "mix"), np.cumsum((0,) + _SIZES[:-1]).tolist()))

P_MIX, P_Z, P_XBC = 0, 8192, 10240
P_GQ, P_GG, P_NQ, P_NK, P_NV, P_NG, P_MG, P_MQA = (13312, 14336, 15360, 16384, 17408, 18432,
                                                    19456, 20480)
P_GK, P_GV, P_MKVA, P_MKR = 21504, 22016, 22528, 23040
P_WIDTH = 23552
MQA_PAD = 1024


def _pick(n, candidates):
    for c in candidates:
        if n % c == 0:
            return c
    raise ValueError(f"no tile for {n} among {candidates}")


def _params(sem):
    return pltpu.CompilerParams(dimension_semantics=sem, vmem_limit_bytes=VMEM_LIMIT)


def _sigmoid(v):
    return 0.5 + 0.5 * jnp.tanh(0.5 * v)


def _silu(v):
    return v * _sigmoid(v)


def _rope(v, cos, sin):
    return v * cos + pltpu.roll(v, 64, axis=1) * sin


def _mod_kernel(c_ref, w_ref, b_ref, o_ref):
    h = _silu(c_ref[...]).astype(BF16)
    o_ref[...] = jnp.dot(h, w_ref[...].astype(BF16), preferred_element_type=F32) + b_ref[...]


def _modulation(cc, ada_w, ada_b):
    rows, d = cc.shape
    depth, _, n = ada_w.shape
    tn = _pick(n, (512, 256, 128))
    return pl.pallas_call(
        _mod_kernel,
        grid=(depth, n // tn),
        in_specs=[pl.BlockSpec((rows, d), lambda l, j: (0, 0)),
                  pl.BlockSpec((None, d, tn), lambda l, j: (l, 0, j)),
                  pl.BlockSpec((None, 1, tn), lambda l, j: (l, 0, j))],
        out_specs=pl.BlockSpec((None, rows, tn), lambda l, j: (l, 0, j)),
        out_shape=jax.ShapeDtypeStruct((depth, rows, n), F32),
        compiler_params=_params(("parallel", "parallel")),
        name="adaln_mod",
    )(cc, ada_w, ada_b.reshape(depth, 1, n))


def _is_ctx_rows(i, tm, tiles_per_batch, n_lat):
    row = (i % tiles_per_batch) * tm + lax.broadcasted_iota(jnp.int32, (tm, 1), 0)
    return row >= n_lat


def _inproj_kernel(x_ref, nw_ref, sc_ref, sh_ref, w_ref, ws_ref, o_ref, os_ref, h_ref, *,
                   tm, tiles_per_batch, n_lat):
    i = pl.program_id(0)

    @pl.when(pl.program_id(1) == 0)
    def _():
        x = x_ref[...]
        y = x * lax.rsqrt(jnp.mean(x * x, axis=-1, keepdims=True) + EPS) * nw_ref[...]
        is_ctx = _is_ctx_rows(i, tm, tiles_per_batch, n_lat)
        sc = jnp.where(is_ctx, sc_ref[1:2, :], sc_ref[0:1, :])
        sh = jnp.where(is_ctx, sh_ref[1:2, :], sh_ref[0:1, :])
        h = (y * (1.0 + sc) + sh).astype(BF16)
        h_ref[...] = h
        os_ref[...] = jnp.dot(h, ws_ref[...], preferred_element_type=F32)

    o_ref[...] = jnp.dot(h_ref[...], w_ref[...], preferred_element_type=F32).astype(o_ref.dtype)


def _in_projection(xa, norm_w, scale, shift, w_p, w_side, n_lat, seq):
    t, d = xa.shape
    tm = _pick(seq, (768, 512, 256))
    tn = _pick(P_WIDTH, (1024, 512))
    tpb = seq // tm
    kern = functools.partial(_inproj_kernel, tm=tm, tiles_per_batch=tpb, n_lat=n_lat)
    return pl.pallas_call(
        kern,
        grid=(t // tm, P_WIDTH // tn),
        in_specs=[pl.BlockSpec((tm, d), lambda i, j: (i, 0)),
                  pl.BlockSpec((1, d), lambda i, j: (0, 0)),
                  pl.BlockSpec((None, 2, d), lambda i, j: (i // tpb, 0, 0)),
                  pl.BlockSpec((None, 2, d), lambda i, j: (i // tpb, 0, 0)),
                  pl.BlockSpec((d, tn), lambda i, j: (0, j)),
                  pl.BlockSpec((d, 2 * LANE), lambda i, j: (0, 0))],
        out_specs=[pl.BlockSpec((tm, tn), lambda i, j: (i, j)),
                   pl.BlockSpec((tm, 2 * LANE), lambda i, j: (i, 0))],
        out_shape=[jax.ShapeDtypeStruct((t, P_WIDTH), BF16),
                   jax.ShapeDtypeStruct((t, 2 * LANE), F32)],
        scratch_shapes=[pltpu.VMEM((tm, d), BF16)],
        compiler_params=_params(("parallel", "arbitrary")),
        name="in_proj",
    )(xa, norm_w.reshape(1, d), scale, shift, w_p, w_side)


def _conv_kernel(u_ref, prev_ref, next_ref, w_ref, b_ref, s_ref, o_ref, *,
                 tm, tiles_per_batch, lat_tiles):
    ib = pl.program_id(0) % tiles_per_batch
    first = jnp.logical_or(ib == 0, ib == lat_tiles)
    last = jnp.logical_or(ib == lat_tiles - 1, ib == tiles_per_batch - 1)
    half = SSM_CONV // 2
    u = u_ref[...]
    acc = b_ref[...] + w_ref[half:half + 1, :] * u.astype(F32)
    for idx, k in enumerate(k for k in range(SSM_CONV) if k != half):
        acc = acc + w_ref[k:k + 1, :] * jnp.dot(s_ref[idx], u, preferred_element_type=F32)
    prev = jnp.where(first, 0.0, prev_ref[...].astype(F32))
    nxt = jnp.where(last, 0.0, next_ref[...].astype(F32))
    row = lax.broadcasted_iota(jnp.int32, (8, 1), 0)
    top = jnp.zeros_like(prev)
    bot = jnp.zeros_like(nxt)
    for k in range(half):
        reach = half - k
        top = top + w_ref[k:k + 1, :] * jnp.where(row < reach, pltpu.roll(prev, reach, axis=0), 0.0)
        kk = SSM_CONV - 1 - k
        bot = bot + w_ref[kk:kk + 1, :] * jnp.where(row >= 8 - reach,
                                                     pltpu.roll(nxt, 8 - reach, axis=0), 0.0)
    y = jnp.concatenate([acc[0:8] + top, acc[8:tm - 8], acc[tm - 8:tm] + bot], axis=0)
    o_ref[...] = _silu(y).astype(o_ref.dtype)


def _conv_silu(p, conv_w, conv_b, n_lat, seq):
    t = p.shape[0]
    tm = 256
    tc = 1024
    tpb = seq // tm
    n_row8 = t // 8
    col0 = P_XBC // tc
    kern = functools.partial(_conv_kernel, tm=tm, tiles_per_batch=tpb, lat_tiles=n_lat // tm)
    half = SSM_CONV // 2
    shifts = np.stack([np.eye(tm, k=k - half, dtype=np.float32) for k in range(SSM_CONV) if k != half])
    return pl.pallas_call(
        kern,
        grid=(t // tm, SSM_CONV_DIM // tc),
        in_specs=[pl.BlockSpec((tm, tc), lambda i, j: (i, col0 + j)),
                  pl.BlockSpec((8, tc), lambda i, j: (jnp.maximum(i * (tm // 8) - 1, 0), col0 + j)),
                  pl.BlockSpec((8, tc), lambda i, j: (jnp.minimum((i + 1) * (tm // 8), n_row8 - 1),
                                                      col0 + j)),
                  pl.BlockSpec((8, tc), lambda i, j: (0, j)),
                  pl.BlockSpec((1, tc), lambda i, j: (0, j)),
                  pl.BlockSpec((SSM_CONV - 1, tm, tm), lambda i, j: (0, 0, 0))],
        out_specs=pl.BlockSpec((tm, tc), lambda i, j: (i, j)),
        out_shape=jax.ShapeDtypeStruct((t, SSM_CONV_DIM), BF16),
        compiler_params=_params(("parallel", "parallel")),
        name="ssm_conv",
    )(p, p, p, jnp.pad(conv_w, ((0, 8 - SSM_CONV), (0, 0))), conv_b.reshape(1, -1),
      jnp.asarray(shifts, BF16))


def _split_dot(a_bf16, v):
    v1 = v.astype(BF16)
    r1 = v - v1.astype(F32)
    v2 = r1.astype(BF16)
    v3 = (r1 - v2.astype(F32)).astype(BF16)
    return jnp.dot(jnp.concatenate([a_bf16, a_bf16, a_bf16], axis=1),
                   jnp.concatenate([v1, v2, v3], axis=0), preferred_element_type=F32)


def _ssd_kernel(uf_ref, ub_ref, dtf_ref, dtb_ref, alog_ref, bias_ref, eh_ref, yf_ref, yb_ref,
                stf_ref, stb_ref):
    @pl.when(pl.program_id(1) == 0)
    def _():
        stf_ref[...] = jnp.zeros_like(stf_ref)
        stb_ref[...] = jnp.zeros_like(stb_ref)

    _ssd_chunk(True, uf_ref, dtf_ref, alog_ref[0], bias_ref[0], eh_ref, yf_ref, stf_ref)
    _ssd_chunk(False, ub_ref, dtb_ref, alog_ref[1], bias_ref[1], eh_ref, yb_ref, stb_ref)


def _ssd_chunk(fwd, u_ref, dtr_ref, a_log, dt_bias, eh_ref, y_ref, st_ref):
    cl = SSM_CHUNK
    raw = dtr_ref[...] + dt_bias
    dt = jnp.maximum(raw, 0.0) + jnp.log(1.0 + jnp.exp(-jnp.abs(raw)))
    da = dt * (-jnp.exp(a_log))
    r = lax.broadcasted_iota(jnp.int32, (cl, cl), 0)
    c = lax.broadcasted_iota(jnp.int32, (cl, cl), 1)
    tri = (r >= c) if fwd else (r <= c)
    acs = _split_dot(jnp.where(tri, 1.0, 0.0).astype(BF16), da)
    total = acs[cl - 1:cl, :] if fwd else acs[0:1, :]
    e_acs = jnp.exp(acs)
    w_end = dt * jnp.exp(total - acs)
    acs_t = acs.T
    dt_t = dt.T

    def hi_lo(v):
        hi = v.astype(BF16)
        return jnp.concatenate([hi, (v - hi.astype(F32)).astype(BF16)], axis=1)

    ex = jnp.dot(jnp.concatenate([hi_lo(w_end), hi_lo(e_acs)], axis=0), eh_ref[...],
                 preferred_element_type=F32)
    w_exp = ex[0:cl]
    e_exp = ex[cl:2 * cl]
    dec = e_exp[cl - 1:cl, :] if fwd else e_exp[0:1, :]
    xw =(u_ref[:, 0:SSM_INNER].astype(F32) * w_exp).astype(BF16)
    lane = lax.broadcasted_iota(jnp.int32, (cl, LANE), 1)
    gw = SSM_INNER // SSM_GROUPS
    hpg = SSM_HEADS // SSM_GROUPS
    for g in range(SSM_GROUPS):
        b_g = u_ref[:, SSM_INNER + g * SSM_STATE:SSM_INNER + (g + 1) * SSM_STATE]
        c_off = SSM_INNER + SSM_GROUPS * SSM_STATE
        c_g = u_ref[:, c_off + g * SSM_STATE:c_off + (g + 1) * SSM_STATE]
        cb = lax.dot_general(c_g, b_g, (((1,), (1,)), ((), ())), preferred_element_type=F32)
        st_g = st_ref[:, g * gw:(g + 1) * gw]
        y_off = jnp.dot(c_g, st_g.astype(BF16), preferred_element_type=F32)
        ys = []
        for k in range(hpg // 2):
            ms = []
            for h in (g * hpg + 2 * k, g * hpg + 2 * k + 1):
                seg = acs[:, h:h + 1] - acs_t[h:h + 1, :]
                dec_h = jnp.exp(jnp.where(tri, seg, NEG_BIG))
                ms.append((dec_h * cb * dt_t[h:h + 1, :]).astype(BF16))
            x_p = u_ref[:, g * gw + k * LANE:g * gw + (k + 1) * LANE]
            zero = jnp.zeros_like(x_p)
            rhs = jnp.concatenate([jnp.where(lane < SSM_HEAD_DIM, x_p, zero),
                                   jnp.where(lane >= SSM_HEAD_DIM, x_p, zero)], axis=0)
            ys.append(jnp.dot(jnp.concatenate(ms, axis=1), rhs, preferred_element_type=F32))
        y_g = jnp.concatenate(ys, axis=1) + y_off * e_exp[:, g * gw:(g + 1) * gw]
        y_ref[:, g * gw:(g + 1) * gw] = y_g.astype(y_ref.dtype)
        upd = lax.dot_general(b_g, xw[:, g * gw:(g + 1) * gw], (((0,), (0,)), ((), ())),
                              preferred_element_type=F32)
        st_ref[:, g * gw:(g + 1) * gw] = st_g * dec[:, g * gw:(g + 1) * gw] + upd


def _ssd(u, dtr, a_log, dt_bias, batch, n_lat, seq):
    cl = SSM_CHUNK
    nch, nlat, nctx = seq // cl, n_lat // cl, (seq - n_lat) // cl

    def fchunk(c):
        return jnp.where(c < nctx, nlat + c, c - nctx)

    def bchunk(c):
        return jnp.where(c < nctx, nlat + nctx - 1 - c, nlat - 1 - (c - nctx))

    def pad_heads(v):
        return jnp.pad(v, ((0, 0), (0, LANE - SSM_HEADS))).reshape(2, 1, LANE)

    eh = np.zeros((2 * LANE, SSM_INNER), np.float32)
    for h in range(SSM_HEADS):
        eh[h, h * SSM_HEAD_DIM:(h + 1) * SSM_HEAD_DIM] = 1.0
        eh[LANE + h, h * SSM_HEAD_DIM:(h + 1) * SSM_HEAD_DIM] = 1.0
    return pl.pallas_call(
        _ssd_kernel,
        grid=(batch, nch),
        in_specs=[pl.BlockSpec((cl, SSM_CONV_DIM), lambda b, c: (b * nch + fchunk(c), 0)),
                  pl.BlockSpec((cl, SSM_CONV_DIM), lambda b, c: (b * nch + bchunk(c), 0)),
                  pl.BlockSpec((cl, LANE), lambda b, c: (b * nch + fchunk(c), 0)),
                  pl.BlockSpec((cl, LANE), lambda b, c: (b * nch + bchunk(c), 1)),
                  pl.BlockSpec((2, 1, LANE), lambda b, c: (0, 0, 0)),
                  pl.BlockSpec((2, 1, LANE), lambda b, c: (0, 0, 0)),
                  pl.BlockSpec((2 * LANE, SSM_INNER), lambda b, c: (0, 0))],
        out_specs=[pl.BlockSpec((cl, SSM_INNER), lambda b, c: (b * nch + fchunk(c), 0)),
                   pl.BlockSpec((cl, SSM_INNER), lambda b, c: (b * nch + bchunk(c), 0))],
        out_shape=[jax.ShapeDtypeStruct((u.shape[0], SSM_INNER), BF16)] * 2,
        scratch_shapes=[pltpu.VMEM((SSM_STATE, SSM_INNER), F32)] * 2,
        compiler_params=_params(("parallel", "arbitrary")),
        name="ssd_scan",
    )(u, u, dtr, dtr, pad_heads(a_log), pad_heads(dt_bias), jnp.asarray(eh, BF16))


def _ssm_out_kernel(yf_ref, yb_ref, x_ref, z_ref, skip_ref, nw_ref, o_ref):
    z = z_ref[...].astype(F32)
    g = (yf_ref[...].astype(F32) + yb_ref[...].astype(F32)
         + skip_ref[...] * x_ref[...].astype(F32)) * _silu(z)
    gw = SSM_INNER // SSM_GROUPS
    for k in range(SSM_GROUPS):
        gk = g[:, k * gw:(k + 1) * gw]
        gk = gk * lax.rsqrt(jnp.mean(gk * gk, axis=-1, keepdims=True) + EPS)
        o_ref[:, k * gw:(k + 1) * gw] = (gk * nw_ref[:, k * gw:(k + 1) * gw]).astype(o_ref.dtype)


def _ssm_out(yf, yb, u, p, d_skip, ssm_norm, seq):
    t = u.shape[0]
    tm = _pick(seq, (384, 256, 128))
    w = SSM_INNER
    return pl.pallas_call(
        _ssm_out_kernel,
        grid=(t // tm,),
        in_specs=[pl.BlockSpec((tm, w), lambda i: (i, 0)),
                  pl.BlockSpec((tm, w), lambda i: (i, 0)),
                  pl.BlockSpec((tm, w), lambda i: (i, 0)),
                  pl.BlockSpec((tm, w), lambda i: (i, P_Z // w)),
                  pl.BlockSpec((1, w), lambda i: (0, 0)),
                  pl.BlockSpec((1, w), lambda i: (0, 0))],
        out_specs=pl.BlockSpec((tm, w), lambda i: (i, 0)),
        out_shape=jax.ShapeDtypeStruct((t, w), BF16),
        compiler_params=_params(("parallel",)),
        name="ssm_gated_norm",
    )(yf, yb, u, p, jnp.repeat(d_skip, SSM_HEAD_DIM).reshape(1, w), ssm_norm.reshape(1, w))


def _head_norm_rope_kernel(x_ref, nw_ref, cos_ref, sin_ref, o_ref, *, n_heads):
    for h in range(n_heads):
        sl = slice(h * LANE, (h + 1) * LANE)
        x = x_ref[:, sl].astype(F32)
        y = x * lax.rsqrt(jnp.mean(x * x, axis=-1, keepdims=True) + EPS) * nw_ref[...]
        o_ref[:, sl] = _rope(y, cos_ref[...], sin_ref[...]).astype(o_ref.dtype)


def _head_norm_rope(p, col0, n_heads, norm_w, cos, sin, seq):
    t = p.shape[0]
    tm = _pick(seq, (768, 512, 256))
    tpb = seq // tm
    w = n_heads * LANE
    return pl.pallas_call(
        functools.partial(_head_norm_rope_kernel, n_heads=n_heads),
        grid=(t // tm,),
        in_specs=[pl.BlockSpec((tm, w), lambda i: (i, col0 // w)),
                  pl.BlockSpec((1, LANE), lambda i: (0, 0)),
                  pl.BlockSpec((tm, LANE), lambda i: (i % tpb, 0)),
                  pl.BlockSpec((tm, LANE), lambda i: (i % tpb, 0))],
        out_specs=pl.BlockSpec((tm, w), lambda i: (i, 0)),
        out_shape=jax.ShapeDtypeStruct((t, w), BF16),
        compiler_params=_params(("parallel",)),
        name="head_norm_rope",
    )(p, norm_w.reshape(1, LANE), cos, sin)


def _mla_q_kernel(x_ref, nw_ref, w_ref, cos_ref, sin_ref, o_ref):
    x = x_ref[...].astype(F32)
    ms = jnp.sum(x * x, axis=-1, keepdims=True) * (1.0 / MLA_Q_LORA)
    h = (x * lax.rsqrt(ms + EPS) * nw_ref[...]).astype(BF16)
    for hd in range(MLA_HEADS):
        c0 = hd * MLA_QK_PAD
        res = jnp.dot(h, w_ref[:, c0:c0 + MLA_QK_PAD], preferred_element_type=F32)
        o_ref[:, c0:c0 + LANE] = res[:, 0:LANE].astype(o_ref.dtype)
        o_ref[:, c0 + LANE:c0 + MLA_QK_PAD] = _rope(res[:, LANE:], cos_ref[...],
                                                    sin_ref[...]).astype(o_ref.dtype)


def _mla_q(p, q_norm, w_uq, cos, sin, seq):
    t = p.shape[0]
    tm = _pick(seq, (768, 512, 256))
    tpb = seq // tm
    kdim = MQA_PAD
    wo = MLA_HEADS * MLA_QK_PAD
    return pl.pallas_call(
        _mla_q_kernel,
        grid=(t // tm,),
        in_specs=[pl.BlockSpec((tm, kdim), lambda i: (i, P_MQA // kdim)),
                  pl.BlockSpec((1, kdim), lambda i: (0, 0)),
                  pl.BlockSpec((kdim, wo), lambda i: (0, 0)),
                  pl.BlockSpec((tm, LANE), lambda i: (i % tpb, 0)),
                  pl.BlockSpec((tm, LANE), lambda i: (i % tpb, 0))],
        out_specs=pl.BlockSpec((tm, wo), lambda i: (i, 0)),
        out_shape=jax.ShapeDtypeStruct((t, wo), BF16),
        compiler_params=_params(("parallel",)),
        name="mla_q_up",
    )(p, q_norm, w_uq, cos, sin)


def _mla_kv_kernel(x_ref, kr_ref, nw_ref, w_ref, cos_ref, sin_ref, k_ref, v_ref):
    x = x_ref[...].astype(F32)
    h = (x * lax.rsqrt(jnp.mean(x * x, axis=-1, keepdims=True) + EPS) * nw_ref[...]).astype(BF16)
    k_rope = _rope(kr_ref[...].astype(F32), cos_ref[...], sin_ref[...]).astype(k_ref.dtype)
    for hd in range(MLA_HEADS):
        c0 = hd * MLA_QK_PAD
        res = jnp.dot(h, w_ref[:, c0:c0 + 2 * LANE], preferred_element_type=F32)
        k_ref[:, c0:c0 + LANE] = res[:, 0:LANE].astype(k_ref.dtype)
        k_ref[:, c0 + LANE:c0 + MLA_QK_PAD] = k_rope
        v_ref[:, hd * LANE:(hd + 1) * LANE] = res[:, LANE:].astype(v_ref.dtype)


def _mla_kv(p, kv_norm, w_ukv, cos, sin, seq):
    t = p.shape[0]
    tm = _pick(seq, (768, 512, 256))
    tpb = seq // tm
    kdim = MLA_KV_LORA
    wk = MLA_HEADS * MLA_QK_PAD
    return pl.pallas_call(
        _mla_kv_kernel,
        grid=(t // tm,),
        in_specs=[pl.BlockSpec((tm, kdim), lambda i: (i, P_MKVA // kdim)),
                  pl.BlockSpec((tm, LANE), lambda i: (i, P_MKR // LANE)),
                  pl.BlockSpec((1, kdim), lambda i: (0, 0)),
                  pl.BlockSpec((kdim, wk), lambda i: (0, 0)),
                  pl.BlockSpec((tm, LANE), lambda i: (i % tpb, 0)),
                  pl.BlockSpec((tm, LANE), lambda i: (i % tpb, 0))],
        out_specs=[pl.BlockSpec((tm, wk), lambda i: (i, 0)),
                   pl.BlockSpec((tm, MLA_HEADS * LANE), lambda i: (i, 0))],
        out_shape=[jax.ShapeDtypeStruct((t, wk), BF16),
                   jax.ShapeDtypeStruct((t, MLA_HEADS * LANE), BF16)],
        compiler_params=_params(("parallel",)),
        name="mla_kv_up",
    )(p, p, kv_norm.reshape(1, kdim), w_ukv, cos, sin)


def _flash_kernel(q_ref, k_ref, v_ref, g_ref, *rest, r, tq, dq, tk, n_keys, scale):
    o_ref = rest[-1]
    q = jnp.concatenate([q_ref[:, j * dq:(j + 1) * dq] for j in range(r)], axis=0)
    q = (q.astype(F32) * (scale * math.log2(math.e))).astype(BF16)
    m_rows = r * tq
    m = jnp.full((m_rows, 1), NEG_BIG, F32)
    l = jnp.zeros((m_rows, LANE), F32)
    acc = jnp.zeros((m_rows, LANE), F32)
    for c in range(n_keys // tk):
        k_c = k_ref[c * tk:(c + 1) * tk, :]
        v_c = v_ref[c * tk:(c + 1) * tk, :]
        s = lax.dot_general(q, k_c, (((1,), (1,)), ((), ())), preferred_element_type=F32)
        m_new = jnp.maximum(m, jnp.max(s, axis=-1, keepdims=True))
        alpha = jnp.exp2(m - m_new)
        pr = jnp.exp2(s - m_new)
        part = pr[:, 0:LANE]
        for j in range(1, tk // LANE):
            part = part + pr[:, j * LANE:(j + 1) * LANE]
        l = alpha * l + part
        acc = alpha * acc + jnp.dot(pr.astype(BF16), v_c, preferred_element_type=F32)
        m = m_new
    o = acc / jnp.sum(l, axis=-1, keepdims=True)
    for j in range(r):
        gate = _silu(g_ref[:, j * LANE:(j + 1) * LANE].astype(F32))
        o_ref[:, j * LANE:(j + 1) * LANE] = (o[j * tq:(j + 1) * tq] * gate).astype(o_ref.dtype)


def _flash(q, k, v, gates, *, q_col0, k_col0, v_col0, g_col0, v_stride, n_kv_heads, r, dq, scale,
           batch, n_lat, seq, ctx_only, tq, into=None):
    n_ctx = seq - n_lat
    tk = FLASH_TK
    if ctx_only:
        tq = n_ctx
        q_blk0, n_q, kv_rows, kv_blk0 = n_lat // tq, 1, n_ctx, n_lat // n_ctx
    else:
        q_blk0, n_q, kv_rows, kv_blk0 = 0, n_lat // tq, seq, 0
    assert kv_rows % tk == 0
    kern = functools.partial(_flash_kernel, r=r, tq=tq, dq=dq, tk=tk, n_keys=kv_rows, scale=scale)
    w_out = n_kv_heads * r * LANE
    args = [a.reshape(batch, seq, a.shape[-1]) for a in (q, k, v, gates)]
    in_specs = [pl.BlockSpec((None, tq, r * dq), lambda b, g, i: (b, q_blk0 + i, q_col0 // (r * dq) + g)),
                pl.BlockSpec((None, kv_rows, dq), lambda b, g, i: (b, kv_blk0, k_col0 // dq + g)),
                pl.BlockSpec((None, kv_rows, LANE),
                             lambda b, g, i: (b, kv_blk0, v_col0 // LANE + v_stride * g)),
                pl.BlockSpec((None, tq, r * LANE),
                             lambda b, g, i: (b, q_blk0 + i, g_col0 // (r * LANE) + g))]
    aliases = {}
    if into is not None:
        args.append(into.reshape(batch, seq, w_out))
        in_specs.append(pl.BlockSpec(memory_space=pl.ANY))
        aliases = {len(args) - 1: 0}
    out = pl.pallas_call(
        kern,
        grid=(batch, n_kv_heads, n_q),
        in_specs=in_specs,
        out_specs=pl.BlockSpec((None, tq, r * LANE), lambda b, g, i: (b, q_blk0 + i, g)),
        out_shape=jax.ShapeDtypeStruct((batch, seq, w_out), BF16),
        input_output_aliases=aliases,
        compiler_params=_params(("parallel", "parallel", "parallel")),
        name="flash_ctx" if ctx_only else "flash_latent",
    )(*args)
    return out.reshape(batch * seq, w_out)


def _attention(q, k, v, gates, tq, **kw):
    lat = _flash(q, k, v, gates, ctx_only=False, tq=tq, **kw)
    return _flash(q, k, v, gates, ctx_only=True, tq=tq, into=lat, **kw)


def _natten_kernel(q_ref, k0_ref, k1_ref, k2_ref, kc_ref, v0_ref, v1_ref, v2_ref, vc_ref, g_ref,
                   bias_ref, o_ref, *, scale, n_heads):
    n_loc = bias_ref.shape[-1]
    for h in range(n_heads):
        sl = slice(h * LANE, (h + 1) * LANE)
        q = (q_ref[:, sl].astype(F32) * (scale * math.log2(math.e))).astype(BF16)
        k = jnp.concatenate([k0_ref[:, sl], k1_ref[:, sl], k2_ref[:, sl], kc_ref[:, sl]], axis=0)
        v = jnp.concatenate([v0_ref[:, sl], v1_ref[:, sl], v2_ref[:, sl], vc_ref[:, sl]], axis=0)
        s = lax.dot_general(q, k, (((1,), (1,)), ((), ())), preferred_element_type=F32)
        s = jnp.concatenate([s[:, :n_loc] + bias_ref[h], s[:, n_loc:]], axis=1)
        pr = jnp.exp2(s - jnp.max(s, axis=-1, keepdims=True))
        l = jnp.sum(pr, axis=-1, keepdims=True)
        o = jnp.dot(pr.astype(BF16), v, preferred_element_type=F32) / l
        o_ref[:, sl] = (o * _silu(g_ref[:, sl].astype(F32))).astype(o_ref.dtype)


def _na_bias_table(rpb):
    qx = np.arange(GRID_W)
    c0 = np.clip(qx - NA_WIN_W // 2, 0, GRID_W - NA_WIN_W)
    col_ok = (qx[None, :] >= c0[:, None]) & (qx[None, :] < c0[:, None] + NA_WIN_W)
    dx = qx[None, :] - qx[:, None] + NA_WIN_W - 1
    pick = np.zeros((2 * NA_WIN_W - 1, GRID_W * GRID_W), np.float32)
    qi, ki = np.nonzero(col_ok)
    pick[dx[qi, ki], qi * GRID_W + ki] = 1.0
    by_dx = jnp.einsum("...d,dn->...n", rpb, pick, precision=lax.Precision.HIGHEST)
    qy, ky = np.arange(NA_QROWS), np.arange(NA_KROWS)
    dys, oks = [], []
    for q_off, first_key in ((0, None), (NA_WIN_H // 2, "q"), (NA_WIN_H, NA_KROWS - NA_WIN_H)):
        r0 = qy if first_key == "q" else np.full_like(qy, 0 if first_key is None else first_key)
        oks.append((ky[None, :] >= r0[:, None]) & (ky[None, :] < r0[:, None] + NA_WIN_H))
        dys.append(np.clip(ky[None, :] - (qy[:, None] + q_off) + NA_WIN_H - 1, 0, 2 * NA_WIN_H - 2))
    rows = jnp.take(by_dx, np.stack(dys).reshape(-1), axis=-2)
    lead = rows.shape[:-2]
    nl = len(lead)
    rows = rows.reshape(lead + (3, NA_QROWS, NA_KROWS, GRID_W, GRID_W))
    rows = rows.transpose(tuple(range(nl)) + (nl, nl + 1, nl + 3, nl + 2, nl + 4))
    ok = np.stack(oks)[:, :, None, :, None] & col_ok[None, None, :, None, :]
    tab = jnp.where(jnp.asarray(ok), rows * math.log2(math.e), NEG_BIG)
    return tab.reshape(lead + (3, NA_QROWS * GRID_W, NA_KROWS * GRID_W))


def _natten(p, bias, batch, n_lat, seq):
    n_ctx = seq - n_lat
    tq = NA_QROWS * GRID_W
    n_blk = n_lat // tq
    assert n_blk >= 3 and n_lat % tq == 0 and n_ctx % tq == 0
    rb = seq // tq
    cb = seq // n_ctx
    nk = NA_KROWS * GRID_W
    hps = NA_HEADS_PER_STEP
    hw = hps * LANE

    def kblk(i):
        return jnp.clip(i - 1, 0, n_blk - 3)

    def kspec(col0, j):
        return pl.BlockSpec((tq, hw), lambda b, h, i: (b * rb + kblk(i) + j, col0 // hw + h))

    def cspec(col0):
        return pl.BlockSpec((n_ctx, hw), lambda b, h, i: (b * cb + n_lat // n_ctx, col0 // hw + h))

    def qspec(col0):
        return pl.BlockSpec((tq, hw), lambda b, h, i: (b * rb + i, col0 // hw + h))

    def btype(i):
        return jnp.where(i == 0, 0, jnp.where(i == n_blk - 1, 2, 1))

    return pl.pallas_call(
        functools.partial(_natten_kernel, scale=HEAD_DIM ** -0.5, n_heads=hps),
        grid=(batch, NA_HEADS // hps, n_blk),
        in_specs=[qspec(P_NQ), kspec(P_NK, 0), kspec(P_NK, 1), kspec(P_NK, 2), cspec(P_NK),
                  kspec(P_NV, 0), kspec(P_NV, 1), kspec(P_NV, 2), cspec(P_NV), qspec(P_NG),
                  pl.BlockSpec((hps, None, tq, nk), lambda b, h, i: (h, btype(i), 0, 0))],
        out_specs=pl.BlockSpec((tq, hw), lambda b, h, i: (b * rb + i, h)),
        out_shape=jax.ShapeDtypeStruct((p.shape[0], NA_HEADS * LANE), BF16),
        compiler_params=_params(("parallel", "parallel", "parallel")),
        name="natten_latent",
    )(p, p, p, p, p, p, p, p, p, p, bias)


def _merge_kernel(o0_ref, o1_ref, o2_ref, o3_ref, w0_ref, w1_ref, w2_ref, w3_ref,
                  m0_ref, m1_ref, m2_ref, m3_ref, y_ref):
    acc = None
    for o_ref, w_ref, m_ref in ((o0_ref, w0_ref, m0_ref), (o1_ref, w1_ref, m1_ref),
                                (o2_ref, w2_ref, m2_ref), (o3_ref, w3_ref, m3_ref)):
        gate = _sigmoid(m_ref[...].astype(F32))
        term = gate * jnp.dot(o_ref[...], w_ref[...], preferred_element_type=F32)
        acc = term if acc is None else acc + term
    y_ref[...] = acc.astype(y_ref.dtype)


def _merge(outs, weights, p, seq):
    t = p.shape[0]
    d = weights[0].shape[1]
    tm = _pick(seq, (768, 512, 256))
    tn = 512
    o_specs = [pl.BlockSpec((tm, o.shape[1]), lambda i, j: (i, 0)) for o in outs]
    w_specs = [pl.BlockSpec((w.shape[0], tn), lambda i, j: (0, j)) for w in weights]
    m_specs = [pl.BlockSpec((tm, tn), lambda i, j, b=b: (i, (P_MIX + b * d) // tn + j))
               for b in range(N_BRANCH)]
    return pl.pallas_call(
        _merge_kernel,
        grid=(t // tm, d // tn),
        in_specs=o_specs + w_specs + m_specs,
        out_specs=pl.BlockSpec((tm, tn), lambda i, j: (i, j)),
        out_shape=jax.ShapeDtypeStruct((t, d), BF16),
        compiler_params=_params(("parallel", "parallel")),
        name="branch_merge",
    )(*outs, *weights, p, p, p, p)


def _out_kernel(y_ref, w_ref, x_ref, nw_ref, gate_ref, o_ref, *, tm, tiles_per_batch, n_lat):
    z = jnp.dot(y_ref[...], w_ref[...], preferred_element_type=F32)
    zn = z * lax.rsqrt(jnp.mean(z * z, axis=-1, keepdims=True) + EPS) * nw_ref[...]
    if tiles_per_batch is None:
        gate = gate_ref[0:1, :]
    else:
        is_ctx = _is_ctx_rows(pl.program_id(0), tm, tiles_per_batch, n_lat)
        gate = jnp.where(is_ctx, gate_ref[1:2, :], gate_ref[0:1, :])
    o_ref[...] = x_ref[...] + gate * zn


def _out_projection_latent(y, w_out, xa, norm_w, gate, batch, n_lat, seq):
    d = xa.shape[1]
    tm = _pick(n_lat, (512, 256, 128))
    kern = functools.partial(_out_kernel, tm=tm, tiles_per_batch=None, n_lat=n_lat)
    return pl.pallas_call(
        kern,
        grid=(batch, n_lat // tm),
        in_specs=[pl.BlockSpec((None, tm, d), lambda b, i: (b, i, 0)),
                  pl.BlockSpec((d, d), lambda b, i: (0, 0)),
                  pl.BlockSpec((None, tm, d), lambda b, i: (b, i, 0)),
                  pl.BlockSpec((1, d), lambda b, i: (0, 0)),
                  pl.BlockSpec((None, 2, d), lambda b, i: (b, 0, 0))],
        out_specs=pl.BlockSpec((None, tm, d), lambda b, i: (b, i, 0)),
        out_shape=jax.ShapeDtypeStruct((batch, n_lat, d), F32),
        compiler_params=_params(("parallel", "parallel")),
        name="out_proj_last",
    )(y.reshape(batch, seq, d), w_out, xa.reshape(batch, seq, d), norm_w.reshape(1, d), gate)


def _out_projection(y, w_out, xa, norm_w, gate, n_lat, seq):
    t, d = xa.shape
    tm = _pick(seq, (768, 384, 256, 128))
    tpb = seq // tm
    kern = functools.partial(_out_kernel, tm=tm, tiles_per_batch=tpb, n_lat=n_lat)
    return pl.pallas_call(
        kern,
        grid=(t // tm,),
        in_specs=[pl.BlockSpec((tm, d), lambda i: (i, 0)),
                  pl.BlockSpec((d, d), lambda i: (0, 0)),
                  pl.BlockSpec((tm, d), lambda i: (i, 0)),
                  pl.BlockSpec((1, d), lambda i: (0, 0)),
                  pl.BlockSpec((None, 2, d), lambda i: (i // tpb, 0, 0))],
        out_specs=pl.BlockSpec((tm, d), lambda i: (i, 0)),
        out_shape=jax.ShapeDtypeStruct((t, d), F32),
        compiler_params=_params(("parallel",)),
        name="out_proj",
    )(y, w_out, xa, norm_w.reshape(1, d), gate)


def _pairs_apart(w, n_heads, dim):
    lead = w.shape[:-1]
    return w.reshape(lead + (n_heads, dim // 2, 2)).swapaxes(-1, -2).reshape(lead + (n_heads * dim,))


def _rope_tile(w):
    lead = w.shape[:-1]
    pr = w.reshape(lead + (MLA_ROPE // 2, 2))
    zero = jnp.zeros(lead + (MLA_ROPE // 2,), w.dtype)
    return jnp.concatenate([pr[..., 0], zero, pr[..., 1], zero], axis=-1)


def _layout_w_in(w):
    o = _OFF
    k = w.shape[0]

    def seg(name, width):
        return w[:, o[name]:o[name] + width]

    cols = [seg("mix", 8192), seg("z", 2048), seg("xbc", SSM_CONV_DIM),
            _pairs_apart(seg("gq", 1024), GQA_HEADS, HEAD_DIM), seg("gg", 1024),
            seg("nq", 1024), seg("nk", 1024), seg("nv", 1024), seg("ng", 1024), seg("mg", 1024),
            seg("mqa", MLA_Q_LORA), jnp.zeros((k, MQA_PAD - MLA_Q_LORA), w.dtype),
            _pairs_apart(seg("gk", 512), GQA_KV_HEADS, HEAD_DIM), seg("gv", 512),
            seg("mkva", MLA_KV_LORA), _rope_tile(seg("mkr", MLA_ROPE))]
    main = jnp.concatenate(cols, axis=1)
    main = jnp.pad(main, ((0, 0), (0, P_WIDTH - main.shape[1]))).astype(BF16)
    dtr = seg("dtr", 2 * SSM_HEADS)
    zero = jnp.zeros((k, LANE - SSM_HEADS), w.dtype)
    side = jnp.concatenate([dtr[:, :SSM_HEADS], zero, dtr[:, SSM_HEADS:], zero], axis=1).astype(BF16)
    return main, side


def _layout_w_uq(w_uq):
    k = w_uq.shape[0]
    w = w_uq.reshape(k, MLA_HEADS, MLA_NOPE + MLA_ROPE)
    w = jnp.concatenate([w[..., :MLA_NOPE], _rope_tile(w[..., MLA_NOPE:])], axis=-1)
    w = w.reshape(k, MLA_HEADS * MLA_QK_PAD)
    return jnp.pad(w, ((0, MQA_PAD - k), (0, 0))).astype(BF16)


def _rope_tables(n_lat, n_ctx, dim):
    t = np.arange(n_lat)
    quarter = dim // 4
    freqs = ROPE_THETA ** (-jnp.arange(quarter, dtype=F32) / quarter)
    row = jnp.asarray(t // GRID_W, F32)
    col = jnp.asarray(t % GRID_W, F32)
    ang = jnp.concatenate([row[:, None] * freqs, col[:, None] * freqs], axis=-1)
    cos, sin = jnp.cos(ang), jnp.sin(ang)
    pad = 64 - dim // 2
    one, zero = jnp.ones((n_lat, pad), F32), jnp.zeros((n_lat, pad), F32)
    cos_t = jnp.concatenate([cos, one, cos, one], axis=-1)
    sin_t = jnp.concatenate([-sin, zero, sin, zero], axis=-1)
    cos_t = jnp.concatenate([cos_t, jnp.ones((n_ctx, LANE), F32)], axis=0)
    sin_t = jnp.concatenate([sin_t, jnp.zeros((n_ctx, LANE), F32)], axis=0)
    return cos_t, sin_t


def _layer(xa, mod, rope_g, rope_m, lp, batch, n_lat, seq, last):
    d = xa.shape[1]

    def per_row(v):
        return jnp.stack([v[:batch], jnp.broadcast_to(v[batch:batch + 1], (batch, d))], axis=1)

    shift, scale, gate = (per_row(mod[:, k * d:(k + 1) * d]) for k in range(3))
    w_main, w_side = _layout_w_in(lp["w_in"])
    p, dtr = _in_projection(xa, lp["norm_pre"], scale, shift, w_main, w_side, n_lat, seq)

    u = _conv_silu(p, lp["conv_w"], lp["conv_b"], n_lat, seq)
    yf, yb = _ssd(u, dtr, lp["a_log"], lp["dt_bias"], batch, n_lat, seq)
    o_ssm = _ssm_out(yf, yb, u, p, lp["d_skip"], lp["ssm_norm"], seq)

    common = dict(batch=batch, n_lat=n_lat, seq=seq)
    qg = _head_norm_rope(p, P_GQ, GQA_HEADS, _pairs_apart(lp["gqa_q_norm"], 1, HEAD_DIM), *rope_g, seq)
    kg = _head_norm_rope(p, P_GK, GQA_KV_HEADS, _pairs_apart(lp["gqa_k_norm"], 1, HEAD_DIM), *rope_g, seq)
    o_gqa = _attention(
        qg, kg, p, p, GQA_TQ, q_col0=0, k_col0=0, v_col0=P_GV, g_col0=P_GG, v_stride=1,
        n_kv_heads=GQA_KV_HEADS, r=GQA_HEADS // GQA_KV_HEADS, dq=HEAD_DIM, scale=HEAD_DIM ** -0.5,
        **common)

    na_lat = _natten(p, lp["na_bias"], **common)
    o_na = _flash(p, p, p, p, q_col0=P_NQ, k_col0=P_NK, v_col0=P_NV, g_col0=P_NG, v_stride=1,
                  n_kv_heads=NA_HEADS, r=1, dq=HEAD_DIM, scale=HEAD_DIM ** -0.5, ctx_only=True,
                  tq=256, into=na_lat, **common)

    q_norm = jnp.pad(lp["mla_q_norm"], (0, MQA_PAD - MLA_Q_LORA)).reshape(1, MQA_PAD)
    qm = _mla_q(p, q_norm, _layout_w_uq(lp["w_uq"]), *rope_m, seq)
    km, vm = _mla_kv(p, lp["mla_kv_norm"], lp["w_ukv"].astype(BF16), *rope_m, seq)
    o_mla = _attention(
        qm, km, vm, p, MLA_TQ, q_col0=0, k_col0=0, v_col0=0, g_col0=P_MG, v_stride=1,
        n_kv_heads=MLA_HEADS, r=1, dq=MLA_QK_PAD, scale=(MLA_NOPE + MLA_ROPE) ** -0.5,
        **common)

    weights = [lp[n].astype(BF16) for n in ("w_o_ssm", "w_o_gqa", "w_o_na", "w_o_mla")]
    ymix = _merge([o_ssm, o_gqa, o_na, o_mla], weights, p, seq)
    w_out = lp["w_out"].astype(BF16)
    if last:
        return _out_projection_latent(ymix, w_out, xa, lp["norm_post"], gate, batch, n_lat, seq)
    return _out_projection(ymix, w_out, xa, lp["norm_post"], gate, n_lat, seq)


def kernel(x, c, ctx, c_ctx, ada_w, ada_b, norm_pre, norm_post, w_in, conv_w, conv_b, a_log, dt_bias,
           d_skip, ssm_norm, w_o_ssm, gqa_q_norm, gqa_k_norm, w_o_gqa, na_rpb, w_o_na, mla_q_norm,
           w_uq, mla_kv_norm, w_ukv, w_o_mla, w_out):
    batch, n_lat, d = x.shape
    n_ctx = ctx.shape[1]
    seq = n_lat + n_ctx
    stacked = dict(norm_pre=norm_pre, norm_post=norm_post, w_in=w_in,
                   conv_w=conv_w, conv_b=conv_b, a_log=a_log, dt_bias=dt_bias, d_skip=d_skip,
                   ssm_norm=ssm_norm, w_o_ssm=w_o_ssm, gqa_q_norm=gqa_q_norm, gqa_k_norm=gqa_k_norm,
                   w_o_gqa=w_o_gqa, na_bias=_na_bias_table(na_rpb), w_o_na=w_o_na,
                   mla_q_norm=mla_q_norm, w_uq=w_uq,
                   mla_kv_norm=mla_kv_norm, w_ukv=w_ukv, w_o_mla=w_o_mla, w_out=w_out)
    xa = jnp.concatenate([x, ctx], axis=1).reshape(batch * seq, d)
    cc = jnp.concatenate([c, c_ctx[None, :], jnp.zeros((8 - batch - 1, d), c.dtype)], axis=0)
    rope_g = _rope_tables(n_lat, n_ctx, HEAD_DIM)
    rope_m = _rope_tables(n_lat, n_ctx, MLA_ROPE)
    depth = ada_w.shape[0]
    mods = _modulation(cc, ada_w, ada_b)
    for layer in range(depth):
        lp = {k: v[layer] for k, v in stacked.items()}
        xa = _layer(xa, mods[layer], rope_g, rope_m, lp, batch, n_lat, seq, last=layer == depth - 1)
    return xa
```

```python
import functools
import math

import jax
import jax.numpy as jnp
import numpy as np
from jax import lax
from jax.experimental import pallas as pl
from jax.experimental.pallas import tpu as pltpu

F32 = jnp.float32
BF16 = jnp.bfloat16

GRID_W = 64
EPS = 1e-6
ROPE_THETA = 10000.0

SSM_HEADS = 32
SSM_HEAD_DIM = 64
SSM_INNER = SSM_HEADS * SSM_HEAD_DIM
SSM_GROUPS = 4
SSM_STATE = 128
SSM_CONV = 5
SSM_CHUNK = 128
SSM_CONV_DIM = SSM_INNER + 2 * SSM_GROUPS * SSM_STATE

GQA_HEADS = 8
GQA_KV_HEADS = 4
HEAD_DIM = 128
NA_HEADS = 8
NA_WIN_H = 8
NA_WIN_W = 16
NA_QROWS = 4
NA_KROWS = NA_QROWS + NA_WIN_H
NA_HEADS_PER_STEP = 8

MLA_HEADS = 8
MLA_Q_LORA = 768
MLA_KV_LORA = 512
MLA_NOPE = 128
MLA_ROPE = 64
MLA_QK_PAD = 256

N_BRANCH = 4
LANE = 128
VMEM_LIMIT = 56 * 1024 * 1024
NEG_BIG = -1e30
FLASH_TK = 256
GQA_TQ = 512
MLA_TQ = 1024

_SIZES = (SSM_INNER, SSM_CONV_DIM, 2 * SSM_HEADS, 1024, 512, 512, 1024, 1024, 1024, 1024, 1024,
          MLA_Q_LORA, MLA_KV_LORA, MLA_ROPE, 1024, N_BRANCH * 2048)
_OFF = dict(zip(("z", "xbc", "dtr", "gq", "gk", "gv", "gg", "nq", "nk", "nv", "ng",
                 "mqa", "mkva", "mkr", "mg", "mix"), np.cumsum((0,) + _SIZES[:-1]).tolist()))

P_MIX, P_Z, P_XBC = 0, 8192, 10240
P_GQ, P_GG, P_NQ, P_NK, P_NV, P_NG, P_MG, P_MQA = (13312, 14336, 15360, 16384, 17408, 18432,
                                                    19456, 20480)
P_GK, P_GV, P_MKVA = 21504, 22016, 22528
P_WIDTH = 23040
MQA_PAD = 1024
P_MKR = P_MQA + MLA_Q_LORA


def _pick(n, candidates):
    for c in candidates:
        if n % c == 0:
            return c
    raise ValueError(f"no tile for {n} among {candidates}")


def _params(sem):
    return pltpu.CompilerParams(dimension_semantics=sem, vmem_limit_bytes=VMEM_LIMIT)


def _sigmoid(v):
    return 0.5 + 0.5 * jnp.tanh(0.5 * v)


def _silu(v):
    return v * _sigmoid(v)


def _rope(v, cos, sin):
    return v * cos + pltpu.roll(v, 64, axis=1) * sin


def _mod_kernel(c_ref, w_ref, b_ref, o_ref):
    h = _silu(c_ref[...]).astype(BF16)
    o_ref[...] = jnp.dot(h, w_ref[...].astype(BF16), preferred_element_type=F32) + b_ref[...]


def _modulation(cc, ada_w, ada_b):
    rows, d = cc.shape
    depth, _, n = ada_w.shape
    tn = _pick(n, (512, 256, 128))
    return pl.pallas_call(
        _mod_kernel,
        grid=(depth, n // tn),
        in_specs=[pl.BlockSpec((rows, d), lambda l, j: (0, 0)),
                  pl.BlockSpec((None, d, tn), lambda l, j: (l, 0, j)),
                  pl.BlockSpec((None, 1, tn), lambda l, j: (l, 0, j))],
        out_specs=pl.BlockSpec((None, rows, tn), lambda l, j: (l, 0, j)),
        out_shape=jax.ShapeDtypeStruct((depth, rows, n), F32),
        compiler_params=_params(("parallel", "parallel")),
        name="adaln_mod",
    )(cc, ada_w, ada_b.reshape(depth, 1, n))


def _is_ctx_rows(i, tm, tiles_per_batch, n_lat):
    row = (i % tiles_per_batch) * tm + lax.broadcasted_iota(jnp.int32, (tm, 1), 0)
    return row >= n_lat


def _inproj_kernel(x_ref, nw_ref, sc_ref, sh_ref, w_ref, ws_ref, o_ref, os_ref, h_ref, *,
                   tm, tiles_per_batch, n_lat):
    i = pl.program_id(0)

    @pl.when(pl.program_id(1) == 0)
    def _():
        x = x_ref[...]
        y = x * lax.rsqrt(jnp.mean(x * x, axis=-1, keepdims=True) + EPS) * nw_ref[...]
        is_ctx = _is_ctx_rows(i, tm, tiles_per_batch, n_lat)
        sc = jnp.where(is_ctx, sc_ref[1:2, :], sc_ref[0:1, :])
        sh = jnp.where(is_ctx, sh_ref[1:2, :], sh_ref[0:1, :])
        h = (y * (1.0 + sc) + sh).astype(BF16)
        h_ref[...] = h
        os_ref[...] = jnp.dot(h, ws_ref[...], preferred_element_type=F32)

    o_ref[...] = jnp.dot(h_ref[...], w_ref[...], preferred_element_type=F32).astype(o_ref.dtype)


def _in_projection(xa, norm_w, scale, shift, w_p, w_side, n_lat, seq):
    t, d = xa.shape
    tm = _pick(seq, (768, 512, 256))
    tn = _pick(P_WIDTH, (1536, 1024, 512))
    tpb = seq // tm
    kern = functools.partial(_inproj_kernel, tm=tm, tiles_per_batch=tpb, n_lat=n_lat)
    return pl.pallas_call(
        kern,
        grid=(t // tm, P_WIDTH // tn),
        in_specs=[pl.BlockSpec((tm, d), lambda i, j: (i, 0)),
                  pl.BlockSpec((1, d), lambda i, j: (0, 0)),
                  pl.BlockSpec((None, 2, d), lambda i, j: (i // tpb, 0, 0)),
                  pl.BlockSpec((None, 2, d), lambda i, j: (i // tpb, 0, 0)),
                  pl.BlockSpec((d, tn), lambda i, j: (0, j)),
                  pl.BlockSpec((d, 2 * LANE), lambda i, j: (0, 0))],
        out_specs=[pl.BlockSpec((tm, tn), lambda i, j: (i, j)),
                   pl.BlockSpec((tm, 2 * LANE), lambda i, j: (i, 0))],
        out_shape=[jax.ShapeDtypeStruct((t, P_WIDTH), BF16),
                   jax.ShapeDtypeStruct((t, 2 * LANE), F32)],
        scratch_shapes=[pltpu.VMEM((tm, d), BF16)],
        compiler_params=_params(("parallel", "arbitrary")),
        name="in_proj",
    )(xa, norm_w.reshape(1, d), scale, shift, w_p, w_side)


def _conv_kernel(u_ref, prev_ref, next_ref, w_ref, b_ref, s_ref, o_ref, *,
                 tm, tiles_per_batch, lat_tiles):
    ib = pl.program_id(0) % tiles_per_batch
    first = jnp.logical_or(ib == 0, ib == lat_tiles)
    last = jnp.logical_or(ib == lat_tiles - 1, ib == tiles_per_batch - 1)
    half = SSM_CONV // 2
    u = u_ref[...]
    acc = b_ref[...] + w_ref[half:half + 1, :] * u.astype(F32)
    for idx, k in enumerate(k for k in range(SSM_CONV) if k != half):
        acc = acc + w_ref[k:k + 1, :] * jnp.dot(s_ref[idx], u, preferred_element_type=F32)
    prev = jnp.where(first, 0.0, prev_ref[...].astype(F32))
    nxt = jnp.where(last, 0.0, next_ref[...].astype(F32))
    row = lax.broadcasted_iota(jnp.int32, (8, 1), 0)
    top = jnp.zeros_like(prev)
    bot = jnp.zeros_like(nxt)
    for k in range(half):
        reach = half - k
        top = top + w_ref[k:k + 1, :] * jnp.where(row < reach, pltpu.roll(prev, reach, axis=0), 0.0)
        kk = SSM_CONV - 1 - k
        bot = bot + w_ref[kk:kk + 1, :] * jnp.where(row >= 8 - reach,
                                                     pltpu.roll(nxt, 8 - reach, axis=0), 0.0)
    y = jnp.concatenate([acc[0:8] + top, acc[8:tm - 8], acc[tm - 8:tm] + bot], axis=0)
    o_ref[...] = _silu(y).astype(o_ref.dtype)


def _conv_silu(p, conv_w, conv_b, n_lat, seq):
    t = p.shape[0]
    tm = 256
    tc = 1024
    tpb = seq // tm
    n_row8 = t // 8
    col0 = P_XBC // tc
    kern = functools.partial(_conv_kernel, tm=tm, tiles_per_batch=tpb, lat_tiles=n_lat // tm)
    half = SSM_CONV // 2
    shifts = np.stack([np.eye(tm, k=k - half, dtype=np.float32) for k in range(SSM_CONV) if k != half])
    return pl.pallas_call(
        kern,
        grid=(t // tm, SSM_CONV_DIM // tc),
        in_specs=[pl.BlockSpec((tm, tc), lambda i, j: (i, col0 + j)),
                  pl.BlockSpec((8, tc), lambda i, j: (jnp.maximum(i * (tm // 8) - 1, 0), col0 + j)),
                  pl.BlockSpec((8, tc), lambda i, j: (jnp.minimum((i + 1) * (tm // 8), n_row8 - 1),
                                                      col0 + j)),
                  pl.BlockSpec((8, tc), lambda i, j: (0, j)),
                  pl.BlockSpec((1, tc), lambda i, j: (0, j)),
                  pl.BlockSpec((SSM_CONV - 1, tm, tm), lambda i, j: (0, 0, 0))],
        out_specs=pl.BlockSpec((tm, tc), lambda i, j: (i, j)),
        out_shape=jax.ShapeDtypeStruct((t, SSM_CONV_DIM), BF16),
        compiler_params=_params(("parallel", "parallel")),
        name="ssm_conv",
    )(p, p, p, jnp.pad(conv_w, ((0, 8 - SSM_CONV), (0, 0))), conv_b.reshape(1, -1),
      jnp.asarray(shifts, BF16))


def _split_dot(a_bf16, v):
    v1 = v.astype(BF16)
    r1 = v - v1.astype(F32)
    v2 = r1.astype(BF16)
    v3 = (r1 - v2.astype(F32)).astype(BF16)
    return jnp.dot(jnp.concatenate([a_bf16, a_bf16, a_bf16], axis=1),
                   jnp.concatenate([v1, v2, v3], axis=0), preferred_element_type=F32)


def _ssd_kernel(uf_ref, ub_ref, dtf_ref, dtb_ref, alog_ref, bias_ref, eh_ref, yf_ref, yb_ref,
                stf_ref, stb_ref):
    @pl.when(pl.program_id(1) == 0)
    def _():
        stf_ref[...] = jnp.zeros_like(stf_ref)
        stb_ref[...] = jnp.zeros_like(stb_ref)

    _ssd_chunk(True, uf_ref, dtf_ref, alog_ref[0], bias_ref[0], eh_ref, yf_ref, stf_ref)
    _ssd_chunk(False, ub_ref, dtb_ref, alog_ref[1], bias_ref[1], eh_ref, yb_ref, stb_ref)


def _ssd_chunk(fwd, u_ref, dtr_ref, a_log, dt_bias, eh_ref, y_ref, st_ref):
    cl = SSM_CHUNK
    raw = dtr_ref[...] + dt_bias
    dt = jnp.maximum(raw, 0.0) + jnp.log(1.0 + jnp.exp(-jnp.abs(raw)))
    da = dt * (-jnp.exp(a_log))
    r = lax.broadcasted_iota(jnp.int32, (cl, cl), 0)
    c = lax.broadcasted_iota(jnp.int32, (cl, cl), 1)
    tri = (r >= c) if fwd else (r <= c)
    acs = _split_dot(jnp.where(tri, 1.0, 0.0).astype(BF16), da)
    total = acs[cl - 1:cl, :] if fwd else acs[0:1, :]
    e_acs = jnp.exp(acs)
    w_end = dt * jnp.exp(total - acs)
    acs_t = acs.T
    dt_t = dt.T

    def hi_lo(v):
        hi = v.astype(BF16)
        return jnp.concatenate([hi, (v - hi.astype(F32)).astype(BF16)], axis=1)

    ex = jnp.dot(jnp.concatenate([hi_lo(w_end), hi_lo(e_acs)], axis=0), eh_ref[...],
                 preferred_element_type=F32)
    w_exp = ex[0:cl]
    e_exp = ex[cl:2 * cl]
    dec = e_exp[cl - 1:cl, :] if fwd else e_exp[0:1, :]
    xw =(u_ref[:, 0:SSM_INNER].astype(F32) * w_exp).astype(BF16)
    lane = lax.broadcasted_iota(jnp.int32, (cl, LANE), 1)
    gw = SSM_INNER // SSM_GROUPS
    hpg = SSM_HEADS // SSM_GROUPS
    for g in range(SSM_GROUPS):
        b_g = u_ref[:, SSM_INNER + g * SSM_STATE:SSM_INNER + (g + 1) * SSM_STATE]
        c_off = SSM_INNER + SSM_GROUPS * SSM_STATE
        c_g = u_ref[:, c_off + g * SSM_STATE:c_off + (g + 1) * SSM_STATE]
        cb = lax.dot_general(c_g, b_g, (((1,), (1,)), ((), ())), preferred_element_type=F32)
        st_g = st_ref[:, g * gw:(g + 1) * gw]
        y_off = jnp.dot(c_g, st_g.astype(BF16), preferred_element_type=F32)
        ys = []
        for k in range(hpg // 2):
            ms = []
            for h in (g * hpg + 2 * k, g * hpg + 2 * k + 1):
                seg = acs[:, h:h + 1] - acs_t[h:h + 1, :]
                dec_h = jnp.exp(jnp.where(tri, seg, NEG_BIG))
                ms.append((dec_h * cb * dt_t[h:h + 1, :]).astype(BF16))
            x_p = u_ref[:, g * gw + k * LANE:g * gw + (k + 1) * LANE]
            zero = jnp.zeros_like(x_p)
            rhs = jnp.concatenate([jnp.where(lane < SSM_HEAD_DIM, x_p, zero),
                                   jnp.where(lane >= SSM_HEAD_DIM, x_p, zero)], axis=0)
            ys.append(jnp.dot(jnp.concatenate(ms, axis=1), rhs, preferred_element_type=F32))
        y_g = jnp.concatenate(ys, axis=1) + y_off * e_exp[:, g * gw:(g + 1) * gw]
        y_ref[:, g * gw:(g + 1) * gw] = y_g.astype(y_ref.dtype)
        upd = lax.dot_general(b_g, xw[:, g * gw:(g + 1) * gw], (((0,), (0,)), ((), ())),
                              preferred_element_type=F32)
        st_ref[:, g * gw:(g + 1) * gw] = st_g * dec[:, g * gw:(g + 1) * gw] + upd


def _ssd(u, dtr, a_log, dt_bias, batch, n_lat, seq):
    cl = SSM_CHUNK
    nch, nlat, nctx = seq // cl, n_lat // cl, (seq - n_lat) // cl

    def fchunk(c):
        return jnp.where(c < nctx, nlat + c, c - nctx)

    def bchunk(c):
        return jnp.where(c < nctx, nlat + nctx - 1 - c, nlat - 1 - (c - nctx))

    def pad_heads(v):
        return jnp.pad(v, ((0, 0), (0, LANE - SSM_HEADS))).reshape(2, 1, LANE)

    eh = np.zeros((2 * LANE, SSM_INNER), np.float32)
    for h in range(SSM_HEADS):
        eh[h, h * SSM_HEAD_DIM:(h + 1) * SSM_HEAD_DIM] = 1.0
        eh[LANE + h, h * SSM_HEAD_DIM:(h + 1) * SSM_HEAD_DIM] = 1.0
    return pl.pallas_call(
        _ssd_kernel,
        grid=(batch, nch),
        in_specs=[pl.BlockSpec((cl, SSM_CONV_DIM), lambda b, c: (b * nch + fchunk(c), 0)),
                  pl.BlockSpec((cl, SSM_CONV_DIM), lambda b, c: (b * nch + bchunk(c), 0)),
                  pl.BlockSpec((cl, LANE), lambda b, c: (b * nch + fchunk(c), 0)),
                  pl.BlockSpec((cl, LANE), lambda b, c: (b * nch + bchunk(c), 1)),
                  pl.BlockSpec((2, 1, LANE), lambda b, c: (0, 0, 0)),
                  pl.BlockSpec((2, 1, LANE), lambda b, c: (0, 0, 0)),
                  pl.BlockSpec((2 * LANE, SSM_INNER), lambda b, c: (0, 0))],
        out_specs=[pl.BlockSpec((cl, SSM_INNER), lambda b, c: (b * nch + fchunk(c), 0)),
                   pl.BlockSpec((cl, SSM_INNER), lambda b, c: (b * nch + bchunk(c), 0))],
        out_shape=[jax.ShapeDtypeStruct((u.shape[0], SSM_INNER), BF16)] * 2,
        scratch_shapes=[pltpu.VMEM((SSM_STATE, SSM_INNER), F32)] * 2,
        compiler_params=_params(("parallel", "arbitrary")),
        name="ssd_scan",
    )(u, u, dtr, dtr, pad_heads(a_log), pad_heads(dt_bias), jnp.asarray(eh, BF16))


def _ssm_out_kernel(yf_ref, yb_ref, x_ref, z_ref, skip_ref, nw_ref, o_ref):
    z = z_ref[...].astype(F32)
    g = (yf_ref[...].astype(F32) + yb_ref[...].astype(F32)
         + skip_ref[...] * x_ref[...].astype(F32)) * _silu(z)
    gw = SSM_INNER // SSM_GROUPS
    for k in range(SSM_GROUPS):
        gk = g[:, k * gw:(k + 1) * gw]
        gk = gk * lax.rsqrt(jnp.mean(gk * gk, axis=-1, keepdims=True) + EPS)
        o_ref[:, k * gw:(k + 1) * gw] = (gk * nw_ref[:, k * gw:(k + 1) * gw]).astype(o_ref.dtype)


def _ssm_out(yf, yb, u, p, d_skip, ssm_norm, seq):
    t = u.shape[0]
    tm = _pick(seq, (384, 256, 128))
    w = SSM_INNER
    return pl.pallas_call(
        _ssm_out_kernel,
        grid=(t // tm,),
        in_specs=[pl.BlockSpec((tm, w), lambda i: (i, 0)),
                  pl.BlockSpec((tm, w), lambda i: (i, 0)),
                  pl.BlockSpec((tm, w), lambda i: (i, 0)),
                  pl.BlockSpec((tm, w), lambda i: (i, P_Z // w)),
                  pl.BlockSpec((1, w), lambda i: (0, 0)),
                  pl.BlockSpec((1, w), lambda i: (0, 0))],
        out_specs=pl.BlockSpec((tm, w), lambda i: (i, 0)),
        out_shape=jax.ShapeDtypeStruct((t, w), BF16),
        compiler_params=_params(("parallel",)),
        name="ssm_gated_norm",
    )(yf, yb, u, p, jnp.repeat(d_skip, SSM_HEAD_DIM).reshape(1, w), ssm_norm.reshape(1, w))


def _head_norm_rope_kernel(x_ref, nw_ref, cos_ref, sin_ref, o_ref, *, n_heads):
    for h in range(n_heads):
        sl = slice(h * LANE, (h + 1) * LANE)
        x = x_ref[:, sl].astype(F32)
        y = x * lax.rsqrt(jnp.mean(x * x, axis=-1, keepdims=True) + EPS) * nw_ref[...]
        o_ref[:, sl] = _rope(y, cos_ref[...], sin_ref[...]).astype(o_ref.dtype)


def _head_norm_rope(p, col0, n_heads, norm_w, cos, sin, seq):
    t = p.shape[0]
    tm = _pick(seq, (768, 512, 256))
    tpb = seq // tm
    w = n_heads * LANE
    return pl.pallas_call(
        functools.partial(_head_norm_rope_kernel, n_heads=n_heads),
        grid=(t // tm,),
        in_specs=[pl.BlockSpec((tm, w), lambda i: (i, col0 // w)),
                  pl.BlockSpec((1, LANE), lambda i: (0, 0)),
                  pl.BlockSpec((tm, LANE), lambda i: (i % tpb, 0)),
                  pl.BlockSpec((tm, LANE), lambda i: (i % tpb, 0))],
        out_specs=pl.BlockSpec((tm, w), lambda i: (i, 0)),
        out_shape=jax.ShapeDtypeStruct((t, w), BF16),
        compiler_params=_params(("parallel",)),
        name="head_norm_rope",
    )(p, norm_w.reshape(1, LANE), cos, sin)


def _mla_q_kernel(x_ref, nw_ref, w_ref, cos_ref, sin_ref, o_ref):
    x = x_ref[:, 0:MLA_Q_LORA].astype(F32)
    h = (x * lax.rsqrt(jnp.mean(x * x, axis=-1, keepdims=True) + EPS) * nw_ref[...]).astype(BF16)
    for hd in range(MLA_HEADS):
        c0 = hd * MLA_QK_PAD
        res = jnp.dot(h, w_ref[:, c0:c0 + MLA_QK_PAD], preferred_element_type=F32)
        o_ref[:, c0:c0 + LANE] = res[:, 0:LANE].astype(o_ref.dtype)
        o_ref[:, c0 + LANE:c0 + MLA_QK_PAD] = _rope(res[:, LANE:], cos_ref[...],
                                                    sin_ref[...]).astype(o_ref.dtype)


def _mla_q(p, q_norm, w_uq, cos, sin, seq):
    t = p.shape[0]
    tm = _pick(seq, (768, 512, 256))
    tpb = seq // tm
    kdim = MQA_PAD
    wo = MLA_HEADS * MLA_QK_PAD
    return pl.pallas_call(
        _mla_q_kernel,
        grid=(t // tm,),
        in_specs=[pl.BlockSpec((tm, kdim), lambda i: (i, P_MQA // kdim)),
                  pl.BlockSpec((1, MLA_Q_LORA), lambda i: (0, 0)),
                  pl.BlockSpec((MLA_Q_LORA, wo), lambda i: (0, 0)),
                  pl.BlockSpec((tm, LANE), lambda i: (i % tpb, 0)),
                  pl.BlockSpec((tm, LANE), lambda i: (i % tpb, 0))],
        out_specs=pl.BlockSpec((tm, wo), lambda i: (i, 0)),
        out_shape=jax.ShapeDtypeStruct((t, wo), BF16),
        compiler_params=_params(("parallel",)),
        name="mla_q_up",
    )(p, q_norm.reshape(1, MLA_Q_LORA), w_uq, cos, sin)


def _mla_kv_kernel(x_ref, kr_ref, nw_ref, w_ref, cos_ref, sin_ref, k_ref, v_ref):
    x = x_ref[...].astype(F32)
    h = (x * lax.rsqrt(jnp.mean(x * x, axis=-1, keepdims=True) + EPS) * nw_ref[...]).astype(BF16)
    k_rope = _rope(kr_ref[...].astype(F32), cos_ref[...], sin_ref[...]).astype(k_ref.dtype)
    for hd in range(MLA_HEADS):
        c0 = hd * MLA_QK_PAD
        res = jnp.dot(h, w_ref[:, c0:c0 + 2 * LANE], preferred_element_type=F32)
        k_ref[:, c0:c0 + LANE] = res[:, 0:LANE].astype(k_ref.dtype)
        k_ref[:, c0 + LANE:c0 + MLA_QK_PAD] = k_rope
        v_ref[:, hd * LANE:(hd + 1) * LANE] = res[:, LANE:].astype(v_ref.dtype)


def _mla_kv(p, kv_norm, w_ukv, cos, sin, seq):
    t = p.shape[0]
    tm = _pick(seq, (768, 512, 256))
    tpb = seq // tm
    kdim = MLA_KV_LORA
    wk = MLA_HEADS * MLA_QK_PAD
    return pl.pallas_call(
        _mla_kv_kernel,
        grid=(t // tm,),
        in_specs=[pl.BlockSpec((tm, kdim), lambda i: (i, P_MKVA // kdim)),
                  pl.BlockSpec((tm, LANE), lambda i: (i, P_MKR // LANE)),
                  pl.BlockSpec((1, kdim), lambda i: (0, 0)),
                  pl.BlockSpec((kdim, wk), lambda i: (0, 0)),
                  pl.BlockSpec((tm, LANE), lambda i: (i % tpb, 0)),
                  pl.BlockSpec((tm, LANE), lambda i: (i % tpb, 0))],
        out_specs=[pl.BlockSpec((tm, wk), lambda i: (i, 0)),
                   pl.BlockSpec((tm, MLA_HEADS * LANE), lambda i: (i, 0))],
        out_shape=[jax.ShapeDtypeStruct((t, wk), BF16),
                   jax.ShapeDtypeStruct((t, MLA_HEADS * LANE), BF16)],
        compiler_params=_params(("parallel",)),
        name="mla_kv_up",
    )(p, p, kv_norm.reshape(1, kdim), w_ukv, cos, sin)


def _flash_kernel(q_ref, k_ref, v_ref, g_ref, *rest, r, tq, dq, tk, n_keys, scale):
    o_ref = rest[-1]
    q = jnp.concatenate([q_ref[:, j * dq:(j + 1) * dq] for j in range(r)], axis=0)
    q = (q.astype(F32) * (scale * math.log2(math.e))).astype(BF16)
    m_rows = r * tq
    m = jnp.full((m_rows, 1), NEG_BIG, F32)
    l = jnp.zeros((m_rows, LANE), F32)
    acc = jnp.zeros((m_rows, LANE), F32)
    for c in range(n_keys // tk):
        k_c = k_ref[c * tk:(c + 1) * tk, :]
        v_c = v_ref[c * tk:(c + 1) * tk, :]
        s = lax.dot_general(q, k_c, (((1,), (1,)), ((), ())), preferred_element_type=F32)
        m_new = jnp.maximum(m, jnp.max(s, axis=-1, keepdims=True))
        alpha = jnp.exp2(m - m_new)
        pr = jnp.exp2(s - m_new)
        part = pr[:, 0:LANE]
        for j in range(1, tk // LANE):
            part = part + pr[:, j * LANE:(j + 1) * LANE]
        l = alpha * l + part
        acc = alpha * acc + jnp.dot(pr.astype(BF16), v_c, preferred_element_type=F32)
        m = m_new
    o = acc / jnp.sum(l, axis=-1, keepdims=True)
    for j in range(r):
        gate = _silu(g_ref[:, j * LANE:(j + 1) * LANE].astype(F32))
        o_ref[:, j * LANE:(j + 1) * LANE] = (o[j * tq:(j + 1) * tq] * gate).astype(o_ref.dtype)


def _flash(q, k, v, gates, *, q_col0, k_col0, v_col0, g_col0, v_stride, n_kv_heads, r, dq, scale,
           batch, n_lat, seq, ctx_only, tq, into=None):
    n_ctx = seq - n_lat
    tk = FLASH_TK
    if ctx_only:
        tq = n_ctx
        q_blk0, n_q, kv_rows, kv_blk0 = n_lat // tq, 1, n_ctx, n_lat // n_ctx
    else:
        q_blk0, n_q, kv_rows, kv_blk0 = 0, n_lat // tq, seq, 0
    assert kv_rows % tk == 0
    kern = functools.partial(_flash_kernel, r=r, tq=tq, dq=dq, tk=tk, n_keys=kv_rows, scale=scale)
    w_out = n_kv_heads * r * LANE
    args = [a.reshape(batch, seq, a.shape[-1]) for a in (q, k, v, gates)]
    in_specs = [pl.BlockSpec((None, tq, r * dq), lambda b, g, i: (b, q_blk0 + i, q_col0 // (r * dq) + g)),
                pl.BlockSpec((None, kv_rows, dq), lambda b, g, i: (b, kv_blk0, k_col0 // dq + g)),
                pl.BlockSpec((None, kv_rows, LANE),
                             lambda b, g, i: (b, kv_blk0, v_col0 // LANE + v_stride * g)),
                pl.BlockSpec((None, tq, r * LANE),
                             lambda b, g, i: (b, q_blk0 + i, g_col0 // (r * LANE) + g))]
    aliases = {}
    if into is not None:
        args.append(into.reshape(batch, seq, w_out))
        in_specs.append(pl.BlockSpec(memory_space=pl.ANY))
        aliases = {len(args) - 1: 0}
    out = pl.pallas_call(
        kern,
        grid=(batch, n_kv_heads, n_q),
        in_specs=in_specs,
        out_specs=pl.BlockSpec((None, tq, r * LANE), lambda b, g, i: (b, q_blk0 + i, g)),
        out_shape=jax.ShapeDtypeStruct((batch, seq, w_out), BF16),
        input_output_aliases=aliases,
        compiler_params=_params(("parallel", "parallel", "parallel")),
        name="flash_ctx" if ctx_only else "flash_latent",
    )(*args)
    return out.reshape(batch * seq, w_out)


def _attention(q, k, v, gates, tq, **kw):
    lat = _flash(q, k, v, gates, ctx_only=False, tq=tq, **kw)
    return _flash(q, k, v, gates, ctx_only=True, tq=tq, into=lat, **kw)


def _natten_kernel(q_ref, k0_ref, k1_ref, k2_ref, kc_ref, v0_ref, v1_ref, v2_ref, vc_ref, g_ref,
                   bias_ref, o_ref, *, scale, n_heads):
    n_loc = bias_ref.shape[-1]
    for h in range(n_heads):
        sl = slice(h * LANE, (h + 1) * LANE)
        q = (q_ref[:, sl].astype(F32) * (scale * math.log2(math.e))).astype(BF16)
        k = jnp.concatenate([k0_ref[:, sl], k1_ref[:, sl], k2_ref[:, sl], kc_ref[:, sl]], axis=0)
        v = jnp.concatenate([v0_ref[:, sl], v1_ref[:, sl], v2_ref[:, sl], vc_ref[:, sl]], axis=0)
        s = lax.dot_general(q, k, (((1,), (1,)), ((), ())), preferred_element_type=F32)
        s = jnp.concatenate([s[:, :n_loc] + bias_ref[h], s[:, n_loc:]], axis=1)
        pr = jnp.exp2(s - jnp.max(s, axis=-1, keepdims=True))
        l = jnp.sum(pr, axis=-1, keepdims=True)
        o = jnp.dot(pr.astype(BF16), v, preferred_element_type=F32) / l
        o_ref[:, sl] = (o * _silu(g_ref[:, sl].astype(F32))).astype(o_ref.dtype)


def _na_bias_table(rpb):
    qx = np.arange(GRID_W)
    c0 = np.clip(qx - NA_WIN_W // 2, 0, GRID_W - NA_WIN_W)
    col_ok = (qx[None, :] >= c0[:, None]) & (qx[None, :] < c0[:, None] + NA_WIN_W)
    dx = qx[None, :] - qx[:, None] + NA_WIN_W - 1
    pick = np.zeros((2 * NA_WIN_W - 1, GRID_W * GRID_W), np.float32)
    qi, ki = np.nonzero(col_ok)
    pick[dx[qi, ki], qi * GRID_W + ki] = 1.0
    by_dx = jnp.einsum("...d,dn->...n", rpb, pick, precision=lax.Precision.HIGHEST)
    qy, ky = np.arange(NA_QROWS), np.arange(NA_KROWS)
    dys, oks = [], []
    for q_off, first_key in ((0, None), (NA_WIN_H // 2, "q"), (NA_WIN_H, NA_KROWS - NA_WIN_H)):
        r0 = qy if first_key == "q" else np.full_like(qy, 0 if first_key is None else first_key)
        oks.append((ky[None, :] >= r0[:, None]) & (ky[None, :] < r0[:, None] + NA_WIN_H))
        dys.append(np.clip(ky[None, :] - (qy[:, None] + q_off) + NA_WIN_H - 1, 0, 2 * NA_WIN_H - 2))
    rows = jnp.take(by_dx, np.stack(dys).reshape(-1), axis=-2)
    lead = rows.shape[:-2]
    nl = len(lead)
    rows = rows.reshape(lead + (3, NA_QROWS, NA_KROWS, GRID_W, GRID_W))
    rows = rows.transpose(tuple(range(nl)) + (nl, nl + 1, nl + 3, nl + 2, nl + 4))
    ok = np.stack(oks)[:, :, None, :, None] & col_ok[None, None, :, None, :]
    tab = jnp.where(jnp.asarray(ok), rows * math.log2(math.e), NEG_BIG)
    return tab.reshape(lead + (3, NA_QROWS * GRID_W, NA_KROWS * GRID_W))


def _natten(p, bias, batch, n_lat, seq):
    n_ctx = seq - n_lat
    tq = NA_QROWS * GRID_W
    n_blk = n_lat // tq
    assert n_blk >= 3 and n_lat % tq == 0 and n_ctx % tq == 0
    rb = seq // tq
    cb = seq // n_ctx
    nk = NA_KROWS * GRID_W
    hps = NA_HEADS_PER_STEP
    hw = hps * LANE

    def kblk(i):
        return jnp.clip(i - 1, 0, n_blk - 3)

    def kspec(col0, j):
        return pl.BlockSpec((tq, hw), lambda b, h, i: (b * rb + kblk(i) + j, col0 // hw + h))

    def cspec(col0):
        return pl.BlockSpec((n_ctx, hw), lambda b, h, i: (b * cb + n_lat // n_ctx, col0 // hw + h))

    def qspec(col0):
        return pl.BlockSpec((tq, hw), lambda b, h, i: (b * rb + i, col0 // hw + h))

    def btype(i):
        return jnp.where(i == 0, 0, jnp.where(i == n_blk - 1, 2, 1))

    return pl.pallas_call(
        functools.partial(_natten_kernel, scale=HEAD_DIM ** -0.5, n_heads=hps),
        grid=(batch, NA_HEADS // hps, n_blk),
        in_specs=[qspec(P_NQ), kspec(P_NK, 0), kspec(P_NK, 1), kspec(P_NK, 2), cspec(P_NK),
                  kspec(P_NV, 0), kspec(P_NV, 1), kspec(P_NV, 2), cspec(P_NV), qspec(P_NG),
                  pl.BlockSpec((hps, None, tq, nk), lambda b, h, i: (h, btype(i), 0, 0))],
        out_specs=pl.BlockSpec((tq, hw), lambda b, h, i: (b * rb + i, h)),
        out_shape=jax.ShapeDtypeStruct((p.shape[0], NA_HEADS * LANE), BF16),
        compiler_params=_params(("parallel", "parallel", "parallel")),
        name="natten_latent",
    )(p, p, p, p, p, p, p, p, p, p, bias)


def _merge_kernel(o0_ref, o1_ref, o2_ref, o3_ref, w0_ref, w1_ref, w2_ref, w3_ref,
                  m0_ref, m1_ref, m2_ref, m3_ref, y_ref):
    acc = None
    for o_ref, w_ref, m_ref in ((o0_ref, w0_ref, m0_ref), (o1_ref, w1_ref, m1_ref),
                                (o2_ref, w2_ref, m2_ref), (o3_ref, w3_ref, m3_ref)):
        gate = _sigmoid(m_ref[...].astype(F32))
        term = gate * jnp.dot(o_ref[...], w_ref[...], preferred_element_type=F32)
        acc = term if acc is None else acc + term
    y_ref[...] = acc.astype(y_ref.dtype)


def _merge(outs, weights, p, seq):
    t = p.shape[0]
    d = weights[0].shape[1]
    tm = _pick(seq, (768, 512, 256))
    tn = 512
    o_specs = [pl.BlockSpec((tm, o.shape[1]), lambda i, j: (i, 0)) for o in outs]
    w_specs = [pl.BlockSpec((w.shape[0], tn), lambda i, j: (0, j)) for w in weights]
    m_specs = [pl.BlockSpec((tm, tn), lambda i, j, b=b: (i, (P_MIX + b * d) // tn + j))
               for b in range(N_BRANCH)]
    return pl.pallas_call(
        _merge_kernel,
        grid=(t // tm, d // tn),
        in_specs=o_specs + w_specs + m_specs,
        out_specs=pl.BlockSpec((tm, tn), lambda i, j: (i, j)),
        out_shape=jax.ShapeDtypeStruct((t, d), BF16),
        compiler_params=_params(("parallel", "parallel")),
        name="branch_merge",
    )(*outs, *weights, p, p, p, p)


def _out_kernel(y_ref, w_ref, x_ref, nw_ref, gate_ref, o_ref, *, tm, tiles_per_batch, n_lat):
    z = jnp.dot(y_ref[...], w_ref[...], preferred_element_type=F32)
    zn = z * lax.rsqrt(jnp.mean(z * z, axis=-1, keepdims=True) + EPS) * nw_ref[...]
    if tiles_per_batch is None:
        gate = gate_ref[0:1, :]
    else:
        is_ctx = _is_ctx_rows(pl.program_id(0), tm, tiles_per_batch, n_lat)
        gate = jnp.where(is_ctx, gate_ref[1:2, :], gate_ref[0:1, :])
    o_ref[...] = x_ref[...] + gate * zn


def _out_projection_latent(y, w_out, xa, norm_w, gate, batch, n_lat, seq):
    d = xa.shape[1]
    tm = _pick(n_lat, (512, 256, 128))
    kern = functools.partial(_out_kernel, tm=tm, tiles_per_batch=None, n_lat=n_lat)
    return pl.pallas_call(
        kern,
        grid=(batch, n_lat // tm),
        in_specs=[pl.BlockSpec((None, tm, d), lambda b, i: (b, i, 0)),
                  pl.BlockSpec((d, d), lambda b, i: (0, 0)),
                  pl.BlockSpec((None, tm, d), lambda b, i: (b, i, 0)),
                  pl.BlockSpec((1, d), lambda b, i: (0, 0)),
                  pl.BlockSpec((None, 2, d), lambda b, i: (b, 0, 0))],
        out_specs=pl.BlockSpec((None, tm, d), lambda b, i: (b, i, 0)),
        out_shape=jax.ShapeDtypeStruct((batch, n_lat, d), F32),
        compiler_params=_params(("parallel", "parallel")),
        name="out_proj_last",
    )(y.reshape(batch, seq, d), w_out, xa.reshape(batch, seq, d), norm_w.reshape(1, d), gate)


def _out_projection(y, w_out, xa, norm_w, gate, n_lat, seq):
    t, d = xa.shape
    tm = _pick(seq, (768, 384, 256, 128))
    tpb = seq // tm
    kern = functools.partial(_out_kernel, tm=tm, tiles_per_batch=tpb, n_lat=n_lat)
    return pl.pallas_call(
        kern,
        grid=(t // tm,),
        in_specs=[pl.BlockSpec((tm, d), lambda i: (i, 0)),
                  pl.BlockSpec((d, d), lambda i: (0, 0)),
                  pl.BlockSpec((tm, d), lambda i: (i, 0)),
                  pl.BlockSpec((1, d), lambda i: (0, 0)),
                  pl.BlockSpec((None, 2, d), lambda i: (i // tpb, 0, 0))],
        out_specs=pl.BlockSpec((tm, d), lambda i: (i, 0)),
        out_shape=jax.ShapeDtypeStruct((t, d), F32),
        compiler_params=_params(("parallel",)),
        name="out_proj",
    )(y, w_out, xa, norm_w.reshape(1, d), gate)


def _pairs_apart(w, n_heads, dim):
    lead = w.shape[:-1]
    return w.reshape(lead + (n_heads, dim // 2, 2)).swapaxes(-1, -2).reshape(lead + (n_heads * dim,))


def _rope_tile(w):
    lead = w.shape[:-1]
    pr = w.reshape(lead + (MLA_ROPE // 2, 2))
    zero = jnp.zeros(lead + (MLA_ROPE // 2,), w.dtype)
    return jnp.concatenate([pr[..., 0], zero, pr[..., 1], zero], axis=-1)


def _layout_w_in(w):
    o = _OFF
    k = w.shape[0]

    def seg(name, width):
        return w[:, o[name]:o[name] + width]

    cols = [seg("mix", 8192), seg("z", 2048), seg("xbc", SSM_CONV_DIM),
            _pairs_apart(seg("gq", 1024), GQA_HEADS, HEAD_DIM), seg("gg", 1024),
            seg("nq", 1024), seg("nk", 1024), seg("nv", 1024), seg("ng", 1024), seg("mg", 1024),
            seg("mqa", MLA_Q_LORA), _rope_tile(seg("mkr", MLA_ROPE)),
            jnp.zeros((k, MQA_PAD - MLA_Q_LORA - LANE), w.dtype),
            _pairs_apart(seg("gk", 512), GQA_KV_HEADS, HEAD_DIM), seg("gv", 512),
            seg("mkva", MLA_KV_LORA)]
    main = jnp.concatenate(cols, axis=1).astype(BF16)
    assert main.shape[1] == P_WIDTH
    dtr = seg("dtr", 2 * SSM_HEADS)
    zero = jnp.zeros((k, LANE - SSM_HEADS), w.dtype)
    side = jnp.concatenate([dtr[:, :SSM_HEADS], zero, dtr[:, SSM_HEADS:], zero], axis=1).astype(BF16)
    return main, side


def _layout_w_uq(w_uq):
    k = w_uq.shape[0]
    w = w_uq.reshape(k, MLA_HEADS, MLA_NOPE + MLA_ROPE)
    w = jnp.concatenate([w[..., :MLA_NOPE], _rope_tile(w[..., MLA_NOPE:])], axis=-1)
    return w.reshape(k, MLA_HEADS * MLA_QK_PAD).astype(BF16)


def _rope_tables(n_lat, n_ctx, dim):
    t = np.arange(n_lat)
    quarter = dim // 4
    freqs = ROPE_THETA ** (-jnp.arange(quarter, dtype=F32) / quarter)
    row = jnp.asarray(t // GRID_W, F32)
    col = jnp.asarray(t % GRID_W, F32)
    ang = jnp.concatenate([row[:, None] * freqs, col[:, None] * freqs], axis=-1)
    cos, sin = jnp.cos(ang), jnp.sin(ang)
    pad = 64 - dim // 2
    one, zero = jnp.ones((n_lat, pad), F32), jnp.zeros((n_lat, pad), F32)
    cos_t = jnp.concatenate([cos, one, cos, one], axis=-1)
    sin_t = jnp.concatenate([-sin, zero, sin, zero], axis=-1)
    cos_t = jnp.concatenate([cos_t, jnp.ones((n_ctx, LANE), F32)], axis=0)
    sin_t = jnp.concatenate([sin_t, jnp.zeros((n_ctx, LANE), F32)], axis=0)
    return cos_t, sin_t


def _layer(xa, mod, rope_g, rope_m, lp, batch, n_lat, seq, last):
    d = xa.shape[1]

    def per_row(v):
        return jnp.stack([v[:batch], jnp.broadcast_to(v[batch:batch + 1], (batch, d))], axis=1)

    shift, scale, gate = (per_row(mod[:, k * d:(k + 1) * d]) for k in range(3))
    w_main, w_side = _layout_w_in(lp["w_in"])
    p, dtr = _in_projection(xa, lp["norm_pre"], scale, shift, w_main, w_side, n_lat, seq)

    u = _conv_silu(p, lp["conv_w"], lp["conv_b"], n_lat, seq)
    yf, yb = _ssd(u, dtr, lp["a_log"], lp["dt_bias"], batch, n_lat, seq)
    o_ssm = _ssm_out(yf, yb, u, p, lp["d_skip"], lp["ssm_norm"], seq)

    common = dict(batch=batch, n_lat=n_lat, seq=seq)
    qg = _head_norm_rope(p, P_GQ, GQA_HEADS, _pairs_apart(lp["gqa_q_norm"], 1, HEAD_DIM), *rope_g, seq)
    kg = _head_norm_rope(p, P_GK, GQA_KV_HEADS, _pairs_apart(lp["gqa_k_norm"], 1, HEAD_DIM), *rope_g, seq)
    o_gqa = _attention(
        qg, kg, p, p, GQA_TQ, q_col0=0, k_col0=0, v_col0=P_GV, g_col0=P_GG, v_stride=1,
        n_kv_heads=GQA_KV_HEADS, r=GQA_HEADS // GQA_KV_HEADS, dq=HEAD_DIM, scale=HEAD_DIM ** -0.5,
        **common)

    na_lat = _natten(p, lp["na_bias"], **common)
    o_na = _flash(p, p, p, p, q_col0=P_NQ, k_col0=P_NK, v_col0=P_NV, g_col0=P_NG, v_stride=1,
                  n_kv_heads=NA_HEADS, r=1, dq=HEAD_DIM, scale=HEAD_DIM ** -0.5, ctx_only=True,
                  tq=256, into=na_lat, **common)

    qm = _mla_q(p, lp["mla_q_norm"], _layout_w_uq(lp["w_uq"]), *rope_m, seq)
    km, vm = _mla_kv(p, lp["mla_kv_norm"], lp["w_ukv"].astype(BF16), *rope_m, seq)
    o_mla = _attention(
        qm, km, vm, p, MLA_TQ, q_col0=0, k_col0=0, v_col0=0, g_col0=P_MG, v_stride=1,
        n_kv_heads=MLA_HEADS, r=1, dq=MLA_QK_PAD, scale=(MLA_NOPE + MLA_ROPE) ** -0.5,
        **common)

    weights = [lp[n].astype(BF16) for n in ("w_o_ssm", "w_o_gqa", "w_o_na", "w_o_mla")]
    ymix = _merge([o_ssm, o_gqa, o_na, o_mla], weights, p, seq)
    w_out = lp["w_out"].astype(BF16)
    if last:
        return _out_projection_latent(ymix, w_out, xa, lp["norm_post"], gate, batch, n_lat, seq)
    return _out_projection(ymix, w_out, xa, lp["norm_post"], gate, n_lat, seq)


def kernel(x, c, ctx, c_ctx, ada_w, ada_b, norm_pre, norm_post, w_in, conv_w, conv_b, a_log, dt_bias,
           d_skip, ssm_norm, w_o_ssm, gqa_q_norm, gqa_k_norm, w_o_gqa, na_rpb, w_o_na, mla_q_norm,
           w_uq, mla_kv_norm, w_ukv, w_o_mla, w_out):
    batch, n_lat, d = x.shape
    n_ctx = ctx.shape[1]
    seq = n_lat + n_ctx
    stacked = dict(norm_pre=norm_pre, norm_post=norm_post, w_in=w_in,
                   conv_w=conv_w, conv_b=conv_b, a_log=a_log, dt_bias=dt_bias, d_skip=d_skip,
                   ssm_norm=ssm_norm, w_o_ssm=w_o_ssm, gqa_q_norm=gqa_q_norm, gqa_k_norm=gqa_k_norm,
                   w_o_gqa=w_o_gqa, na_bias=_na_bias_table(na_rpb), w_o_na=w_o_na,
                   mla_q_norm=mla_q_norm, w_uq=w_uq,
                   mla_kv_norm=mla_kv_norm, w_ukv=w_ukv, w_o_mla=w_o_mla, w_out=w_out)
    xa = jnp.concatenate([x, ctx], axis=1).reshape(batch * seq, d)
    cc = jnp.concatenate([c, c_ctx[None, :], jnp.zeros((8 - batch - 1, d), c.dtype)], axis=0)
    rope_g = _rope_tables(n_lat, n_ctx, HEAD_DIM)
    rope_m = _rope_tables(n_lat, n_ctx, MLA_ROPE)
    depth = ada_w.shape[0]
    mods = _modulation(cc, ada_w, ada_b)
    for layer in range(depth):
        lp = {k: v[layer] for k, v in stacked.items()}
        xa = _layer(xa, mods[layer], rope_g, rope_m, lp, batch, n_lat, seq, last=layer == depth - 1)
    return xa
```

```python
import functools
import math

import jax
import jax.numpy as jnp
import numpy as np
from jax import lax
from jax.experimental import pallas as pl
from jax.experimental.pallas import tpu as pltpu

F32 = jnp.float32
BF16 = jnp.bfloat16

GRID_W = 64
EPS = 1e-6
ROPE_THETA = 10000.0

SSM_HEADS = 32
SSM_HEAD_DIM = 64
SSM_INNER = SSM_HEADS * SSM_HEAD_DIM
SSM_GROUPS = 4
SSM_STATE = 128
SSM_CONV = 5
SSM_CHUNK = 128
SSM_CONV_DIM = SSM_INNER + 2 * SSM_GROUPS * SSM_STATE

GQA_HEADS = 8
GQA_KV_HEADS = 4
HEAD_DIM = 128
NA_HEADS = 8
NA_WIN_H = 8
NA_WIN_W = 16
NA_QROWS = 4
NA_KROWS = NA_QROWS + NA_WIN_H
NA_HEADS_PER_STEP = 8

MLA_HEADS = 8
MLA_Q_LORA = 768
MLA_KV_LORA = 512
MLA_NOPE = 128
MLA_ROPE = 64
MLA_QK_PAD = 256

N_BRANCH = 4
LANE = 128
VMEM_LIMIT = 56 * 1024 * 1024
NEG_BIG = -1e30
FLASH_TK = 256
GQA_TQ = 512
MLA_TQ = 1024

_SIZES = (SSM_INNER, SSM_CONV_DIM, 2 * SSM_HEADS, 1024, 512, 512, 1024, 1024, 1024, 1024, 1024,
          MLA_Q_LORA, MLA_KV_LORA, MLA_ROPE, 1024, N_BRANCH * 2048)
_OFF = dict(zip(("z", "xbc", "dtr", "gq", "gk", "gv", "gg", "nq", "nk", "nv", "ng",
                 "mqa", "mkva", "mkr", "mg", "mix"), np.cumsum((0,) + _SIZES[:-1]).tolist()))

P_MIX, P_Z, P_XBC = 0, 8192, 10240
P_GQ, P_GG, P_NQ, P_NK, P_NV, P_NG, P_MG, P_MQA = (13312, 14336, 15360, 16384, 17408, 18432,
                                                    19456, 20480)
P_GK, P_GV, P_MKVA = 21504, 22016, 22528
P_WIDTH = 23040
MQA_PAD = 1024
P_MKR = P_MQA + MLA_Q_LORA


def _pick(n, candidates):
    for c in candidates:
        if n % c == 0:
            return c
    raise ValueError(f"no tile for {n} among {candidates}")


def _params(sem):
    return pltpu.CompilerParams(dimension_semantics=sem, vmem_limit_bytes=VMEM_LIMIT)


def _sigmoid(v):
    return 0.5 + 0.5 * jnp.tanh(0.5 * v)


def _silu(v):
    return v * _sigmoid(v)


def _rope(v, cos, sin):
    return v * cos + pltpu.roll(v, 64, axis=1) * sin


def _mod_kernel(c_ref, w_ref, b_ref, o_ref):
    h = _silu(c_ref[...]).astype(BF16)
    o_ref[...] = jnp.dot(h, w_ref[...].astype(BF16), preferred_element_type=F32) + b_ref[...]


def _modulation(cc, ada_w, ada_b):
    rows, d = cc.shape
    depth, _, n = ada_w.shape
    tn = _pick(n, (512, 256, 128))
    return pl.pallas_call(
        _mod_kernel,
        grid=(depth, n // tn),
        in_specs=[pl.BlockSpec((rows, d), lambda l, j: (0, 0)),
                  pl.BlockSpec((None, d, tn), lambda l, j: (l, 0, j)),
                  pl.BlockSpec((None, 1, tn), lambda l, j: (l, 0, j))],
        out_specs=pl.BlockSpec((None, rows, tn), lambda l, j: (l, 0, j)),
        out_shape=jax.ShapeDtypeStruct((depth, rows, n), F32),
        compiler_params=_params(("parallel", "parallel")),
        name="adaln_mod",
    )(cc, ada_w, ada_b.reshape(depth, 1, n))


def _is_ctx_rows(i, tm, tiles_per_batch, n_lat):
    row = (i % tiles_per_batch) * tm + lax.broadcasted_iota(jnp.int32, (tm, 1), 0)
    return row >= n_lat


def _inproj_kernel(x_ref, nw_ref, sc_ref, sh_ref, w_ref, ws_ref, o_ref, os_ref, h_ref, *,
                   tm, tiles_per_batch, n_lat):
    i = pl.program_id(0)

    @pl.when(pl.program_id(1) == 0)
    def _():
        x = x_ref[...]
        y = x * lax.rsqrt(jnp.mean(x * x, axis=-1, keepdims=True) + EPS) * nw_ref[...]
        is_ctx = _is_ctx_rows(i, tm, tiles_per_batch, n_lat)
        sc = jnp.where(is_ctx, sc_ref[1:2, :], sc_ref[0:1, :])
        sh = jnp.where(is_ctx, sh_ref[1:2, :], sh_ref[0:1, :])
        h = (y * (1.0 + sc) + sh).astype(BF16)
        h_ref[...] = h
        os_ref[...] = jnp.dot(h, ws_ref[...], preferred_element_type=F32)

    o_ref[...] = jnp.dot(h_ref[...], w_ref[...], preferred_element_type=F32).astype(o_ref.dtype)


def _in_projection(xa, norm_w, scale, shift, w_p, w_side, n_lat, seq):
    t, d = xa.shape
    tm = _pick(seq, (768, 512, 256))
    tn = _pick(P_WIDTH, (1536, 1024, 512))
    tpb = seq // tm
    kern = functools.partial(_inproj_kernel, tm=tm, tiles_per_batch=tpb, n_lat=n_lat)
    return pl.pallas_call(
        kern,
        grid=(t // tm, P_WIDTH // tn),
        in_specs=[pl.BlockSpec((tm, d), lambda i, j: (i, 0)),
                  pl.BlockSpec((1, d), lambda i, j: (0, 0)),
                  pl.BlockSpec((None, 2, d), lambda i, j: (i // tpb, 0, 0)),
                  pl.BlockSpec((None, 2, d), lambda i, j: (i // tpb, 0, 0)),
                  pl.BlockSpec((d, tn), lambda i, j: (0, j)),
                  pl.BlockSpec((d, 2 * LANE), lambda i, j: (0, 0))],
        out_specs=[pl.BlockSpec((tm, tn), lambda i, j: (i, j)),
                   pl.BlockSpec((tm, 2 * LANE), lambda i, j: (i, 0))],
        out_shape=[jax.ShapeDtypeStruct((t, P_WIDTH), BF16),
                   jax.ShapeDtypeStruct((t, 2 * LANE), F32)],
        scratch_shapes=[pltpu.VMEM((tm, d), BF16)],
        compiler_params=_params(("parallel", "arbitrary")),
        name="in_proj",
    )(xa, norm_w.reshape(1, d), scale, shift, w_p, w_side)


def _conv_kernel(u_ref, prev_ref, next_ref, w_ref, b_ref, s_ref, o_ref, *,
                 tm, tiles_per_batch, lat_tiles):
    ib = pl.program_id(0) % tiles_per_batch
    first = jnp.logical_or(ib == 0, ib == lat_tiles)
    last = jnp.logical_or(ib == lat_tiles - 1, ib == tiles_per_batch - 1)
    half = SSM_CONV // 2
    row = lax.broadcasted_iota(jnp.int32, (8, 1), 0)
    cw = 2 * LANE
    for c0 in range(0, u_ref.shape[1], cw):
        sl = slice(c0, c0 + cw)
        u = u_ref[:, sl]
        acc = b_ref[:, sl] + w_ref[half:half + 1, sl] * u.astype(F32)
        for idx, k in enumerate(k for k in range(SSM_CONV) if k != half):
            acc = acc + w_ref[k:k + 1, sl] * jnp.dot(s_ref[idx], u, preferred_element_type=F32)
        prev = jnp.where(first, 0.0, prev_ref[:, sl].astype(F32))
        nxt = jnp.where(last, 0.0, next_ref[:, sl].astype(F32))
        top = jnp.zeros_like(prev)
        bot = jnp.zeros_like(nxt)
        for k in range(half):
            reach = half - k
            top = top + w_ref[k:k + 1, sl] * jnp.where(row < reach, pltpu.roll(prev, reach, axis=0), 0.0)
            kk = SSM_CONV - 1 - k
            bot = bot + w_ref[kk:kk + 1, sl] * jnp.where(row >= 8 - reach,
                                                         pltpu.roll(nxt, 8 - reach, axis=0), 0.0)
        y = jnp.concatenate([acc[0:8] + top, acc[8:tm - 8], acc[tm - 8:tm] + bot], axis=0)
        o_ref[:, sl] = _silu(y).astype(o_ref.dtype)


def _conv_silu(p, conv_w, conv_b, n_lat, seq):
    t = p.shape[0]
    tm = 256
    tc = 1024
    tpb = seq // tm
    n_row8 = t // 8
    col0 = P_XBC // tc
    kern = functools.partial(_conv_kernel, tm=tm, tiles_per_batch=tpb, lat_tiles=n_lat // tm)
    half = SSM_CONV // 2
    shifts = np.stack([np.eye(tm, k=k - half, dtype=np.float32) for k in range(SSM_CONV) if k != half])
    return pl.pallas_call(
        kern,
        grid=(t // tm, SSM_CONV_DIM // tc),
        in_specs=[pl.BlockSpec((tm, tc), lambda i, j: (i, col0 + j)),
                  pl.BlockSpec((8, tc), lambda i, j: (jnp.maximum(i * (tm // 8) - 1, 0), col0 + j)),
                  pl.BlockSpec((8, tc), lambda i, j: (jnp.minimum((i + 1) * (tm // 8), n_row8 - 1),
                                                      col0 + j)),
                  pl.BlockSpec((8, tc), lambda i, j: (0, j)),
                  pl.BlockSpec((1, tc), lambda i, j: (0, j)),
                  pl.BlockSpec((SSM_CONV - 1, tm, tm), lambda i, j: (0, 0, 0))],
        out_specs=pl.BlockSpec((tm, tc), lambda i, j: (i, j)),
        out_shape=jax.ShapeDtypeStruct((t, SSM_CONV_DIM), BF16),
        compiler_params=_params(("parallel", "parallel")),
        name="ssm_conv",
    )(p, p, p, jnp.pad(conv_w, ((0, 8 - SSM_CONV), (0, 0))), conv_b.reshape(1, -1),
      jnp.asarray(shifts, BF16))


def _split_dot(a_bf16, v):
    v1 = v.astype(BF16)
    r1 = v - v1.astype(F32)
    v2 = r1.astype(BF16)
    v3 = (r1 - v2.astype(F32)).astype(BF16)
    return jnp.dot(jnp.concatenate([a_bf16, a_bf16, a_bf16], axis=1),
                   jnp.concatenate([v1, v2, v3], axis=0), preferred_element_type=F32)


def _ssd_kernel(uf_ref, ub_ref, dtf_ref, dtb_ref, alog_ref, bias_ref, eh_ref, yf_ref, yb_ref,
                stf_ref, stb_ref):
    @pl.when(pl.program_id(1) == 0)
    def _():
        stf_ref[...] = jnp.zeros_like(stf_ref)
        stb_ref[...] = jnp.zeros_like(stb_ref)

    _ssd_chunk(True, uf_ref, dtf_ref, alog_ref[0], bias_ref[0], eh_ref, yf_ref, stf_ref)
    _ssd_chunk(False, ub_ref, dtb_ref, alog_ref[1], bias_ref[1], eh_ref, yb_ref, stb_ref)


def _ssd_chunk(fwd, u_ref, dtr_ref, a_log, dt_bias, eh_ref, y_ref, st_ref):
    cl = SSM_CHUNK
    raw = dtr_ref[...] + dt_bias
    dt = jnp.maximum(raw, 0.0) + jnp.log(1.0 + jnp.exp(-jnp.abs(raw)))
    da = dt * (-jnp.exp(a_log))
    r = lax.broadcasted_iota(jnp.int32, (cl, cl), 0)
    c = lax.broadcasted_iota(jnp.int32, (cl, cl), 1)
    tri = (r >= c) if fwd else (r <= c)
    acs = _split_dot(jnp.where(tri, 1.0, 0.0).astype(BF16), da)
    total = acs[cl - 1:cl, :] if fwd else acs[0:1, :]
    e_acs = jnp.exp(acs)
    w_end = dt * jnp.exp(total - acs)
    acs_t = acs.T
    dt_t = dt.T

    def hi_lo(v):
        hi = v.astype(BF16)
        return jnp.concatenate([hi, (v - hi.astype(F32)).astype(BF16)], axis=1)

    ex = jnp.dot(jnp.concatenate([hi_lo(w_end), hi_lo(e_acs)], axis=0), eh_ref[...],
                 preferred_element_type=F32)
    w_exp = ex[0:cl]
    e_exp = ex[cl:2 * cl]
    dec = e_exp[cl - 1:cl, :] if fwd else e_exp[0:1, :]
    xw =(u_ref[:, 0:SSM_INNER].astype(F32) * w_exp).astype(BF16)
    lane = lax.broadcasted_iota(jnp.int32, (cl, LANE), 1)
    gw = SSM_INNER // SSM_GROUPS
    hpg = SSM_HEADS // SSM_GROUPS
    for g in range(SSM_GROUPS):
        b_g = u_ref[:, SSM_INNER + g * SSM_STATE:SSM_INNER + (g + 1) * SSM_STATE]
        c_off = SSM_INNER + SSM_GROUPS * SSM_STATE
        c_g = u_ref[:, c_off + g * SSM_STATE:c_off + (g + 1) * SSM_STATE]
        cb = lax.dot_general(c_g, b_g, (((1,), (1,)), ((), ())), preferred_element_type=F32)
        st_g = st_ref[:, g * gw:(g + 1) * gw]
        y_off = jnp.dot(c_g, st_g.astype(BF16), preferred_element_type=F32)
        ys = []
        for k in range(hpg // 2):
            ms = []
            for h in (g * hpg + 2 * k, g * hpg + 2 * k + 1):
                seg = acs[:, h:h + 1] - acs_t[h:h + 1, :]
                dec_h = jnp.exp(jnp.where(tri, seg, NEG_BIG))
                ms.append((dec_h * cb * dt_t[h:h + 1, :]).astype(BF16))
            x_p = u_ref[:, g * gw + k * LANE:g * gw + (k + 1) * LANE]
            zero = jnp.zeros_like(x_p)
            rhs = jnp.concatenate([jnp.where(lane < SSM_HEAD_DIM, x_p, zero),
                                   jnp.where(lane >= SSM_HEAD_DIM, x_p, zero)], axis=0)
            ys.append(jnp.dot(jnp.concatenate(ms, axis=1), rhs, preferred_element_type=F32))
        y_g = jnp.concatenate(ys, axis=1) + y_off * e_exp[:, g * gw:(g + 1) * gw]
        y_ref[:, g * gw:(g + 1) * gw] = y_g.astype(y_ref.dtype)
        upd = lax.dot_general(b_g, xw[:, g * gw:(g + 1) * gw], (((0,), (0,)), ((), ())),
                              preferred_element_type=F32)
        st_ref[:, g * gw:(g + 1) * gw] = st_g * dec[:, g * gw:(g + 1) * gw] + upd


def _ssd(u, dtr, a_log, dt_bias, batch, n_lat, seq):
    cl = SSM_CHUNK
    nch, nlat, nctx = seq // cl, n_lat // cl, (seq - n_lat) // cl

    def fchunk(c):
        return jnp.where(c < nctx, nlat + c, c - nctx)

    def bchunk(c):
        return jnp.where(c < nctx, nlat + nctx - 1 - c, nlat - 1 - (c - nctx))

    def pad_heads(v):
        return jnp.pad(v, ((0, 0), (0, LANE - SSM_HEADS))).reshape(2, 1, LANE)

    eh = np.zeros((2 * LANE, SSM_INNER), np.float32)
    for h in range(SSM_HEADS):
        eh[h, h * SSM_HEAD_DIM:(h + 1) * SSM_HEAD_DIM] = 1.0
        eh[LANE + h, h * SSM_HEAD_DIM:(h + 1) * SSM_HEAD_DIM] = 1.0
    return pl.pallas_call(
        _ssd_kernel,
        grid=(batch, nch),
        in_specs=[pl.BlockSpec((cl, SSM_CONV_DIM), lambda b, c: (b * nch + fchunk(c), 0)),
                  pl.BlockSpec((cl, SSM_CONV_DIM), lambda b, c: (b * nch + bchunk(c), 0)),
                  pl.BlockSpec((cl, LANE), lambda b, c: (b * nch + fchunk(c), 0)),
                  pl.BlockSpec((cl, LANE), lambda b, c: (b * nch + bchunk(c), 1)),
                  pl.BlockSpec((2, 1, LANE), lambda b, c: (0, 0, 0)),
                  pl.BlockSpec((2, 1, LANE), lambda b, c: (0, 0, 0)),
                  pl.BlockSpec((2 * LANE, SSM_INNER), lambda b, c: (0, 0))],
        out_specs=[pl.BlockSpec((cl, SSM_INNER), lambda b, c: (b * nch + fchunk(c), 0)),
                   pl.BlockSpec((cl, SSM_INNER), lambda b, c: (b * nch + bchunk(c), 0))],
        out_shape=[jax.ShapeDtypeStruct((u.shape[0], SSM_INNER), BF16)] * 2,
        scratch_shapes=[pltpu.VMEM((SSM_STATE, SSM_INNER), F32)] * 2,
        compiler_params=_params(("parallel", "arbitrary")),
        name="ssd_scan",
    )(u, u, dtr, dtr, pad_heads(a_log), pad_heads(dt_bias), jnp.asarray(eh, BF16))


def _ssm_out_kernel(yf_ref, yb_ref, x_ref, z_ref, skip_ref, nw_ref, o_ref):
    z = z_ref[...].astype(F32)
    y = (yf_ref[...] + yb_ref[...]).astype(F32)
    g = (y + skip_ref[...] * x_ref[...].astype(F32)) * _silu(z)
    gw = SSM_INNER // SSM_GROUPS
    for k in range(SSM_GROUPS):
        gk = g[:, k * gw:(k + 1) * gw]
        gk = gk * lax.rsqrt(jnp.mean(gk * gk, axis=-1, keepdims=True) + EPS)
        o_ref[:, k * gw:(k + 1) * gw] = (gk * nw_ref[:, k * gw:(k + 1) * gw]).astype(o_ref.dtype)


def _ssm_out(yf, yb, u, p, d_skip, ssm_norm, seq):
    t = u.shape[0]
    tm = _pick(seq, (384, 256, 128))
    w = SSM_INNER
    return pl.pallas_call(
        _ssm_out_kernel,
        grid=(t // tm,),
        in_specs=[pl.BlockSpec((tm, w), lambda i: (i, 0)),
                  pl.BlockSpec((tm, w), lambda i: (i, 0)),
                  pl.BlockSpec((tm, w), lambda i: (i, 0)),
                  pl.BlockSpec((tm, w), lambda i: (i, P_Z // w)),
                  pl.BlockSpec((1, w), lambda i: (0, 0)),
                  pl.BlockSpec((1, w), lambda i: (0, 0))],
        out_specs=pl.BlockSpec((tm, w), lambda i: (i, 0)),
        out_shape=jax.ShapeDtypeStruct((t, w), BF16),
        compiler_params=_params(("parallel",)),
        name="ssm_gated_norm",
    )(yf, yb, u, p, jnp.repeat(d_skip, SSM_HEAD_DIM).reshape(1, w), ssm_norm.reshape(1, w))


def _head_norm_rope_kernel(x_ref, nw_ref, cos_ref, sin_ref, o_ref, *, n_heads):
    for h in range(n_heads):
        sl = slice(h * LANE, (h + 1) * LANE)
        x = x_ref[:, sl].astype(F32)
        y = x * lax.rsqrt(jnp.mean(x * x, axis=-1, keepdims=True) + EPS) * nw_ref[...]
        o_ref[:, sl] = _rope(y, cos_ref[...], sin_ref[...]).astype(o_ref.dtype)


def _head_norm_rope(p, col0, n_heads, norm_w, cos, sin, seq):
    t = p.shape[0]
    tm = _pick(seq, (768, 512, 256))
    tpb = seq // tm
    w = n_heads * LANE
    return pl.pallas_call(
        functools.partial(_head_norm_rope_kernel, n_heads=n_heads),
        grid=(t // tm,),
        in_specs=[pl.BlockSpec((tm, w), lambda i: (i, col0 // w)),
                  pl.BlockSpec((1, LANE), lambda i: (0, 0)),
                  pl.BlockSpec((tm, LANE), lambda i: (i % tpb, 0)),
                  pl.BlockSpec((tm, LANE), lambda i: (i % tpb, 0))],
        out_specs=pl.BlockSpec((tm, w), lambda i: (i, 0)),
        out_shape=jax.ShapeDtypeStruct((t, w), BF16),
        compiler_params=_params(("parallel",)),
        name="head_norm_rope",
    )(p, norm_w.reshape(1, LANE), cos, sin)


def _mla_q_kernel(x_ref, nw_ref, w_ref, cos_ref, sin_ref, o_ref):
    x = x_ref[:, 0:MLA_Q_LORA].astype(F32)
    h = (x * lax.rsqrt(jnp.mean(x * x, axis=-1, keepdims=True) + EPS) * nw_ref[...]).astype(BF16)
    for hd in range(MLA_HEADS):
        c0 = hd * MLA_QK_PAD
        res = jnp.dot(h, w_ref[:, c0:c0 + MLA_QK_PAD], preferred_element_type=F32)
        o_ref[:, c0:c0 + LANE] = res[:, 0:LANE].astype(o_ref.dtype)
        o_ref[:, c0 + LANE:c0 + MLA_QK_PAD] = _rope(res[:, LANE:], cos_ref[...],
                                                    sin_ref[...]).astype(o_ref.dtype)


def _mla_q(p, q_norm, w_uq, cos, sin, seq):
    t = p.shape[0]
    tm = _pick(seq, (768, 512, 256))
    tpb = seq // tm
    kdim = MQA_PAD
    wo = MLA_HEADS * MLA_QK_PAD
    return pl.pallas_call(
        _mla_q_kernel,
        grid=(t // tm,),
        in_specs=[pl.BlockSpec((tm, kdim), lambda i: (i, P_MQA // kdim)),
                  pl.BlockSpec((1, MLA_Q_LORA), lambda i: (0, 0)),
                  pl.BlockSpec((MLA_Q_LORA, wo), lambda i: (0, 0)),
                  pl.BlockSpec((tm, LANE), lambda i: (i % tpb, 0)),
                  pl.BlockSpec((tm, LANE), lambda i: (i % tpb, 0))],
        out_specs=pl.BlockSpec((tm, wo), lambda i: (i, 0)),
        out_shape=jax.ShapeDtypeStruct((t, wo), BF16),
        compiler_params=_params(("parallel",)),
        name="mla_q_up",
    )(p, q_norm.reshape(1, MLA_Q_LORA), w_uq, cos, sin)


def _mla_kv_kernel(x_ref, kr_ref, nw_ref, w_ref, cos_ref, sin_ref, k_ref, v_ref):
    x = x_ref[...].astype(F32)
    h = (x * lax.rsqrt(jnp.mean(x * x, axis=-1, keepdims=True) + EPS) * nw_ref[...]).astype(BF16)
    k_rope = _rope(kr_ref[...].astype(F32), cos_ref[...], sin_ref[...]).astype(k_ref.dtype)
    for hd in range(MLA_HEADS):
        c0 = hd * MLA_QK_PAD
        res = jnp.dot(h, w_ref[:, c0:c0 + 2 * LANE], preferred_element_type=F32)
        k_ref[:, c0:c0 + LANE] = res[:, 0:LANE].astype(k_ref.dtype)
        k_ref[:, c0 + LANE:c0 + MLA_QK_PAD] = k_rope
        v_ref[:, hd * LANE:(hd + 1) * LANE] = res[:, LANE:].astype(v_ref.dtype)


def _mla_kv(p, kv_norm, w_ukv, cos, sin, seq):
    t = p.shape[0]
    tm = _pick(seq, (768, 512, 256))
    tpb = seq // tm
    kdim = MLA_KV_LORA
    wk = MLA_HEADS * MLA_QK_PAD
    return pl.pallas_call(
        _mla_kv_kernel,
        grid=(t // tm,),
        in_specs=[pl.BlockSpec((tm, kdim), lambda i: (i, P_MKVA // kdim)),
                  pl.BlockSpec((tm, LANE), lambda i: (i, P_MKR // LANE)),
                  pl.BlockSpec((1, kdim), lambda i: (0, 0)),
                  pl.BlockSpec((kdim, wk), lambda i: (0, 0)),
                  pl.BlockSpec((tm, LANE), lambda i: (i % tpb, 0)),
                  pl.BlockSpec((tm, LANE), lambda i: (i % tpb, 0))],
        out_specs=[pl.BlockSpec((tm, wk), lambda i: (i, 0)),
                   pl.BlockSpec((tm, MLA_HEADS * LANE), lambda i: (i, 0))],
        out_shape=[jax.ShapeDtypeStruct((t, wk), BF16),
                   jax.ShapeDtypeStruct((t, MLA_HEADS * LANE), BF16)],
        compiler_params=_params(("parallel",)),
        name="mla_kv_up",
    )(p, p, kv_norm.reshape(1, kdim), w_ukv, cos, sin)


def _flash_kernel(q_ref, k_ref, v_ref, g_ref, *rest, r, tq, dq, tk, n_keys, scale):
    o_ref = rest[-1]
    q = jnp.concatenate([q_ref[:, j * dq:(j + 1) * dq] for j in range(r)], axis=0)
    q = (q.astype(F32) * (scale * math.log2(math.e))).astype(BF16)
    m_rows = r * tq
    m = jnp.full((m_rows, 1), NEG_BIG, F32)
    l = jnp.zeros((m_rows, LANE), F32)
    acc = jnp.zeros((m_rows, LANE), F32)
    for c in range(n_keys // tk):
        k_c = k_ref[c * tk:(c + 1) * tk, :]
        v_c = v_ref[c * tk:(c + 1) * tk, :]
        s = lax.dot_general(q, k_c, (((1,), (1,)), ((), ())), preferred_element_type=F32)
        m_new = jnp.maximum(m, jnp.max(s, axis=-1, keepdims=True))
        alpha = jnp.exp2(m - m_new)
        pr = jnp.exp2(s - m_new)
        part = pr[:, 0:LANE]
        for j in range(1, tk // LANE):
            part = part + pr[:, j * LANE:(j + 1) * LANE]
        l = alpha * l + part
        acc = alpha * acc + jnp.dot(pr.astype(BF16), v_c, preferred_element_type=F32)
        m = m_new
    o = acc / jnp.sum(l, axis=-1, keepdims=True)
    for j in range(r):
        gate = _silu(g_ref[:, j * LANE:(j + 1) * LANE].astype(F32))
        o_ref[:, j * LANE:(j + 1) * LANE] = (o[j * tq:(j + 1) * tq] * gate).astype(o_ref.dtype)


def _flash(q, k, v, gates, *, q_col0, k_col0, v_col0, g_col0, v_stride, n_kv_heads, r, dq, scale,
           batch, n_lat, seq, ctx_only, tq, into=None):
    n_ctx = seq - n_lat
    tk = FLASH_TK
    if ctx_only:
        tq = n_ctx
        q_blk0, n_q, kv_rows, kv_blk0 = n_lat // tq, 1, n_ctx, n_lat // n_ctx
    else:
        q_blk0, n_q, kv_rows, kv_blk0 = 0, n_lat // tq, seq, 0
    assert kv_rows % tk == 0
    kern = functools.partial(_flash_kernel, r=r, tq=tq, dq=dq, tk=tk, n_keys=kv_rows, scale=scale)
    w_out = n_kv_heads * r * LANE
    args = [a.reshape(batch, seq, a.shape[-1]) for a in (q, k, v, gates)]
    in_specs = [pl.BlockSpec((None, tq, r * dq), lambda b, g, i: (b, q_blk0 + i, q_col0 // (r * dq) + g)),
                pl.BlockSpec((None, kv_rows, dq), lambda b, g, i: (b, kv_blk0, k_col0 // dq + g)),
                pl.BlockSpec((None, kv_rows, LANE),
                             lambda b, g, i: (b, kv_blk0, v_col0 // LANE + v_stride * g)),
                pl.BlockSpec((None, tq, r * LANE),
                             lambda b, g, i: (b, q_blk0 + i, g_col0 // (r * LANE) + g))]
    aliases = {}
    if into is not None:
        args.append(into.reshape(batch, seq, w_out))
        in_specs.append(pl.BlockSpec(memory_space=pl.ANY))
        aliases = {len(args) - 1: 0}
    out = pl.pallas_call(
        kern,
        grid=(batch, n_kv_heads, n_q),
        in_specs=in_specs,
        out_specs=pl.BlockSpec((None, tq, r * LANE), lambda b, g, i: (b, q_blk0 + i, g)),
        out_shape=jax.ShapeDtypeStruct((batch, seq, w_out), BF16),
        input_output_aliases=aliases,
        compiler_params=_params(("parallel", "parallel", "parallel")),
        name="flash_ctx" if ctx_only else "flash_latent",
    )(*args)
    return out.reshape(batch * seq, w_out)


def _attention(q, k, v, gates, tq, **kw):
    lat = _flash(q, k, v, gates, ctx_only=False, tq=tq, **kw)
    return _flash(q, k, v, gates, ctx_only=True, tq=tq, into=lat, **kw)


def _natten_kernel(q_ref, k0_ref, k1_ref, k2_ref, kc_ref, v0_ref, v1_ref, v2_ref, vc_ref, g_ref,
                   bias_ref, o_ref, *, scale, n_heads):
    n_loc = bias_ref.shape[-1]
    for h in range(n_heads):
        sl = slice(h * LANE, (h + 1) * LANE)
        q = (q_ref[:, sl].astype(F32) * (scale * math.log2(math.e))).astype(BF16)
        k = jnp.concatenate([k0_ref[:, sl], k1_ref[:, sl], k2_ref[:, sl], kc_ref[:, sl]], axis=0)
        v = jnp.concatenate([v0_ref[:, sl], v1_ref[:, sl], v2_ref[:, sl], vc_ref[:, sl]], axis=0)
        s = lax.dot_general(q, k, (((1,), (1,)), ((), ())), preferred_element_type=F32)
        s = jnp.concatenate([s[:, :n_loc] + bias_ref[h], s[:, n_loc:]], axis=1)
        pr = jnp.exp2(s - jnp.max(s, axis=-1, keepdims=True)).astype(BF16)
        ov = jnp.dot(pr, jnp.concatenate([v, jnp.ones_like(v)], axis=1), preferred_element_type=F32)
        o = ov[:, 0:LANE] / ov[:, LANE:]
        o_ref[:, sl] = (o * _silu(g_ref[:, sl].astype(F32))).astype(o_ref.dtype)


def _na_bias_table(rpb):
    qx = np.arange(GRID_W)
    c0 = np.clip(qx - NA_WIN_W // 2, 0, GRID_W - NA_WIN_W)
    col_ok = (qx[None, :] >= c0[:, None]) & (qx[None, :] < c0[:, None] + NA_WIN_W)
    dx = qx[None, :] - qx[:, None] + NA_WIN_W - 1
    pick = np.zeros((2 * NA_WIN_W - 1, GRID_W * GRID_W), np.float32)
    qi, ki = np.nonzero(col_ok)
    pick[dx[qi, ki], qi * GRID_W + ki] = 1.0
    by_dx = jnp.einsum("...d,dn->...n", rpb, pick, precision=lax.Precision.HIGHEST)
    qy, ky = np.arange(NA_QROWS), np.arange(NA_KROWS)
    dys, oks = [], []
    for q_off, first_key in ((0, None), (NA_WIN_H // 2, "q"), (NA_WIN_H, NA_KROWS - NA_WIN_H)):
        r0 = qy if first_key == "q" else np.full_like(qy, 0 if first_key is None else first_key)
        oks.append((ky[None, :] >= r0[:, None]) & (ky[None, :] < r0[:, None] + NA_WIN_H))
        dys.append(np.clip(ky[None, :] - (qy[:, None] + q_off) + NA_WIN_H - 1, 0, 2 * NA_WIN_H - 2))
    rows = jnp.take(by_dx, np.stack(dys).reshape(-1), axis=-2)
    lead = rows.shape[:-2]
    nl = len(lead)
    rows = rows.reshape(lead + (3, NA_QROWS, NA_KROWS, GRID_W, GRID_W))
    rows = rows.transpose(tuple(range(nl)) + (nl, nl + 1, nl + 3, nl + 2, nl + 4))
    ok = np.stack(oks)[:, :, None, :, None] & col_ok[None, None, :, None, :]
    tab = jnp.where(jnp.asarray(ok), rows * math.log2(math.e), NEG_BIG)
    return tab.reshape(lead + (3, NA_QROWS * GRID_W, NA_KROWS * GRID_W))


def _natten(p, bias, batch, n_lat, seq):
    n_ctx = seq - n_lat
    tq = NA_QROWS * GRID_W
    n_blk = n_lat // tq
    assert n_blk >= 3 and n_lat % tq == 0 and n_ctx % tq == 0
    rb = seq // tq
    cb = seq // n_ctx
    nk = NA_KROWS * GRID_W
    hps = NA_HEADS_PER_STEP
    hw = hps * LANE

    def kblk(i):
        return jnp.clip(i - 1, 0, n_blk - 3)

    def kspec(col0, j):
        return pl.BlockSpec((tq, hw), lambda b, h, i: (b * rb + kblk(i) + j, col0 // hw + h))

    def cspec(col0):
        return pl.BlockSpec((n_ctx, hw), lambda b, h, i: (b * cb + n_lat // n_ctx, col0 // hw + h))

    def qspec(col0):
        return pl.BlockSpec((tq, hw), lambda b, h, i: (b * rb + i, col0 // hw + h))

    def btype(i):
        return jnp.where(i == 0, 0, jnp.where(i == n_blk - 1, 2, 1))

    return pl.pallas_call(
        functools.partial(_natten_kernel, scale=HEAD_DIM ** -0.5, n_heads=hps),
        grid=(batch, NA_HEADS // hps, n_blk),
        in_specs=[qspec(P_NQ), kspec(P_NK, 0), kspec(P_NK, 1), kspec(P_NK, 2), cspec(P_NK),
                  kspec(P_NV, 0), kspec(P_NV, 1), kspec(P_NV, 2), cspec(P_NV), qspec(P_NG),
                  pl.BlockSpec((hps, None, tq, nk), lambda b, h, i: (h, btype(i), 0, 0))],
        out_specs=pl.BlockSpec((tq, hw), lambda b, h, i: (b * rb + i, h)),
        out_shape=jax.ShapeDtypeStruct((p.shape[0], NA_HEADS * LANE), BF16),
        compiler_params=_params(("parallel", "parallel", "parallel")),
        name="natten_latent",
    )(p, p, p, p, p, p, p, p, p, p, bias)


def _merge_kernel(o0_ref, o1_ref, o2_ref, o3_ref, w0_ref, w1_ref, w2_ref, w3_ref,
                  m0_ref, m1_ref, m2_ref, m3_ref, y_ref):
    acc = None
    for o_ref, w_ref, m_ref in ((o0_ref, w0_ref, m0_ref), (o1_ref, w1_ref, m1_ref),
                                (o2_ref, w2_ref, m2_ref), (o3_ref, w3_ref, m3_ref)):
        gate = _sigmoid(m_ref[...].astype(F32))
        term = gate * jnp.dot(o_ref[...], w_ref[...], preferred_element_type=F32)
        acc = term if acc is None else acc + term
    y_ref[...] = acc.astype(y_ref.dtype)


def _merge(outs, weights, p, seq):
    t = p.shape[0]
    d = weights[0].shape[1]
    tm = _pick(seq, (768, 512, 256))
    tn = 512
    o_specs = [pl.BlockSpec((tm, o.shape[1]), lambda i, j: (i, 0)) for o in outs]
    w_specs = [pl.BlockSpec((w.shape[0], tn), lambda i, j: (0, j)) for w in weights]
    m_specs = [pl.BlockSpec((tm, tn), lambda i, j, b=b: (i, (P_MIX + b * d) // tn + j))
               for b in range(N_BRANCH)]
    return pl.pallas_call(
        _merge_kernel,
        grid=(t // tm, d // tn),
        in_specs=o_specs + w_specs + m_specs,
        out_specs=pl.BlockSpec((tm, tn), lambda i, j: (i, j)),
        out_shape=jax.ShapeDtypeStruct((t, d), BF16),
        compiler_params=_params(("parallel", "parallel")),
        name="branch_merge",
    )(*outs, *weights, p, p, p, p)


def _out_kernel(y_ref, w_ref, x_ref, nw_ref, gate_ref, o_ref, *, tm, tiles_per_batch, n_lat):
    z = jnp.dot(y_ref[...], w_ref[...], preferred_element_type=F32)
    zn = z * lax.rsqrt(jnp.mean(z * z, axis=-1, keepdims=True) + EPS) * nw_ref[...]
    if tiles_per_batch is None:
        gate = gate_ref[0:1, :]
    else:
        is_ctx = _is_ctx_rows(pl.program_id(0), tm, tiles_per_batch, n_lat)
        gate = jnp.where(is_ctx, gate_ref[1:2, :], gate_ref[0:1, :])
    o_ref[...] = x_ref[...] + gate * zn


def _out_projection_latent(y, w_out, xa, norm_w, gate, batch, n_lat, seq):
    d = xa.shape[1]
    tm = _pick(n_lat, (512, 256, 128))
    kern = functools.partial(_out_kernel, tm=tm, tiles_per_batch=None, n_lat=n_lat)
    return pl.pallas_call(
        kern,
        grid=(batch, n_lat // tm),
        in_specs=[pl.BlockSpec((None, tm, d), lambda b, i: (b, i, 0)),
                  pl.BlockSpec((d, d), lambda b, i: (0, 0)),
                  pl.BlockSpec((None, tm, d), lambda b, i: (b, i, 0)),
                  pl.BlockSpec((1, d), lambda b, i: (0, 0)),
                  pl.BlockSpec((None, 2, d), lambda b, i: (b, 0, 0))],
        out_specs=pl.BlockSpec((None, tm, d), lambda b, i: (b, i, 0)),
        out_shape=jax.ShapeDtypeStruct((batch, n_lat, d), F32),
        compiler_params=_params(("parallel", "parallel")),
        name="out_proj_last",
    )(y.reshape(batch, seq, d), w_out, xa.reshape(batch, seq, d), norm_w.reshape(1, d), gate)


def _out_projection(y, w_out, xa, norm_w, gate, n_lat, seq):
    t, d = xa.shape
    tm = _pick(seq, (768, 384, 256, 128))
    tpb = seq // tm
    kern = functools.partial(_out_kernel, tm=tm, tiles_per_batch=tpb, n_lat=n_lat)
    return pl.pallas_call(
        kern,
        grid=(t // tm,),
        in_specs=[pl.BlockSpec((tm, d), lambda i: (i, 0)),
                  pl.BlockSpec((d, d), lambda i: (0, 0)),
                  pl.BlockSpec((tm, d), lambda i: (i, 0)),
                  pl.BlockSpec((1, d), lambda i: (0, 0)),
                  pl.BlockSpec((None, 2, d), lambda i: (i // tpb, 0, 0))],
        out_specs=pl.BlockSpec((tm, d), lambda i: (i, 0)),
        out_shape=jax.ShapeDtypeStruct((t, d), F32),
        compiler_params=_params(("parallel",)),
        name="out_proj",
    )(y, w_out, xa, norm_w.reshape(1, d), gate)


def _pairs_apart(w, n_heads, dim):
    lead = w.shape[:-1]
    return w.reshape(lead + (n_heads, dim // 2, 2)).swapaxes(-1, -2).reshape(lead + (n_heads * dim,))


def _rope_tile(w):
    lead = w.shape[:-1]
    pr = w.reshape(lead + (MLA_ROPE // 2, 2))
    zero = jnp.zeros(lead + (MLA_ROPE // 2,), w.dtype)
    return jnp.concatenate([pr[..., 0], zero, pr[..., 1], zero], axis=-1)


def _layout_w_in(w):
    o = _OFF
    k = w.shape[0]

    def seg(name, width):
        return w[:, o[name]:o[name] + width]

    cols = [seg("mix", 8192), seg("z", 2048), seg("xbc", SSM_CONV_DIM),
            _pairs_apart(seg("gq", 1024), GQA_HEADS, HEAD_DIM), seg("gg", 1024),
            seg("nq", 1024), seg("nk", 1024), seg("nv", 1024), seg("ng", 1024), seg("mg", 1024),
            seg("mqa", MLA_Q_LORA), _rope_tile(seg("mkr", MLA_ROPE)),
            jnp.zeros((k, MQA_PAD - MLA_Q_LORA - LANE), w.dtype),
            _pairs_apart(seg("gk", 512), GQA_KV_HEADS, HEAD_DIM), seg("gv", 512),
            seg("mkva", MLA_KV_LORA)]
    main = jnp.concatenate(cols, axis=1).astype(BF16)
    assert main.shape[1] == P_WIDTH
    dtr = seg("dtr", 2 * SSM_HEADS)
    zero = jnp.zeros((k, LANE - SSM_HEADS), w.dtype)
    side = jnp.concatenate([dtr[:, :SSM_HEADS], zero, dtr[:, SSM_HEADS:], zero], axis=1).astype(BF16)
    return main, side


def _layout_w_uq(w_uq):
    k = w_uq.shape[0]
    w = w_uq.reshape(k, MLA_HEADS, MLA_NOPE + MLA_ROPE)
    w = jnp.concatenate([w[..., :MLA_NOPE], _rope_tile(w[..., MLA_NOPE:])], axis=-1)
    return w.reshape(k, MLA_HEADS * MLA_QK_PAD).astype(BF16)


def _rope_tables(n_lat, n_ctx, dim):
    t = np.arange(n_lat)
    quarter = dim // 4
    freqs = ROPE_THETA ** (-jnp.arange(quarter, dtype=F32) / quarter)
    row = jnp.asarray(t // GRID_W, F32)
    col = jnp.asarray(t % GRID_W, F32)
    ang = jnp.concatenate([row[:, None] * freqs, col[:, None] * freqs], axis=-1)
    cos, sin = jnp.cos(ang), jnp.sin(ang)
    pad = 64 - dim // 2
    one, zero = jnp.ones((n_lat, pad), F32), jnp.zeros((n_lat, pad), F32)
    cos_t = jnp.concatenate([cos, one, cos, one], axis=-1)
    sin_t = jnp.concatenate([-sin, zero, sin, zero], axis=-1)
    cos_t = jnp.concatenate([cos_t, jnp.ones((n_ctx, LANE), F32)], axis=0)
    sin_t = jnp.concatenate([sin_t, jnp.zeros((n_ctx, LANE), F32)], axis=0)
    return cos_t, sin_t


def _layer(xa, mod, rope_g, rope_m, lp, batch, n_lat, seq, last):
    d = xa.shape[1]

    def per_row(v):
        return jnp.stack([v[:batch], jnp.broadcast_to(v[batch:batch + 1], (batch, d))], axis=1)

    shift, scale, gate = (per_row(mod[:, k * d:(k + 1) * d]) for k in range(3))
    w_main, w_side = _layout_w_in(lp["w_in"])
    p, dtr = _in_projection(xa, lp["norm_pre"], scale, shift, w_main, w_side, n_lat, seq)

    u = _conv_silu(p, lp["conv_w"], lp["conv_b"], n_lat, seq)
    yf, yb = _ssd(u, dtr, lp["a_log"], lp["dt_bias"], batch, n_lat, seq)
    o_ssm = _ssm_out(yf, yb, u, p, lp["d_skip"], lp["ssm_norm"], seq)

    common = dict(batch=batch, n_lat=n_lat, seq=seq)
    qg = _head_norm_rope(p, P_GQ, GQA_HEADS, _pairs_apart(lp["gqa_q_norm"], 1, HEAD_DIM), *rope_g, seq)
    kg = _head_norm_rope(p, P_GK, GQA_KV_HEADS, _pairs_apart(lp["gqa_k_norm"], 1, HEAD_DIM), *rope_g, seq)
    o_gqa = _attention(
        qg, kg, p, p, GQA_TQ, q_col0=0, k_col0=0, v_col0=P_GV, g_col0=P_GG, v_stride=1,
        n_kv_heads=GQA_KV_HEADS, r=GQA_HEADS // GQA_KV_HEADS, dq=HEAD_DIM, scale=HEAD_DIM ** -0.5,
        **common)

    na_lat = _natten(p, lp["na_bias"], **common)
    o_na = _flash(p, p, p, p, q_col0=P_NQ, k_col0=P_NK, v_col0=P_NV, g_col0=P_NG, v_stride=1,
                  n_kv_heads=NA_HEADS, r=1, dq=HEAD_DIM, scale=HEAD_DIM ** -0.5, ctx_only=True,
                  tq=256, into=na_lat, **common)

    qm = _mla_q(p, lp["mla_q_norm"], _layout_w_uq(lp["w_uq"]), *rope_m, seq)
    km, vm = _mla_kv(p, lp["mla_kv_norm"], lp["w_ukv"].astype(BF16), *rope_m, seq)
    o_mla = _attention(
        qm, km, vm, p, MLA_TQ, q_col0=0, k_col0=0, v_col0=0, g_col0=P_MG, v_stride=1,
        n_kv_heads=MLA_HEADS, r=1, dq=MLA_QK_PAD, scale=(MLA_NOPE + MLA_ROPE) ** -0.5,
        **common)

    weights = [lp[n].astype(BF16) for n in ("w_o_ssm", "w_o_gqa", "w_o_na", "w_o_mla")]
    ymix = _merge([o_ssm, o_gqa, o_na, o_mla], weights, p, seq)
    w_out = lp["w_out"].astype(BF16)
    if last:
        return _out_projection_latent(ymix, w_out, xa, lp["norm_post"], gate, batch, n_lat, seq)
    return _out_projection(ymix, w_out, xa, lp["norm_post"], gate, n_lat, seq)


def kernel(x, c, ctx, c_ctx, ada_w, ada_b, norm_pre, norm_post, w_in, conv_w, conv_b, a_log, dt_bias,
           d_skip, ssm_norm, w_o_ssm, gqa_q_norm, gqa_k_norm, w_o_gqa, na_rpb, w_o_na, mla_q_norm,
           w_uq, mla_kv_norm, w_ukv, w_o_mla, w_out):
    batch, n_lat, d = x.shape
    n_ctx = ctx.shape[1]
    seq = n_lat + n_ctx
    stacked = dict(norm_pre=norm_pre, norm_post=norm_post, w_in=w_in,
                   conv_w=conv_w, conv_b=conv_b, a_log=a_log, dt_bias=dt_bias, d_skip=d_skip,
                   ssm_norm=ssm_norm, w_o_ssm=w_o_ssm, gqa_q_norm=gqa_q_norm, gqa_k_norm=gqa_k_norm,
                   w_o_gqa=w_o_gqa, na_bias=_na_bias_table(na_rpb), w_o_na=w_o_na,
                   mla_q_norm=mla_q_norm, w_uq=w_uq,
                   mla_kv_norm=mla_kv_norm, w_ukv=w_ukv, w_o_mla=w_o_mla, w_out=w_out)
    xa = jnp.concatenate([x, ctx], axis=1).reshape(batch * seq, d)
    cc = jnp.concatenate([c, c_ctx[None, :], jnp.zeros((8 - batch - 1, d), c.dtype)], axis=0)
    rope_g = _rope_tables(n_lat, n_ctx, HEAD_DIM)
    rope_m = _rope_tables(n_lat, n_ctx, MLA_ROPE)
    depth = ada_w.shape[0]
    mods = _modulation(cc, ada_w, ada_b)
    for layer in range(depth):
        lp = {k: v[layer] for k, v in stacked.items()}
        xa = _layer(xa, mods[layer], rope_g, rope_m, lp, batch, n_lat, seq, last=layer == depth - 1)
    return xa
```

```python
import functools
import math

import jax
import jax.numpy as jnp
import numpy as np
from jax import lax
from jax.experimental import pallas as pl
from jax.experimental.pallas import tpu as pltpu

F32 = jnp.float32
BF16 = jnp.bfloat16

GRID_W = 64
EPS = 1e-6
ROPE_THETA = 10000.0

SSM_HEADS = 32
SSM_HEAD_DIM = 64
SSM_INNER = SSM_HEADS * SSM_HEAD_DIM
SSM_GROUPS = 4
SSM_STATE = 128
SSM_CONV = 5
SSM_CHUNK = 128
SSM_CONV_DIM = SSM_INNER + 2 * SSM_GROUPS * SSM_STATE

GQA_HEADS = 8
GQA_KV_HEADS = 4
HEAD_DIM = 128
NA_HEADS = 8
NA_WIN_H = 8
NA_WIN_W = 16
NA_QROWS = 4
NA_KROWS = NA_QROWS + NA_WIN_H
NA_HEADS_PER_STEP = 8

MLA_HEADS = 8
MLA_Q_LORA = 768
MLA_KV_LORA = 512
MLA_NOPE = 128
MLA_ROPE = 64
MLA_QK_PAD = 256

N_BRANCH = 4
LANE = 128
VMEM_LIMIT = 56 * 1024 * 1024
NEG_BIG = -1e30
FLASH_TK = 256
GQA_TQ = 512
MLA_TQ = 1024

_SIZES = (SSM_INNER, SSM_CONV_DIM, 2 * SSM_HEADS, 1024, 512, 512, 1024, 1024, 1024, 1024, 1024,
          MLA_Q_LORA, MLA_KV_LORA, MLA_ROPE, 1024, N_BRANCH * 2048)
_OFF = dict(zip(("z", "xbc", "dtr", "gq", "gk", "gv", "gg", "nq", "nk", "nv", "ng",
                 "mqa", "mkva", "mkr", "mg", "mix"), np.cumsum((0,) + _SIZES[:-1]).tolist()))

P_MIX, P_Z, P_XBC = 0, 8192, 10240
P_GQ, P_GG, P_NQ, P_NK, P_NV, P_NG, P_MG, P_MQA = (13312, 14336, 15360, 16384, 17408, 18432,
                                                    19456, 20480)
P_GK, P_GV, P_MKVA = 21504, 22016, 22528
P_WIDTH = 23040
MQA_PAD = 1024
P_MKR = P_MQA + MLA_Q_LORA


def _pick(n, candidates):
    for c in candidates:
        if n % c == 0:
            return c
    raise ValueError(f"no tile for {n} among {candidates}")


def _params(sem):
    return pltpu.CompilerParams(dimension_semantics=sem, vmem_limit_bytes=VMEM_LIMIT)


def _sigmoid(v):
    return 0.5 + 0.5 * jnp.tanh(0.5 * v)


def _silu(v):
    return v * _sigmoid(v)


def _rope(v, cos, sin):
    return v * cos + pltpu.roll(v, 64, axis=1) * sin


def _mod_kernel(c_ref, w_ref, b_ref, o_ref):
    h = _silu(c_ref[...]).astype(BF16)
    o_ref[...] = jnp.dot(h, w_ref[...].astype(BF16), preferred_element_type=F32) + b_ref[...]


def _modulation(cc, ada_w, ada_b):
    rows, d = cc.shape
    depth, _, n = ada_w.shape
    tn = _pick(n, (512, 256, 128))
    return pl.pallas_call(
        _mod_kernel,
        grid=(depth, n // tn),
        in_specs=[pl.BlockSpec((rows, d), lambda l, j: (0, 0)),
                  pl.BlockSpec((None, d, tn), lambda l, j: (l, 0, j)),
                  pl.BlockSpec((None, 1, tn), lambda l, j: (l, 0, j))],
        out_specs=pl.BlockSpec((None, rows, tn), lambda l, j: (l, 0, j)),
        out_shape=jax.ShapeDtypeStruct((depth, rows, n), F32),
        compiler_params=_params(("parallel", "parallel")),
        name="adaln_mod",
    )(cc, ada_w, ada_b.reshape(depth, 1, n))


def _is_ctx_rows(i, tm, tiles_per_batch, n_lat):
    row = (i % tiles_per_batch) * tm + lax.broadcasted_iota(jnp.int32, (tm, 1), 0)
    return row >= n_lat


def _inproj_kernel(x_ref, nw_ref, sc_ref, sh_ref, w_ref, ws_ref, o_ref, os_ref, h_ref, *,
                   tm, tiles_per_batch, n_lat):
    i = pl.program_id(0)

    @pl.when(pl.program_id(1) == 0)
    def _():
        x = x_ref[...]
        y = x * lax.rsqrt(jnp.mean(x * x, axis=-1, keepdims=True) + EPS) * nw_ref[...]
        is_ctx = _is_ctx_rows(i, tm, tiles_per_batch, n_lat)
        sc = jnp.where(is_ctx, sc_ref[1:2, :], sc_ref[0:1, :])
        sh = jnp.where(is_ctx, sh_ref[1:2, :], sh_ref[0:1, :])
        h = (y * (1.0 + sc) + sh).astype(BF16)
        h_ref[...] = h
        os_ref[...] = jnp.dot(h, ws_ref[...], preferred_element_type=F32)

    o_ref[...] = jnp.dot(h_ref[...], w_ref[...], preferred_element_type=F32).astype(o_ref.dtype)


def _in_projection(xa, norm_w, scale, shift, w_p, w_side, n_lat, seq):
    t, d = xa.shape
    tm = _pick(seq, (1056, 768, 512, 256))
    tn = _pick(P_WIDTH, (1536, 1024, 512))
    tpb = seq // tm
    kern = functools.partial(_inproj_kernel, tm=tm, tiles_per_batch=tpb, n_lat=n_lat)
    return pl.pallas_call(
        kern,
        grid=(t // tm, P_WIDTH // tn),
        in_specs=[pl.BlockSpec((tm, d), lambda i, j: (i, 0)),
                  pl.BlockSpec((1, d), lambda i, j: (0, 0)),
                  pl.BlockSpec((None, 2, d), lambda i, j: (i // tpb, 0, 0)),
                  pl.BlockSpec((None, 2, d), lambda i, j: (i // tpb, 0, 0)),
                  pl.BlockSpec((d, tn), lambda i, j: (0, j)),
                  pl.BlockSpec((d, 2 * LANE), lambda i, j: (0, 0))],
        out_specs=[pl.BlockSpec((tm, tn), lambda i, j: (i, j)),
                   pl.BlockSpec((tm, 2 * LANE), lambda i, j: (i, 0))],
        out_shape=[jax.ShapeDtypeStruct((t, P_WIDTH), BF16),
                   jax.ShapeDtypeStruct((t, 2 * LANE), F32)],
        scratch_shapes=[pltpu.VMEM((tm, d), BF16)],
        compiler_params=_params(("parallel", "arbitrary")),
        name="in_proj",
    )(xa, norm_w.reshape(1, d), scale, shift, w_p, w_side)


def _conv_kernel(u_ref, prev_ref, next_ref, w_ref, b_ref, s_ref, o_ref, *,
                 tm, tiles_per_batch, lat_tiles):
    ib = pl.program_id(0) % tiles_per_batch
    first = jnp.logical_or(ib == 0, ib == lat_tiles)
    last = jnp.logical_or(ib == lat_tiles - 1, ib == tiles_per_batch - 1)
    half = SSM_CONV // 2
    row = lax.broadcasted_iota(jnp.int32, (8, 1), 0)
    cw = 2 * LANE
    for c0 in range(0, u_ref.shape[1], cw):
        sl = slice(c0, c0 + cw)
        u = u_ref[:, sl]
        acc = b_ref[:, sl] + w_ref[half:half + 1, sl] * u.astype(F32)
        for idx, k in enumerate(k for k in range(SSM_CONV) if k != half):
            acc = acc + w_ref[k:k + 1, sl] * jnp.dot(s_ref[idx], u, preferred_element_type=F32)
        prev = jnp.where(first, 0.0, prev_ref[:, sl].astype(F32))
        nxt = jnp.where(last, 0.0, next_ref[:, sl].astype(F32))
        top = jnp.zeros_like(prev)
        bot = jnp.zeros_like(nxt)
        for k in range(half):
            reach = half - k
            top = top + w_ref[k:k + 1, sl] * jnp.where(row < reach, pltpu.roll(prev, reach, axis=0), 0.0)
            kk = SSM_CONV - 1 - k
            bot = bot + w_ref[kk:kk + 1, sl] * jnp.where(row >= 8 - reach,
                                                         pltpu.roll(nxt, 8 - reach, axis=0), 0.0)
        y = jnp.concatenate([acc[0:8] + top, acc[8:tm - 8], acc[tm - 8:tm] + bot], axis=0)
        o_ref[:, sl] = _silu(y).astype(o_ref.dtype)


def _conv_silu(p, conv_w, conv_b, n_lat, seq):
    t = p.shape[0]
    tm = 256
    tc = 1024
    tpb = seq // tm
    n_row8 = t // 8
    col0 = P_XBC // tc
    kern = functools.partial(_conv_kernel, tm=tm, tiles_per_batch=tpb, lat_tiles=n_lat // tm)
    half = SSM_CONV // 2
    shifts = np.stack([np.eye(tm, k=k - half, dtype=np.float32) for k in range(SSM_CONV) if k != half])
    return pl.pallas_call(
        kern,
        grid=(t // tm, SSM_CONV_DIM // tc),
        in_specs=[pl.BlockSpec((tm, tc), lambda i, j: (i, col0 + j)),
                  pl.BlockSpec((8, tc), lambda i, j: (jnp.maximum(i * (tm // 8) - 1, 0), col0 + j)),
                  pl.BlockSpec((8, tc), lambda i, j: (jnp.minimum((i + 1) * (tm // 8), n_row8 - 1),
                                                      col0 + j)),
                  pl.BlockSpec((8, tc), lambda i, j: (0, j)),
                  pl.BlockSpec((1, tc), lambda i, j: (0, j)),
                  pl.BlockSpec((SSM_CONV - 1, tm, tm), lambda i, j: (0, 0, 0))],
        out_specs=pl.BlockSpec((tm, tc), lambda i, j: (i, j)),
        out_shape=jax.ShapeDtypeStruct((t, SSM_CONV_DIM), BF16),
        compiler_params=_params(("parallel", "parallel")),
        name="ssm_conv",
    )(p, p, p, jnp.pad(conv_w, ((0, 8 - SSM_CONV), (0, 0))), conv_b.reshape(1, -1),
      jnp.asarray(shifts, BF16))


def _split_dot(a_bf16, v):
    v1 = v.astype(BF16)
    r1 = v - v1.astype(F32)
    v2 = r1.astype(BF16)
    v3 = (r1 - v2.astype(F32)).astype(BF16)
    return jnp.dot(jnp.concatenate([a_bf16, a_bf16, a_bf16], axis=1),
                   jnp.concatenate([v1, v2, v3], axis=0), preferred_element_type=F32)


def _ssd_kernel(uf_ref, ub_ref, dtf_ref, dtb_ref, alog_ref, bias_ref, eh_ref, yf_ref, yb_ref,
                stf_ref, stb_ref):
    @pl.when(pl.program_id(1) == 0)
    def _():
        stf_ref[...] = jnp.zeros_like(stf_ref)
        stb_ref[...] = jnp.zeros_like(stb_ref)

    _ssd_chunk(True, uf_ref, dtf_ref, alog_ref[0], bias_ref[0], eh_ref, yf_ref, stf_ref)
    _ssd_chunk(False, ub_ref, dtb_ref, alog_ref[1], bias_ref[1], eh_ref, yb_ref, stb_ref)


def _ssd_chunk(fwd, u_ref, dtr_ref, a_log, dt_bias, eh_ref, y_ref, st_ref):
    cl = SSM_CHUNK
    raw = dtr_ref[...] + dt_bias
    dt = jnp.maximum(raw, 0.0) + jnp.log(1.0 + jnp.exp(-jnp.abs(raw)))
    da = dt * (-jnp.exp(a_log))
    r = lax.broadcasted_iota(jnp.int32, (cl, cl), 0)
    c = lax.broadcasted_iota(jnp.int32, (cl, cl), 1)
    tri = (r >= c) if fwd else (r <= c)
    acs = _split_dot(jnp.where(tri, 1.0, 0.0).astype(BF16), da)
    total = acs[cl - 1:cl, :] if fwd else acs[0:1, :]
    e_acs = jnp.exp(acs)
    w_end = dt * jnp.exp(total - acs)
    acs_t = acs.T
    dt_t = dt.T

    def hi_lo(v):
        hi = v.astype(BF16)
        return jnp.concatenate([hi, (v - hi.astype(F32)).astype(BF16)], axis=1)

    ex = jnp.dot(jnp.concatenate([hi_lo(w_end), hi_lo(e_acs)], axis=0), eh_ref[...],
                 preferred_element_type=F32)
    w_exp = ex[0:cl]
    e_exp = ex[cl:2 * cl]
    dec = e_exp[cl - 1:cl, :] if fwd else e_exp[0:1, :]
    xw =(u_ref[:, 0:SSM_INNER].astype(F32) * w_exp).astype(BF16)
    lane = lax.broadcasted_iota(jnp.int32, (cl, LANE), 1)
    gw = SSM_INNER // SSM_GROUPS
    hpg = SSM_HEADS // SSM_GROUPS
    for g in range(SSM_GROUPS):
        b_g = u_ref[:, SSM_INNER + g * SSM_STATE:SSM_INNER + (g + 1) * SSM_STATE]
        c_off = SSM_INNER + SSM_GROUPS * SSM_STATE
        c_g = u_ref[:, c_off + g * SSM_STATE:c_off + (g + 1) * SSM_STATE]
        cb = lax.dot_general(c_g, b_g, (((1,), (1,)), ((), ())), preferred_element_type=F32)
        st_g = st_ref[:, g * gw:(g + 1) * gw]
        y_off = jnp.dot(c_g, st_g.astype(BF16), preferred_element_type=F32)
        ys = []
        for k in range(hpg // 2):
            ms = []
            for h in (g * hpg + 2 * k, g * hpg + 2 * k + 1):
                seg = acs[:, h:h + 1] - acs_t[h:h + 1, :]
                dec_h = jnp.exp(jnp.where(tri, seg, NEG_BIG))
                ms.append((dec_h * cb * dt_t[h:h + 1, :]).astype(BF16))
            x_p = u_ref[:, g * gw + k * LANE:g * gw + (k + 1) * LANE]
            zero = jnp.zeros_like(x_p)
            rhs = jnp.concatenate([jnp.where(lane < SSM_HEAD_DIM, x_p, zero),
                                   jnp.where(lane >= SSM_HEAD_DIM, x_p, zero)], axis=0)
            ys.append(jnp.dot(jnp.concatenate(ms, axis=1), rhs, preferred_element_type=F32))
        y_g = jnp.concatenate(ys, axis=1) + y_off * e_exp[:, g * gw:(g + 1) * gw]
        y_ref[:, g * gw:(g + 1) * gw] = y_g.astype(y_ref.dtype)
        upd = lax.dot_general(b_g, xw[:, g * gw:(g + 1) * gw], (((0,), (0,)), ((), ())),
                              preferred_element_type=F32)
        st_ref[:, g * gw:(g + 1) * gw] = st_g * dec[:, g * gw:(g + 1) * gw] + upd


def _ssd(u, dtr, a_log, dt_bias, batch, n_lat, seq):
    cl = SSM_CHUNK
    nch, nlat, nctx = seq // cl, n_lat // cl, (seq - n_lat) // cl

    def fchunk(c):
        return jnp.where(c < nctx, nlat + c, c - nctx)

    def bchunk(c):
        return jnp.where(c < nctx, nlat + nctx - 1 - c, nlat - 1 - (c - nctx))

    def pad_heads(v):
        return jnp.pad(v, ((0, 0), (0, LANE - SSM_HEADS))).reshape(2, 1, LANE)

    eh = np.zeros((2 * LANE, SSM_INNER), np.float32)
    for h in range(SSM_HEADS):
        eh[h, h * SSM_HEAD_DIM:(h + 1) * SSM_HEAD_DIM] = 1.0
        eh[LANE + h, h * SSM_HEAD_DIM:(h + 1) * SSM_HEAD_DIM] = 1.0
    return pl.pallas_call(
        _ssd_kernel,
        grid=(batch, nch),
        in_specs=[pl.BlockSpec((cl, SSM_CONV_DIM), lambda b, c: (b * nch + fchunk(c), 0)),
                  pl.BlockSpec((cl, SSM_CONV_DIM), lambda b, c: (b * nch + bchunk(c), 0)),
                  pl.BlockSpec((cl, LANE), lambda b, c: (b * nch + fchunk(c), 0)),
                  pl.BlockSpec((cl, LANE), lambda b, c: (b * nch + bchunk(c), 1)),
                  pl.BlockSpec((2, 1, LANE), lambda b, c: (0, 0, 0)),
                  pl.BlockSpec((2, 1, LANE), lambda b, c: (0, 0, 0)),
                  pl.BlockSpec((2 * LANE, SSM_INNER), lambda b, c: (0, 0))],
        out_specs=[pl.BlockSpec((cl, SSM_INNER), lambda b, c: (b * nch + fchunk(c), 0)),
                   pl.BlockSpec((cl, SSM_INNER), lambda b, c: (b * nch + bchunk(c), 0))],
        out_shape=[jax.ShapeDtypeStruct((u.shape[0], SSM_INNER), BF16)] * 2,
        scratch_shapes=[pltpu.VMEM((SSM_STATE, SSM_INNER), F32)] * 2,
        compiler_params=_params(("parallel", "arbitrary")),
        name="ssd_scan",
    )(u, u, dtr, dtr, pad_heads(a_log), pad_heads(dt_bias), jnp.asarray(eh, BF16))


def _ssm_out_kernel(yf_ref, yb_ref, x_ref, z_ref, skip_ref, nw_ref, o_ref):
    z = z_ref[...].astype(F32)
    y = (yf_ref[...] + yb_ref[...]).astype(F32)
    g = (y + skip_ref[...] * x_ref[...].astype(F32)) * _silu(z)
    gw = SSM_INNER // SSM_GROUPS
    for k in range(SSM_GROUPS):
        gk = g[:, k * gw:(k + 1) * gw]
        gk = gk * lax.rsqrt(jnp.mean(gk * gk, axis=-1, keepdims=True) + EPS)
        o_ref[:, k * gw:(k + 1) * gw] = (gk * nw_ref[:, k * gw:(k + 1) * gw]).astype(o_ref.dtype)


def _ssm_out(yf, yb, u, p, d_skip, ssm_norm, seq):
    t = u.shape[0]
    tm = _pick(seq, (384, 256, 128))
    w = SSM_INNER
    return pl.pallas_call(
        _ssm_out_kernel,
        grid=(t // tm,),
        in_specs=[pl.BlockSpec((tm, w), lambda i: (i, 0)),
                  pl.BlockSpec((tm, w), lambda i: (i, 0)),
                  pl.BlockSpec((tm, w), lambda i: (i, 0)),
                  pl.BlockSpec((tm, w), lambda i: (i, P_Z // w)),
                  pl.BlockSpec((1, w), lambda i: (0, 0)),
                  pl.BlockSpec((1, w), lambda i: (0, 0))],
        out_specs=pl.BlockSpec((tm, w), lambda i: (i, 0)),
        out_shape=jax.ShapeDtypeStruct((t, w), BF16),
        compiler_params=_params(("parallel",)),
        name="ssm_gated_norm",
    )(yf, yb, u, p, jnp.repeat(d_skip, SSM_HEAD_DIM).reshape(1, w), ssm_norm.reshape(1, w))


def _head_norm_rope_kernel(x_ref, nw_ref, cos_ref, sin_ref, o_ref, *, n_heads):
    for h in range(n_heads):
        sl = slice(h * LANE, (h + 1) * LANE)
        x = x_ref[:, sl].astype(F32)
        y = x * lax.rsqrt(jnp.mean(x * x, axis=-1, keepdims=True) + EPS) * nw_ref[...]
        o_ref[:, sl] = _rope(y, cos_ref[...], sin_ref[...]).astype(o_ref.dtype)


def _head_norm_rope(p, col0, n_heads, norm_w, cos, sin, seq):
    t = p.shape[0]
    tm = _pick(seq, (768, 512, 256))
    tpb = seq // tm
    w = n_heads * LANE
    return pl.pallas_call(
        functools.partial(_head_norm_rope_kernel, n_heads=n_heads),
        grid=(t // tm,),
        in_specs=[pl.BlockSpec((tm, w), lambda i: (i, col0 // w)),
                  pl.BlockSpec((1, LANE), lambda i: (0, 0)),
                  pl.BlockSpec((tm, LANE), lambda i: (i % tpb, 0)),
                  pl.BlockSpec((tm, LANE), lambda i: (i % tpb, 0))],
        out_specs=pl.BlockSpec((tm, w), lambda i: (i, 0)),
        out_shape=jax.ShapeDtypeStruct((t, w), BF16),
        compiler_params=_params(("parallel",)),
        name="head_norm_rope",
    )(p, norm_w.reshape(1, LANE), cos, sin)


def _mla_q_kernel(x_ref, nw_ref, w_ref, cos_ref, sin_ref, o_ref):
    x = x_ref[:, 0:MLA_Q_LORA].astype(F32)
    h = (x * lax.rsqrt(jnp.mean(x * x, axis=-1, keepdims=True) + EPS) * nw_ref[...]).astype(BF16)
    for hd in range(MLA_HEADS):
        c0 = hd * MLA_QK_PAD
        res = jnp.dot(h, w_ref[:, c0:c0 + MLA_QK_PAD], preferred_element_type=F32)
        o_ref[:, c0:c0 + LANE] = res[:, 0:LANE].astype(o_ref.dtype)
        o_ref[:, c0 + LANE:c0 + MLA_QK_PAD] = _rope(res[:, LANE:], cos_ref[...],
                                                    sin_ref[...]).astype(o_ref.dtype)


def _mla_q(p, q_norm, w_uq, cos, sin, seq):
    t = p.shape[0]
    tm = _pick(seq, (768, 512, 256))
    tpb = seq // tm
    kdim = MQA_PAD
    wo = MLA_HEADS * MLA_QK_PAD
    return pl.pallas_call(
        _mla_q_kernel,
        grid=(t // tm,),
        in_specs=[pl.BlockSpec((tm, kdim), lambda i: (i, P_MQA // kdim)),
                  pl.BlockSpec((1, MLA_Q_LORA), lambda i: (0, 0)),
                  pl.BlockSpec((MLA_Q_LORA, wo), lambda i: (0, 0)),
                  pl.BlockSpec((tm, LANE), lambda i: (i % tpb, 0)),
                  pl.BlockSpec((tm, LANE), lambda i: (i % tpb, 0))],
        out_specs=pl.BlockSpec((tm, wo), lambda i: (i, 0)),
        out_shape=jax.ShapeDtypeStruct((t, wo), BF16),
        compiler_params=_params(("parallel",)),
        name="mla_q_up",
    )(p, q_norm.reshape(1, MLA_Q_LORA), w_uq, cos, sin)


def _mla_kv_kernel(x_ref, kr_ref, nw_ref, w_ref, cos_ref, sin_ref, k_ref, v_ref):
    x = x_ref[...].astype(F32)
    h = (x * lax.rsqrt(jnp.mean(x * x, axis=-1, keepdims=True) + EPS) * nw_ref[...]).astype(BF16)
    k_rope = _rope(kr_ref[...].astype(F32), cos_ref[...], sin_ref[...]).astype(k_ref.dtype)
    for hd in range(MLA_HEADS):
        c0 = hd * MLA_QK_PAD
        res = jnp.dot(h, w_ref[:, c0:c0 + 2 * LANE], preferred_element_type=F32)
        k_ref[:, c0:c0 + LANE] = res[:, 0:LANE].astype(k_ref.dtype)
        k_ref[:, c0 + LANE:c0 + MLA_QK_PAD] = k_rope
        v_ref[:, hd * LANE:(hd + 1) * LANE] = res[:, LANE:].astype(v_ref.dtype)


def _mla_kv(p, kv_norm, w_ukv, cos, sin, seq):
    t = p.shape[0]
    tm = _pick(seq, (768, 512, 256))
    tpb = seq // tm
    kdim = MLA_KV_LORA
    wk = MLA_HEADS * MLA_QK_PAD
    return pl.pallas_call(
        _mla_kv_kernel,
        grid=(t // tm,),
        in_specs=[pl.BlockSpec((tm, kdim), lambda i: (i, P_MKVA // kdim)),
                  pl.BlockSpec((tm, LANE), lambda i: (i, P_MKR // LANE)),
                  pl.BlockSpec((1, kdim), lambda i: (0, 0)),
                  pl.BlockSpec((kdim, wk), lambda i: (0, 0)),
                  pl.BlockSpec((tm, LANE), lambda i: (i % tpb, 0)),
                  pl.BlockSpec((tm, LANE), lambda i: (i % tpb, 0))],
        out_specs=[pl.BlockSpec((tm, wk), lambda i: (i, 0)),
                   pl.BlockSpec((tm, MLA_HEADS * LANE), lambda i: (i, 0))],
        out_shape=[jax.ShapeDtypeStruct((t, wk), BF16),
                   jax.ShapeDtypeStruct((t, MLA_HEADS * LANE), BF16)],
        compiler_params=_params(("parallel",)),
        name="mla_kv_up",
    )(p, p, kv_norm.reshape(1, kdim), w_ukv, cos, sin)


def _flash_kernel(q_ref, k_ref, v_ref, g_ref, *rest, r, tq, dq, tk, n_keys, scale):
    o_ref = rest[-1]
    q = jnp.concatenate([q_ref[:, j * dq:(j + 1) * dq] for j in range(r)], axis=0)
    q = (q.astype(F32) * (scale * math.log2(math.e))).astype(BF16)
    m_rows = r * tq
    m = jnp.full((m_rows, 1), NEG_BIG, F32)
    l = jnp.zeros((m_rows, LANE), F32)
    acc = jnp.zeros((m_rows, LANE), F32)
    for c in range(n_keys // tk):
        k_c = k_ref[c * tk:(c + 1) * tk, :]
        v_c = v_ref[c * tk:(c + 1) * tk, :]
        s = lax.dot_general(q, k_c, (((1,), (1,)), ((), ())), preferred_element_type=F32)
        m_new = jnp.maximum(m, jnp.max(s, axis=-1, keepdims=True))
        alpha = jnp.exp2(m - m_new)
        pr = jnp.exp2(s - m_new)
        part = pr[:, 0:LANE]
        for j in range(1, tk // LANE):
            part = part + pr[:, j * LANE:(j + 1) * LANE]
        l = alpha * l + part
        acc = alpha * acc + jnp.dot(pr.astype(BF16), v_c, preferred_element_type=F32)
        m = m_new
    o = acc / jnp.sum(l, axis=-1, keepdims=True)
    for j in range(r):
        gate = _silu(g_ref[:, j * LANE:(j + 1) * LANE].astype(F32))
        o_ref[:, j * LANE:(j + 1) * LANE] = (o[j * tq:(j + 1) * tq] * gate).astype(o_ref.dtype)


def _flash(q, k, v, gates, *, q_col0, k_col0, v_col0, g_col0, v_stride, n_kv_heads, r, dq, scale,
           batch, n_lat, seq, ctx_only, tq, into=None):
    n_ctx = seq - n_lat
    tk = FLASH_TK
    if ctx_only:
        tq = n_ctx
        q_blk0, n_q, kv_rows, kv_blk0 = n_lat // tq, 1, n_ctx, n_lat // n_ctx
    else:
        q_blk0, n_q, kv_rows, kv_blk0 = 0, n_lat // tq, seq, 0
    assert kv_rows % tk == 0
    kern = functools.partial(_flash_kernel, r=r, tq=tq, dq=dq, tk=tk, n_keys=kv_rows, scale=scale)
    w_out = n_kv_heads * r * LANE
    args = [a.reshape(batch, seq, a.shape[-1]) for a in (q, k, v, gates)]
    in_specs = [pl.BlockSpec((None, tq, r * dq), lambda b, g, i: (b, q_blk0 + i, q_col0 // (r * dq) + g)),
                pl.BlockSpec((None, kv_rows, dq), lambda b, g, i: (b, kv_blk0, k_col0 // dq + g)),
                pl.BlockSpec((None, kv_rows, LANE),
                             lambda b, g, i: (b, kv_blk0, v_col0 // LANE + v_stride * g)),
                pl.BlockSpec((None, tq, r * LANE),
                             lambda b, g, i: (b, q_blk0 + i, g_col0 // (r * LANE) + g))]
    aliases = {}
    if into is not None:
        args.append(into.reshape(batch, seq, w_out))
        in_specs.append(pl.BlockSpec(memory_space=pl.ANY))
        aliases = {len(args) - 1: 0}
    out = pl.pallas_call(
        kern,
        grid=(batch, n_kv_heads, n_q),
        in_specs=in_specs,
        out_specs=pl.BlockSpec((None, tq, r * LANE), lambda b, g, i: (b, q_blk0 + i, g)),
        out_shape=jax.ShapeDtypeStruct((batch, seq, w_out), BF16),
        input_output_aliases=aliases,
        compiler_params=_params(("parallel", "parallel", "parallel")),
        name="flash_ctx" if ctx_only else "flash_latent",
    )(*args)
    return out.reshape(batch * seq, w_out)


def _attention(q, k, v, gates, tq, **kw):
    lat = _flash(q, k, v, gates, ctx_only=False, tq=tq, **kw)
    return _flash(q, k, v, gates, ctx_only=True, tq=tq, into=lat, **kw)


def _natten_kernel(q_ref, k0_ref, k1_ref, k2_ref, kc_ref, v0_ref, v1_ref, v2_ref, vc_ref, g_ref,
                   bias_ref, o_ref, *, scale, n_heads):
    n_loc = bias_ref.shape[-1]
    for h in range(n_heads):
        sl = slice(h * LANE, (h + 1) * LANE)
        q = (q_ref[:, sl].astype(F32) * (scale * math.log2(math.e))).astype(BF16)
        k = jnp.concatenate([k0_ref[:, sl], k1_ref[:, sl], k2_ref[:, sl], kc_ref[:, sl]], axis=0)
        v = jnp.concatenate([v0_ref[:, sl], v1_ref[:, sl], v2_ref[:, sl], vc_ref[:, sl]], axis=0)
        s = lax.dot_general(q, k, (((1,), (1,)), ((), ())), preferred_element_type=F32)
        s = jnp.concatenate([s[:, :n_loc] + bias_ref[h], s[:, n_loc:]], axis=1)
        pr = jnp.exp2(s - jnp.max(s, axis=-1, keepdims=True)).astype(BF16)
        ov = jnp.dot(pr, jnp.concatenate([v, jnp.ones_like(v)], axis=1), preferred_element_type=F32)
        o = ov[:, 0:LANE] / ov[:, LANE:]
        o_ref[:, sl] = (o * _silu(g_ref[:, sl].astype(F32))).astype(o_ref.dtype)


def _na_bias_table(rpb):
    qx = np.arange(GRID_W)
    c0 = np.clip(qx - NA_WIN_W // 2, 0, GRID_W - NA_WIN_W)
    col_ok = (qx[None, :] >= c0[:, None]) & (qx[None, :] < c0[:, None] + NA_WIN_W)
    dx = qx[None, :] - qx[:, None] + NA_WIN_W - 1
    pick = np.zeros((2 * NA_WIN_W - 1, GRID_W * GRID_W), np.float32)
    qi, ki = np.nonzero(col_ok)
    pick[dx[qi, ki], qi * GRID_W + ki] = 1.0
    by_dx = jnp.einsum("...d,dn->...n", rpb, pick, precision=lax.Precision.HIGHEST)
    qy, ky = np.arange(NA_QROWS), np.arange(NA_KROWS)
    dys, oks = [], []
    for q_off, first_key in ((0, None), (NA_WIN_H // 2, "q"), (NA_WIN_H, NA_KROWS - NA_WIN_H)):
        r0 = qy if first_key == "q" else np.full_like(qy, 0 if first_key is None else first_key)
        oks.append((ky[None, :] >= r0[:, None]) & (ky[None, :] < r0[:, None] + NA_WIN_H))
        dys.append(np.clip(ky[None, :] - (qy[:, None] + q_off) + NA_WIN_H - 1, 0, 2 * NA_WIN_H - 2))
    rows = jnp.take(by_dx, np.stack(dys).reshape(-1), axis=-2)
    lead = rows.shape[:-2]
    nl = len(lead)
    rows = rows.reshape(lead + (3, NA_QROWS, NA_KROWS, GRID_W, GRID_W))
    rows = rows.transpose(tuple(range(nl)) + (nl, nl + 1, nl + 3, nl + 2, nl + 4))
    ok = np.stack(oks)[:, :, None, :, None] & col_ok[None, None, :, None, :]
    tab = jnp.where(jnp.asarray(ok), rows * math.log2(math.e), NEG_BIG)
    return tab.reshape(lead + (3, NA_QROWS * GRID_W, NA_KROWS * GRID_W)).astype(BF16)


def _natten(p, bias, batch, n_lat, seq):
    n_ctx = seq - n_lat
    tq = NA_QROWS * GRID_W
    n_blk = n_lat // tq
    assert n_blk >= 3 and n_lat % tq == 0 and n_ctx % tq == 0
    rb = seq // tq
    cb = seq // n_ctx
    nk = NA_KROWS * GRID_W
    hps = NA_HEADS_PER_STEP
    hw = hps * LANE

    def kblk(i):
        return jnp.clip(i - 1, 0, n_blk - 3)

    def kspec(col0, j):
        return pl.BlockSpec((tq, hw), lambda b, h, i: (b * rb + kblk(i) + j, col0 // hw + h))

    def cspec(col0):
        return pl.BlockSpec((n_ctx, hw), lambda b, h, i: (b * cb + n_lat // n_ctx, col0 // hw + h))

    def qspec(col0):
        return pl.BlockSpec((tq, hw), lambda b, h, i: (b * rb + i, col0 // hw + h))

    def btype(i):
        return jnp.where(i == 0, 0, jnp.where(i == n_blk - 1, 2, 1))

    return pl.pallas_call(
        functools.partial(_natten_kernel, scale=HEAD_DIM ** -0.5, n_heads=hps),
        grid=(batch, NA_HEADS // hps, n_blk),
        in_specs=[qspec(P_NQ), kspec(P_NK, 0), kspec(P_NK, 1), kspec(P_NK, 2), cspec(P_NK),
                  kspec(P_NV, 0), kspec(P_NV, 1), kspec(P_NV, 2), cspec(P_NV), qspec(P_NG),
                  pl.BlockSpec((hps, None, tq, nk), lambda b, h, i: (h, btype(i), 0, 0))],
        out_specs=pl.BlockSpec((tq, hw), lambda b, h, i: (b * rb + i, h)),
        out_shape=jax.ShapeDtypeStruct((p.shape[0], NA_HEADS * LANE), BF16),
        compiler_params=_params(("parallel", "parallel", "parallel")),
        name="natten_latent",
    )(p, p, p, p, p, p, p, p, p, p, bias)


def _merge_kernel(o0_ref, o1_ref, o2_ref, o3_ref, w0_ref, w1_ref, w2_ref, w3_ref,
                  m0_ref, m1_ref, m2_ref, m3_ref, y_ref):
    acc = None
    for o_ref, w_ref, m_ref in ((o0_ref, w0_ref, m0_ref), (o1_ref, w1_ref, m1_ref),
                                (o2_ref, w2_ref, m2_ref), (o3_ref, w3_ref, m3_ref)):
        gate = _sigmoid(m_ref[...].astype(F32))
        term = gate * jnp.dot(o_ref[...], w_ref[...], preferred_element_type=F32)
        acc = term if acc is None else acc + term
    y_ref[...] = acc.astype(y_ref.dtype)


def _merge(outs, weights, p, seq):
    t = p.shape[0]
    d = weights[0].shape[1]
    tm = _pick(seq, (1056, 768, 512, 256))
    tn = 512
    o_specs =[pl.BlockSpec((tm, o.shape[1]), lambda i, j: (i, 0)) for o in outs]
    w_specs = [pl.BlockSpec((w.shape[0], tn), lambda i, j: (0, j)) for w in weights]
    m_specs = [pl.BlockSpec((tm, tn), lambda i, j, b=b: (i, (P_MIX + b * d) // tn + j))
               for b in range(N_BRANCH)]
    return pl.pallas_call(
        _merge_kernel,
        grid=(t // tm, d // tn),
        in_specs=o_specs + w_specs + m_specs,
        out_specs=pl.BlockSpec((tm, tn), lambda i, j: (i, j)),
        out_shape=jax.ShapeDtypeStruct((t, d), BF16),
        compiler_params=_params(("parallel", "parallel")),
        name="branch_merge",
    )(*outs, *weights, p, p, p, p)


def _out_kernel(y_ref, w_ref, x_ref, nw_ref, gate_ref, o_ref, *, tm, tiles_per_batch, n_lat):
    z = jnp.dot(y_ref[...], w_ref[...], preferred_element_type=F32)
    zn = z * lax.rsqrt(jnp.mean(z * z, axis=-1, keepdims=True) + EPS) * nw_ref[...]
    if tiles_per_batch is None:
        gate = gate_ref[0:1, :]
    else:
        is_ctx = _is_ctx_rows(pl.program_id(0), tm, tiles_per_batch, n_lat)
        gate = jnp.where(is_ctx, gate_ref[1:2, :], gate_ref[0:1, :])
    o_ref[...] = x_ref[...] + gate * zn


def _out_projection_latent(y, w_out, xa, norm_w, gate, batch, n_lat, seq):
    d = xa.shape[1]
    tm = _pick(n_lat, (512, 256, 128))
    kern = functools.partial(_out_kernel, tm=tm, tiles_per_batch=None, n_lat=n_lat)
    return pl.pallas_call(
        kern,
        grid=(batch, n_lat // tm),
        in_specs=[pl.BlockSpec((None, tm, d), lambda b, i: (b, i, 0)),
                  pl.BlockSpec((d, d), lambda b, i: (0, 0)),
                  pl.BlockSpec((None, tm, d), lambda b, i: (b, i, 0)),
                  pl.BlockSpec((1, d), lambda b, i: (0, 0)),
                  pl.BlockSpec((None, 2, d), lambda b, i: (b, 0, 0))],
        out_specs=pl.BlockSpec((None, tm, d), lambda b, i: (b, i, 0)),
        out_shape=jax.ShapeDtypeStruct((batch, n_lat, d), F32),
        compiler_params=_params(("parallel", "parallel")),
        name="out_proj_last",
    )(y.reshape(batch, seq, d), w_out, xa.reshape(batch, seq, d), norm_w.reshape(1, d), gate)


def _out_projection(y, w_out, xa, norm_w, gate, n_lat, seq):
    t, d = xa.shape
    tm = _pick(seq, (768, 384, 256, 128))
    tpb = seq // tm
    kern = functools.partial(_out_kernel, tm=tm, tiles_per_batch=tpb, n_lat=n_lat)
    return pl.pallas_call(
        kern,
        grid=(t // tm,),
        in_specs=[pl.BlockSpec((tm, d), lambda i: (i, 0)),
                  pl.BlockSpec((d, d), lambda i: (0, 0)),
                  pl.BlockSpec((tm, d), lambda i: (i, 0)),
                  pl.BlockSpec((1, d), lambda i: (0, 0)),
                  pl.BlockSpec((None, 2, d), lambda i: (i // tpb, 0, 0))],
        out_specs=pl.BlockSpec((tm, d), lambda i: (i, 0)),
        out_shape=jax.ShapeDtypeStruct((t, d), F32),
        compiler_params=_params(("parallel",)),
        name="out_proj",
    )(y, w_out, xa, norm_w.reshape(1, d), gate)


def _pairs_apart(w, n_heads, dim):
    lead = w.shape[:-1]
    return w.reshape(lead + (n_heads, dim // 2, 2)).swapaxes(-1, -2).reshape(lead + (n_heads * dim,))


def _rope_tile(w):
    lead = w.shape[:-1]
    pr = w.reshape(lead + (MLA_ROPE // 2, 2))
    zero = jnp.zeros(lead + (MLA_ROPE // 2,), w.dtype)
    return jnp.concatenate([pr[..., 0], zero, pr[..., 1], zero], axis=-1)


def _layout_w_in(w):
    o = _OFF
    k = w.shape[0]

    def seg(name, width):
        return w[:, o[name]:o[name] + width]

    cols = [seg("mix", 8192), seg("z", 2048), seg("xbc", SSM_CONV_DIM),
            _pairs_apart(seg("gq", 1024), GQA_HEADS, HEAD_DIM), seg("gg", 1024),
            seg("nq", 1024), seg("nk", 1024), seg("nv", 1024), seg("ng", 1024), seg("mg", 1024),
            seg("mqa", MLA_Q_LORA), _rope_tile(seg("mkr", MLA_ROPE)),
            jnp.zeros((k, MQA_PAD - MLA_Q_LORA - LANE), w.dtype),
            _pairs_apart(seg("gk", 512), GQA_KV_HEADS, HEAD_DIM), seg("gv", 512),
            seg("mkva", MLA_KV_LORA)]
    main = jnp.concatenate(cols, axis=1).astype(BF16)
    assert main.shape[1] == P_WIDTH
    dtr = seg("dtr", 2 * SSM_HEADS)
    zero = jnp.zeros((k, LANE - SSM_HEADS), w.dtype)
    side = jnp.concatenate([dtr[:, :SSM_HEADS], zero, dtr[:, SSM_HEADS:], zero], axis=1).astype(BF16)
    return main, side


def _layout_w_uq(w_uq):
    k = w_uq.shape[0]
    w = w_uq.reshape(k, MLA_HEADS, MLA_NOPE + MLA_ROPE)
    w = jnp.concatenate([w[..., :MLA_NOPE], _rope_tile(w[..., MLA_NOPE:])], axis=-1)
    return w.reshape(k, MLA_HEADS * MLA_QK_PAD).astype(BF16)


def _rope_tables(n_lat, n_ctx, dim):
    t = np.arange(n_lat)
    quarter = dim // 4
    freqs = ROPE_THETA ** (-jnp.arange(quarter, dtype=F32) / quarter)
    row = jnp.asarray(t // GRID_W, F32)
    col = jnp.asarray(t % GRID_W, F32)
    ang = jnp.concatenate([row[:, None] * freqs, col[:, None] * freqs], axis=-1)
    cos, sin = jnp.cos(ang), jnp.sin(ang)
    pad = 64 - dim // 2
    one, zero = jnp.ones((n_lat, pad), F32), jnp.zeros((n_lat, pad), F32)
    cos_t = jnp.concatenate([cos, one, cos, one], axis=-1)
    sin_t = jnp.concatenate([-sin, zero, sin, zero], axis=-1)
    cos_t = jnp.concatenate([cos_t, jnp.ones((n_ctx, LANE), F32)], axis=0)
    sin_t = jnp.concatenate([sin_t, jnp.zeros((n_ctx, LANE), F32)], axis=0)
    return cos_t, sin_t


def _layer(xa, mod, rope_g, rope_m, lp, batch, n_lat, seq, last):
    d = xa.shape[1]

    def per_row(v):
        return jnp.stack([v[:batch], jnp.broadcast_to(v[batch:batch + 1], (batch, d))], axis=1)

    shift, scale, gate = (per_row(mod[:, k * d:(k + 1) * d]) for k in range(3))
    w_main, w_side = _layout_w_in(lp["w_in"])
    p, dtr = _in_projection(xa, lp["norm_pre"], scale, shift, w_main, w_side, n_lat, seq)

    u = _conv_silu(p, lp["conv_w"], lp["conv_b"], n_lat, seq)
    yf, yb = _ssd(u, dtr, lp["a_log"], lp["dt_bias"], batch, n_lat, seq)
    o_ssm = _ssm_out(yf, yb, u, p, lp["d_skip"], lp["ssm_norm"], seq)

    common = dict(batch=batch, n_lat=n_lat, seq=seq)
    qg = _head_norm_rope(p, P_GQ, GQA_HEADS, _pairs_apart(lp["gqa_q_norm"], 1, HEAD_DIM), *rope_g, seq)
    kg = _head_norm_rope(p, P_GK, GQA_KV_HEADS, _pairs_apart(lp["gqa_k_norm"], 1, HEAD_DIM), *rope_g, seq)
    o_gqa = _attention(
        qg, kg, p, p, GQA_TQ, q_col0=0, k_col0=0, v_col0=P_GV, g_col0=P_GG, v_stride=1,
        n_kv_heads=GQA_KV_HEADS, r=GQA_HEADS // GQA_KV_HEADS, dq=HEAD_DIM, scale=HEAD_DIM ** -0.5,
        **common)

    na_lat = _natten(p, lp["na_bias"], **common)
    o_na = _flash(p, p, p, p, q_col0=P_NQ, k_col0=P_NK, v_col0=P_NV, g_col0=P_NG, v_stride=1,
                  n_kv_heads=NA_HEADS, r=1, dq=HEAD_DIM, scale=HEAD_DIM ** -0.5, ctx_only=True,
                  tq=256, into=na_lat, **common)

    qm = _mla_q(p, lp["mla_q_norm"], _layout_w_uq(lp["w_uq"]), *rope_m, seq)
    km, vm = _mla_kv(p, lp["mla_kv_norm"], lp["w_ukv"].astype(BF16), *rope_m, seq)
    o_mla = _attention(
        qm, km, vm, p, MLA_TQ, q_col0=0, k_col0=0, v_col0=0, g_col0=P_MG, v_stride=1,
        n_kv_heads=MLA_HEADS, r=1, dq=MLA_QK_PAD, scale=(MLA_NOPE + MLA_ROPE) ** -0.5,
        **common)

    weights = [lp[n].astype(BF16) for n in ("w_o_ssm", "w_o_gqa", "w_o_na", "w_o_mla")]
    ymix = _merge([o_ssm, o_gqa, o_na, o_mla], weights, p, seq)
    w_out = lp["w_out"].astype(BF16)
    if last:
        return _out_projection_latent(ymix, w_out, xa, lp["norm_post"], gate, batch, n_lat, seq)
    return _out_projection(ymix, w_out, xa, lp["norm_post"], gate, n_lat, seq)


def kernel(x, c, ctx, c_ctx, ada_w, ada_b, norm_pre, norm_post, w_in, conv_w, conv_b, a_log, dt_bias,
           d_skip, ssm_norm, w_o_ssm, gqa_q_norm, gqa_k_norm, w_o_gqa, na_rpb, w_o_na, mla_q_norm,
           w_uq, mla_kv_norm, w_ukv, w_o_mla, w_out):
    batch, n_lat, d = x.shape
    n_ctx = ctx.shape[1]
    seq = n_lat + n_ctx
    stacked = dict(norm_pre=norm_pre, norm_post=norm_post, w_in=w_in,
                   conv_w=conv_w, conv_b=conv_b, a_log=a_log, dt_bias=dt_bias, d_skip=d_skip,
                   ssm_norm=ssm_norm, w_o_ssm=w_o_ssm, gqa_q_norm=gqa_q_norm, gqa_k_norm=gqa_k_norm,
                   w_o_gqa=w_o_gqa, na_bias=_na_bias_table(na_rpb), w_o_na=w_o_na,
                   mla_q_norm=mla_q_norm, w_uq=w_uq,
                   mla_kv_norm=mla_kv_norm, w_ukv=w_ukv, w_o_mla=w_o_mla, w_out=w_out)
    xa = jnp.concatenate([x, ctx], axis=1).reshape(batch * seq, d)
    cc = jnp.concatenate([c, c_ctx[None, :], jnp.zeros((8 - batch - 1, d), c.dtype)], axis=0)
    rope_g = _rope_tables(n_lat, n_ctx, HEAD_DIM)
    rope_m = _rope_tables(n_lat, n_ctx, MLA_ROPE)
    depth = ada_w.shape[0]
    mods = _modulation(cc, ada_w, ada_b)
    for layer in range(depth):
        lp = {k: v[layer] for k, v in stacked.items()}
        xa = _layer(xa, mods[layer], rope_g, rope_m, lp, batch, n_lat, seq, last=layer == depth - 1)
    return xa
```

```python
import functools
import math

import jax
import jax.numpy as jnp
import numpy as np
from jax import lax
from jax.experimental import pallas as pl
from jax.experimental.pallas import tpu as pltpu

F32 = jnp.float32
BF16 = jnp.bfloat16

GRID_W = 64
EPS = 1e-6
ROPE_THETA = 10000.0

SSM_HEADS = 32
SSM_HEAD_DIM = 64
SSM_INNER = SSM_HEADS * SSM_HEAD_DIM
SSM_GROUPS = 4
SSM_STATE = 128
SSM_CONV = 5
SSM_CHUNK = 128
SSM_CONV_DIM = SSM_INNER + 2 * SSM_GROUPS * SSM_STATE

GQA_HEADS = 8
GQA_KV_HEADS = 4
HEAD_DIM = 128
NA_HEADS = 8
NA_WIN_H = 8
NA_WIN_W = 16
NA_QROWS = 4
NA_KROWS = NA_QROWS + NA_WIN_H
NA_HEADS_PER_STEP = 8

MLA_HEADS = 8
MLA_Q_LORA = 768
MLA_KV_LORA = 512
MLA_NOPE = 128
MLA_ROPE = 64
MLA_QK_PAD = 256

N_BRANCH = 4
LANE = 128
VMEM_LIMIT = 56 * 1024 * 1024
NEG_BIG = -1e30
FLASH_TK = 256
GQA_TQ = 512
MLA_TQ = 1024

_SIZES = (SSM_INNER, SSM_CONV_DIM, 2 * SSM_HEADS, 1024, 512, 512, 1024, 1024, 1024, 1024, 1024,
          MLA_Q_LORA, MLA_KV_LORA, MLA_ROPE, 1024, N_BRANCH * 2048)
_OFF = dict(zip(("z", "xbc", "dtr", "gq", "gk", "gv", "gg", "nq", "nk", "nv", "ng",
                 "mqa", "mkva", "mkr", "mg", "mix"), np.cumsum((0,) + _SIZES[:-1]).tolist()))

P_MIX, P_Z, P_XBC = 0, 8192, 10240
P_GQ, P_GG, P_NQ, P_NK, P_NV, P_NG, P_MG, P_MQA = (13312, 14336, 15360, 16384, 17408, 18432,
                                                    19456, 20480)
P_GK, P_GV, P_MKVA = 21504, 22016, 22528
P_WIDTH = 23040
MQA_PAD = 1024
P_MKR = P_MQA + MLA_Q_LORA


def _pick(n, candidates):
    for c in candidates:
        if n % c == 0:
            return c
    raise ValueError(f"no tile for {n} among {candidates}")


def _params(sem):
    return pltpu.CompilerParams(dimension_semantics=sem, vmem_limit_bytes=VMEM_LIMIT)


def _sigmoid(v):
    return 0.5 + 0.5 * jnp.tanh(0.5 * v)


def _silu(v):
    return v * _sigmoid(v)


def _rope(v, cos, sin):
    return v * cos + pltpu.roll(v, 64, axis=1) * sin


def _mod_kernel(c_ref, w_ref, b_ref, o_ref):
    h = _silu(c_ref[...]).astype(BF16)
    o_ref[...] = jnp.dot(h, w_ref[...].astype(BF16), preferred_element_type=F32) + b_ref[...]


def _modulation(cc, ada_w, ada_b):
    rows, d = cc.shape
    depth, _, n = ada_w.shape
    tn = _pick(n, (512, 256, 128))
    return pl.pallas_call(
        _mod_kernel,
        grid=(depth, n // tn),
        in_specs=[pl.BlockSpec((rows, d), lambda l, j: (0, 0)),
                  pl.BlockSpec((None, d, tn), lambda l, j: (l, 0, j)),
                  pl.BlockSpec((None, 1, tn), lambda l, j: (l, 0, j))],
        out_specs=pl.BlockSpec((None, rows, tn), lambda l, j: (l, 0, j)),
        out_shape=jax.ShapeDtypeStruct((depth, rows, n), F32),
        compiler_params=_params(("parallel", "parallel")),
        name="adaln_mod",
    )(cc, ada_w, ada_b.reshape(depth, 1, n))


def _inproj_kernel(x_ref, nw_ref, sc_ref, sh_ref, w_ref, ws_ref, o_ref, os_ref, h_ref, *,
                   tm, tiles_per_batch, n_lat):
    i = pl.program_id(0)
    n_chunk = 3 if tm % 48 == 0 else 1
    rc = tm // n_chunk

    @pl.when(pl.program_id(1) == 0)
    def _():
        row0 = (i % tiles_per_batch) * tm
        for c in range(n_chunk):
            rows = slice(c * rc, (c + 1) * rc)
            x = x_ref[rows, :]
            y = x * lax.rsqrt(jnp.mean(x * x, axis=-1, keepdims=True) + EPS) * nw_ref[...]
            is_ctx = row0 + c * rc + lax.broadcasted_iota(jnp.int32, (rc, 1), 0) >= n_lat
            sc = jnp.where(is_ctx, sc_ref[1:2, :], sc_ref[0:1, :])
            sh = jnp.where(is_ctx, sh_ref[1:2, :], sh_ref[0:1, :])
            h = (y * (1.0 + sc) + sh).astype(BF16)
            h_ref[rows, :] = h
            os_ref[rows, :] = jnp.dot(h, ws_ref[...], preferred_element_type=F32)
            o_ref[rows, :] = jnp.dot(h, w_ref[...], preferred_element_type=F32).astype(o_ref.dtype)

    @pl.when(pl.program_id(1) != 0)
    def _():
        o_ref[...] = jnp.dot(h_ref[...], w_ref[...], preferred_element_type=F32).astype(o_ref.dtype)


def _in_projection(xa, norm_w, scale, shift, w_p, w_side, n_lat, seq):
    t, d = xa.shape
    tm = _pick(seq, (1056, 768, 512, 256))
    tn = _pick(P_WIDTH, (1536, 1024, 512))
    tpb = seq // tm
    kern = functools.partial(_inproj_kernel, tm=tm, tiles_per_batch=tpb, n_lat=n_lat)
    return pl.pallas_call(
        kern,
        grid=(t // tm, P_WIDTH // tn),
        in_specs=[pl.BlockSpec((tm, d), lambda i, j: (i, 0)),
                  pl.BlockSpec((1, d), lambda i, j: (0, 0)),
                  pl.BlockSpec((None, 2, d), lambda i, j: (i // tpb, 0, 0)),
                  pl.BlockSpec((None, 2, d), lambda i, j: (i // tpb, 0, 0)),
                  pl.BlockSpec((d, tn), lambda i, j: (0, j)),
                  pl.BlockSpec((d, 2 * LANE), lambda i, j: (0, 0))],
        out_specs=[pl.BlockSpec((tm, tn), lambda i, j: (i, j)),
                   pl.BlockSpec((tm, 2 * LANE), lambda i, j: (i, 0))],
        out_shape=[jax.ShapeDtypeStruct((t, P_WIDTH), BF16),
                   jax.ShapeDtypeStruct((t, 2 * LANE), F32)],
        scratch_shapes=[pltpu.VMEM((tm, d), BF16)],
        compiler_params=_params(("parallel", "arbitrary")),
        name="in_proj",
    )(xa, norm_w.reshape(1, d), scale, shift, w_p, w_side)


def _conv_kernel(u_ref, prev_ref, next_ref, w_ref, b_ref, s_ref, o_ref, *,
                 tm, tiles_per_batch, lat_tiles):
    ib = pl.program_id(0) % tiles_per_batch
    first = jnp.logical_or(ib == 0, ib == lat_tiles)
    last = jnp.logical_or(ib == lat_tiles - 1, ib == tiles_per_batch - 1)
    half = SSM_CONV // 2
    row = lax.broadcasted_iota(jnp.int32, (8, 1), 0)
    cw = 2 * LANE
    for c0 in range(0, u_ref.shape[1], cw):
        sl = slice(c0, c0 + cw)
        u = u_ref[:, sl]
        acc = b_ref[:, sl] + w_ref[half:half + 1, sl] * u.astype(F32)
        for idx, k in enumerate(k for k in range(SSM_CONV) if k != half):
            acc = acc + w_ref[k:k + 1, sl] * jnp.dot(s_ref[idx], u, preferred_element_type=F32)
        prev = jnp.where(first, 0.0, prev_ref[:, sl].astype(F32))
        nxt = jnp.where(last, 0.0, next_ref[:, sl].astype(F32))
        top = jnp.zeros_like(prev)
        bot = jnp.zeros_like(nxt)
        for k in range(half):
            reach = half - k
            top = top + w_ref[k:k + 1, sl] * jnp.where(row < reach, pltpu.roll(prev, reach, axis=0), 0.0)
            kk = SSM_CONV - 1 - k
            bot = bot + w_ref[kk:kk + 1, sl] * jnp.where(row >= 8 - reach,
                                                         pltpu.roll(nxt, 8 - reach, axis=0), 0.0)
        y = jnp.concatenate([acc[0:8] + top, acc[8:tm - 8], acc[tm - 8:tm] + bot], axis=0)
        o_ref[:, sl] = _silu(y).astype(o_ref.dtype)


def _conv_silu(p, conv_w, conv_b, n_lat, seq):
    t = p.shape[0]
    tm = 256
    tc = 1024
    tpb = seq // tm
    n_row8 = t // 8
    col0 = P_XBC // tc
    kern = functools.partial(_conv_kernel, tm=tm, tiles_per_batch=tpb, lat_tiles=n_lat // tm)
    half = SSM_CONV // 2
    shifts = np.stack([np.eye(tm, k=k - half, dtype=np.float32) for k in range(SSM_CONV) if k != half])
    return pl.pallas_call(
        kern,
        grid=(t // tm, SSM_CONV_DIM // tc),
        in_specs=[pl.BlockSpec((tm, tc), lambda i, j: (i, col0 + j)),
                  pl.BlockSpec((8, tc), lambda i, j: (jnp.maximum(i * (tm // 8) - 1, 0), col0 + j)),
                  pl.BlockSpec((8, tc), lambda i, j: (jnp.minimum((i + 1) * (tm // 8), n_row8 - 1),
                                                      col0 + j)),
                  pl.BlockSpec((8, tc), lambda i, j: (0, j)),
                  pl.BlockSpec((1, tc), lambda i, j: (0, j)),
                  pl.BlockSpec((SSM_CONV - 1, tm, tm), lambda i, j: (0, 0, 0))],
        out_specs=pl.BlockSpec((tm, tc), lambda i, j: (i, j)),
        out_shape=jax.ShapeDtypeStruct((t, SSM_CONV_DIM), BF16),
        compiler_params=_params(("parallel", "parallel")),
        name="ssm_conv",
    )(p, p, p, jnp.pad(conv_w, ((0, 8 - SSM_CONV), (0, 0))), conv_b.reshape(1, -1),
      jnp.asarray(shifts, BF16))


def _split_dot(a_bf16, v):
    v1 = v.astype(BF16)
    r1 = v - v1.astype(F32)
    v2 = r1.astype(BF16)
    v3 = (r1 - v2.astype(F32)).astype(BF16)
    return jnp.dot(jnp.concatenate([a_bf16, a_bf16, a_bf16], axis=1),
                   jnp.concatenate([v1, v2, v3], axis=0), preferred_element_type=F32)


def _ssd_kernel(uf_ref, ub_ref, dtf_ref, dtb_ref, alog_ref, bias_ref, eh_ref, yf_ref, yb_ref,
                stf_ref, stb_ref):
    @pl.when(pl.program_id(1) == 0)
    def _():
        stf_ref[...] = jnp.zeros_like(stf_ref)
        stb_ref[...] = jnp.zeros_like(stb_ref)

    _ssd_chunk(True, uf_ref, dtf_ref, alog_ref[0], bias_ref[0], eh_ref, yf_ref, stf_ref)
    _ssd_chunk(False, ub_ref, dtb_ref, alog_ref[1], bias_ref[1], eh_ref, yb_ref, stb_ref)


def _ssd_chunk(fwd, u_ref, dtr_ref, a_log, dt_bias, eh_ref, y_ref, st_ref):
    cl = SSM_CHUNK
    raw = dtr_ref[...] + dt_bias
    dt = jnp.maximum(raw, 0.0) + jnp.log(1.0 + jnp.exp(-jnp.abs(raw)))
    da = dt * (-jnp.exp(a_log))
    r = lax.broadcasted_iota(jnp.int32, (cl, cl), 0)
    c = lax.broadcasted_iota(jnp.int32, (cl, cl), 1)
    tri = (r >= c) if fwd else (r <= c)
    acs = _split_dot(jnp.where(tri, 1.0, 0.0).astype(BF16), da)
    total = acs[cl - 1:cl, :] if fwd else acs[0:1, :]
    e_acs = jnp.exp(acs)
    w_end = dt * jnp.exp(total - acs)
    acs_t = acs.T
    dt_t = dt.T

    def hi_lo(v):
        hi = v.astype(BF16)
        return jnp.concatenate([hi, (v - hi.astype(F32)).astype(BF16)], axis=1)

    ex = jnp.dot(jnp.concatenate([hi_lo(w_end), hi_lo(e_acs)], axis=0), eh_ref[...],
                 preferred_element_type=F32)
    w_exp = ex[0:cl]
    e_exp = ex[cl:2 * cl]
    dec = e_exp[cl - 1:cl, :] if fwd else e_exp[0:1, :]
    xw =(u_ref[:, 0:SSM_INNER].astype(F32) * w_exp).astype(BF16)
    lane = lax.broadcasted_iota(jnp.int32, (cl, LANE), 1)
    gw = SSM_INNER // SSM_GROUPS
    hpg = SSM_HEADS // SSM_GROUPS
    for g in range(SSM_GROUPS):
        b_g = u_ref[:, SSM_INNER + g * SSM_STATE:SSM_INNER + (g + 1) * SSM_STATE]
        c_off = SSM_INNER + SSM_GROUPS * SSM_STATE
        c_g = u_ref[:, c_off + g * SSM_STATE:c_off + (g + 1) * SSM_STATE]
        cb = lax.dot_general(c_g, b_g, (((1,), (1,)), ((), ())), preferred_element_type=F32)
        st_g = st_ref[:, g * gw:(g + 1) * gw]
        y_off = jnp.dot(c_g, st_g.astype(BF16), preferred_element_type=F32)
        ys = []
        for k in range(hpg // 2):
            ms = []
            for h in (g * hpg + 2 * k, g * hpg + 2 * k + 1):
                seg = acs[:, h:h + 1] - acs_t[h:h + 1, :]
                dec_h = jnp.exp(jnp.where(tri, seg, NEG_BIG))
                ms.append((dec_h * cb * dt_t[h:h + 1, :]).astype(BF16))
            x_p = u_ref[:, g * gw + k * LANE:g * gw + (k + 1) * LANE]
            zero = jnp.zeros_like(x_p)
            rhs = jnp.concatenate([jnp.where(lane < SSM_HEAD_DIM, x_p, zero),
                                   jnp.where(lane >= SSM_HEAD_DIM, x_p, zero)], axis=0)
            ys.append(jnp.dot(jnp.concatenate(ms, axis=1), rhs, preferred_element_type=F32))
        y_g = jnp.concatenate(ys, axis=1) + y_off * e_exp[:, g * gw:(g + 1) * gw]
        y_ref[:, g * gw:(g + 1) * gw] = y_g.astype(y_ref.dtype)
        upd = lax.dot_general(b_g, xw[:, g * gw:(g + 1) * gw], (((0,), (0,)), ((), ())),
                              preferred_element_type=F32)
        st_ref[:, g * gw:(g + 1) * gw] = st_g * dec[:, g * gw:(g + 1) * gw] + upd


def _ssd(u, dtr, a_log, dt_bias, batch, n_lat, seq):
    cl = SSM_CHUNK
    nch, nlat, nctx = seq // cl, n_lat // cl, (seq - n_lat) // cl

    def fchunk(c):
        return jnp.where(c < nctx, nlat + c, c - nctx)

    def bchunk(c):
        return jnp.where(c < nctx, nlat + nctx - 1 - c, nlat - 1 - (c - nctx))

    def pad_heads(v):
        return jnp.pad(v, ((0, 0), (0, LANE - SSM_HEADS))).reshape(2, 1, LANE)

    eh = np.zeros((2 * LANE, SSM_INNER), np.float32)
    for h in range(SSM_HEADS):
        eh[h, h * SSM_HEAD_DIM:(h + 1) * SSM_HEAD_DIM] = 1.0
        eh[LANE + h, h * SSM_HEAD_DIM:(h + 1) * SSM_HEAD_DIM] = 1.0
    return pl.pallas_call(
        _ssd_kernel,
        grid=(batch, nch),
        in_specs=[pl.BlockSpec((cl, SSM_CONV_DIM), lambda b, c: (b * nch + fchunk(c), 0)),
                  pl.BlockSpec((cl, SSM_CONV_DIM), lambda b, c: (b * nch + bchunk(c), 0)),
                  pl.BlockSpec((cl, LANE), lambda b, c: (b * nch + fchunk(c), 0)),
                  pl.BlockSpec((cl, LANE), lambda b, c: (b * nch + bchunk(c), 1)),
                  pl.BlockSpec((2, 1, LANE), lambda b, c: (0, 0, 0)),
                  pl.BlockSpec((2, 1, LANE), lambda b, c: (0, 0, 0)),
                  pl.BlockSpec((2 * LANE, SSM_INNER), lambda b, c: (0, 0))],
        out_specs=[pl.BlockSpec((cl, SSM_INNER), lambda b, c: (b * nch + fchunk(c), 0)),
                   pl.BlockSpec((cl, SSM_INNER), lambda b, c: (b * nch + bchunk(c), 0))],
        out_shape=[jax.ShapeDtypeStruct((u.shape[0], SSM_INNER), BF16)] * 2,
        scratch_shapes=[pltpu.VMEM((SSM_STATE, SSM_INNER), F32)] * 2,
        compiler_params=_params(("parallel", "arbitrary")),
        name="ssd_scan",
    )(u, u, dtr, dtr, pad_heads(a_log), pad_heads(dt_bias), jnp.asarray(eh, BF16))


def _ssm_out_kernel(yf_ref, yb_ref, x_ref, z_ref, skip_ref, nw_ref, o_ref):
    z = z_ref[...].astype(F32)
    y = (yf_ref[...] + yb_ref[...]).astype(F32)
    g = (y + skip_ref[...] * x_ref[...].astype(F32)) * _silu(z)
    gw = SSM_INNER // SSM_GROUPS
    for k in range(SSM_GROUPS):
        gk = g[:, k * gw:(k + 1) * gw]
        gk = gk * lax.rsqrt(jnp.mean(gk * gk, axis=-1, keepdims=True) + EPS)
        o_ref[:, k * gw:(k + 1) * gw] = (gk * nw_ref[:, k * gw:(k + 1) * gw]).astype(o_ref.dtype)


def _ssm_out(yf, yb, u, p, d_skip, ssm_norm, seq):
    t = u.shape[0]
    tm = _pick(seq, (384, 256, 128))
    w = SSM_INNER
    return pl.pallas_call(
        _ssm_out_kernel,
        grid=(t // tm,),
        in_specs=[pl.BlockSpec((tm, w), lambda i: (i, 0)),
                  pl.BlockSpec((tm, w), lambda i: (i, 0)),
                  pl.BlockSpec((tm, w), lambda i: (i, 0)),
                  pl.BlockSpec((tm, w), lambda i: (i, P_Z // w)),
                  pl.BlockSpec((1, w), lambda i: (0, 0)),
                  pl.BlockSpec((1, w), lambda i: (0, 0))],
        out_specs=pl.BlockSpec((tm, w), lambda i: (i, 0)),
        out_shape=jax.ShapeDtypeStruct((t, w), BF16),
        compiler_params=_params(("parallel",)),
        name="ssm_gated_norm",
    )(yf, yb, u, p, jnp.repeat(d_skip, SSM_HEAD_DIM).reshape(1, w), ssm_norm.reshape(1, w))


def _head_norm_rope_kernel(x_ref, nw_ref, cos_ref, sin_ref, o_ref, *, n_heads):
    for h in range(n_heads):
        sl = slice(h * LANE, (h + 1) * LANE)
        x = x_ref[:, sl].astype(F32)
        y = x * lax.rsqrt(jnp.mean(x * x, axis=-1, keepdims=True) + EPS) * nw_ref[...]
        o_ref[:, sl] = _rope(y, cos_ref[...], sin_ref[...]).astype(o_ref.dtype)


def _head_norm_rope(p, col0, n_heads, norm_w, cos, sin, seq):
    t = p.shape[0]
    tm = _pick(seq, (768, 512, 256))
    tpb = seq // tm
    w = n_heads * LANE
    return pl.pallas_call(
        functools.partial(_head_norm_rope_kernel, n_heads=n_heads),
        grid=(t // tm,),
        in_specs=[pl.BlockSpec((tm, w), lambda i: (i, col0 // w)),
                  pl.BlockSpec((1, LANE), lambda i: (0, 0)),
                  pl.BlockSpec((tm, LANE), lambda i: (i % tpb, 0)),
                  pl.BlockSpec((tm, LANE), lambda i: (i % tpb, 0))],
        out_specs=pl.BlockSpec((tm, w), lambda i: (i, 0)),
        out_shape=jax.ShapeDtypeStruct((t, w), BF16),
        compiler_params=_params(("parallel",)),
        name="head_norm_rope",
    )(p, norm_w.reshape(1, LANE), cos, sin)


def _mla_q_kernel(x_ref, nw_ref, w_ref, cos_ref, sin_ref, o_ref):
    x = x_ref[:, 0:MLA_Q_LORA].astype(F32)
    h = (x * lax.rsqrt(jnp.mean(x * x, axis=-1, keepdims=True) + EPS) * nw_ref[...]).astype(BF16)
    for hd in range(MLA_HEADS):
        c0 = hd * MLA_QK_PAD
        res = jnp.dot(h, w_ref[:, c0:c0 + MLA_QK_PAD], preferred_element_type=F32)
        o_ref[:, c0:c0 + LANE] = res[:, 0:LANE].astype(o_ref.dtype)
        o_ref[:, c0 + LANE:c0 + MLA_QK_PAD] = _rope(res[:, LANE:], cos_ref[...],
                                                    sin_ref[...]).astype(o_ref.dtype)


def _mla_q(p, q_norm, w_uq, cos, sin, seq):
    t = p.shape[0]
    tm = _pick(seq, (768, 512, 256))
    tpb = seq // tm
    kdim = MQA_PAD
    wo = MLA_HEADS * MLA_QK_PAD
    return pl.pallas_call(
        _mla_q_kernel,
        grid=(t // tm,),
        in_specs=[pl.BlockSpec((tm, kdim), lambda i: (i, P_MQA // kdim)),
                  pl.BlockSpec((1, MLA_Q_LORA), lambda i: (0, 0)),
                  pl.BlockSpec((MLA_Q_LORA, wo), lambda i: (0, 0)),
                  pl.BlockSpec((tm, LANE), lambda i: (i % tpb, 0)),
                  pl.BlockSpec((tm, LANE), lambda i: (i % tpb, 0))],
        out_specs=pl.BlockSpec((tm, wo), lambda i: (i, 0)),
        out_shape=jax.ShapeDtypeStruct((t, wo), BF16),
        compiler_params=_params(("parallel",)),
        name="mla_q_up",
    )(p, q_norm.reshape(1, MLA_Q_LORA), w_uq, cos, sin)


def _mla_kv_kernel(x_ref, kr_ref, nw_ref, w_ref, cos_ref, sin_ref, k_ref, v_ref):
    x = x_ref[...].astype(F32)
    h = (x * lax.rsqrt(jnp.mean(x * x, axis=-1, keepdims=True) + EPS) * nw_ref[...]).astype(BF16)
    k_rope = _rope(kr_ref[...].astype(F32), cos_ref[...], sin_ref[...]).astype(k_ref.dtype)
    for hd in range(MLA_HEADS):
        c0 = hd * MLA_QK_PAD
        res = jnp.dot(h, w_ref[:, c0:c0 + 2 * LANE], preferred_element_type=F32)
        k_ref[:, c0:c0 + LANE] = res[:, 0:LANE].astype(k_ref.dtype)
        k_ref[:, c0 + LANE:c0 + MLA_QK_PAD] = k_rope
        v_ref[:, hd * LANE:(hd + 1) * LANE] = res[:, LANE:].astype(v_ref.dtype)


def _mla_kv(p, kv_norm, w_ukv, cos, sin, seq):
    t = p.shape[0]
    tm = _pick(seq, (768, 512, 256))
    tpb = seq // tm
    kdim = MLA_KV_LORA
    wk = MLA_HEADS * MLA_QK_PAD
    return pl.pallas_call(
        _mla_kv_kernel,
        grid=(t // tm,),
        in_specs=[pl.BlockSpec((tm, kdim), lambda i: (i, P_MKVA // kdim)),
                  pl.BlockSpec((tm, LANE), lambda i: (i, P_MKR // LANE)),
                  pl.BlockSpec((1, kdim), lambda i: (0, 0)),
                  pl.BlockSpec((kdim, wk), lambda i: (0, 0)),
                  pl.BlockSpec((tm, LANE), lambda i: (i % tpb, 0)),
                  pl.BlockSpec((tm, LANE), lambda i: (i % tpb, 0))],
        out_specs=[pl.BlockSpec((tm, wk), lambda i: (i, 0)),
                   pl.BlockSpec((tm, MLA_HEADS * LANE), lambda i: (i, 0))],
        out_shape=[jax.ShapeDtypeStruct((t, wk), BF16),
                   jax.ShapeDtypeStruct((t, MLA_HEADS * LANE), BF16)],
        compiler_params=_params(("parallel",)),
        name="mla_kv_up",
    )(p, p, kv_norm.reshape(1, kdim), w_ukv, cos, sin)


def _flash_kernel(q_ref, k_ref, v_ref, g_ref, *rest, r, tq, dq, tk, n_keys, scale):
    o_ref = rest[-1]
    q = jnp.concatenate([q_ref[:, j * dq:(j + 1) * dq] for j in range(r)], axis=0)
    q = (q.astype(F32) * (scale * math.log2(math.e))).astype(BF16)
    m_rows = r * tq
    m = jnp.full((m_rows, 1), NEG_BIG, F32)
    l = jnp.zeros((m_rows, LANE), F32)
    acc = jnp.zeros((m_rows, LANE), F32)
    for c in range(n_keys // tk):
        k_c = k_ref[c * tk:(c + 1) * tk, :]
        v_c = v_ref[c * tk:(c + 1) * tk, :]
        s = lax.dot_general(q, k_c, (((1,), (1,)), ((), ())), preferred_element_type=F32)
        m_new = jnp.maximum(m, jnp.max(s, axis=-1, keepdims=True))
        alpha = jnp.exp2(m - m_new)
        pr = jnp.exp2(s - m_new)
        part = pr[:, 0:LANE]
        for j in range(1, tk // LANE):
            part = part + pr[:, j * LANE:(j + 1) * LANE]
        l = alpha * l + part
        acc = alpha * acc + jnp.dot(pr.astype(BF16), v_c, preferred_element_type=F32)
        m = m_new
    o = acc / jnp.sum(l, axis=-1, keepdims=True)
    for j in range(r):
        gate = _silu(g_ref[:, j * LANE:(j + 1) * LANE].astype(F32))
        o_ref[:, j * LANE:(j + 1) * LANE] = (o[j * tq:(j + 1) * tq] * gate).astype(o_ref.dtype)


def _flash(q, k, v, gates, *, q_col0, k_col0, v_col0, g_col0, v_stride, n_kv_heads, r, dq, scale,
           batch, n_lat, seq, ctx_only, tq, into=None):
    n_ctx = seq - n_lat
    tk = FLASH_TK
    if ctx_only:
        tq = n_ctx
        q_blk0, n_q, kv_rows, kv_blk0 = n_lat // tq, 1, n_ctx, n_lat // n_ctx
    else:
        q_blk0, n_q, kv_rows, kv_blk0 = 0, n_lat // tq, seq, 0
    assert kv_rows % tk == 0
    kern = functools.partial(_flash_kernel, r=r, tq=tq, dq=dq, tk=tk, n_keys=kv_rows, scale=scale)
    w_out = n_kv_heads * r * LANE
    args = [a.reshape(batch, seq, a.shape[-1]) for a in (q, k, v, gates)]
    in_specs = [pl.BlockSpec((None, tq, r * dq), lambda b, g, i: (b, q_blk0 + i, q_col0 // (r * dq) + g)),
                pl.BlockSpec((None, kv_rows, dq), lambda b, g, i: (b, kv_blk0, k_col0 // dq + g)),
                pl.BlockSpec((None, kv_rows, LANE),
                             lambda b, g, i: (b, kv_blk0, v_col0 // LANE + v_stride * g)),
                pl.BlockSpec((None, tq, r * LANE),
                             lambda b, g, i: (b, q_blk0 + i, g_col0 // (r * LANE) + g))]
    aliases = {}
    if into is not None:
        args.append(into.reshape(batch, seq, w_out))
        in_specs.append(pl.BlockSpec(memory_space=pl.ANY))
        aliases = {len(args) - 1: 0}
    out = pl.pallas_call(
        kern,
        grid=(batch, n_kv_heads, n_q),
        in_specs=in_specs,
        out_specs=pl.BlockSpec((None, tq, r * LANE), lambda b, g, i: (b, q_blk0 + i, g)),
        out_shape=jax.ShapeDtypeStruct((batch, seq, w_out), BF16),
        input_output_aliases=aliases,
        compiler_params=_params(("parallel", "parallel", "parallel")),
        name="flash_ctx" if ctx_only else "flash_latent",
    )(*args)
    return out.reshape(batch * seq, w_out)


def _attention(q, k, v, gates, tq, **kw):
    lat = _flash(q, k, v, gates, ctx_only=False, tq=tq, **kw)
    return _flash(q, k, v, gates, ctx_only=True, tq=tq, into=lat, **kw)


def _natten_kernel(q_ref, k0_ref, k1_ref, k2_ref, kc_ref, v0_ref, v1_ref, v2_ref, vc_ref, g_ref,
                   bias_ref, o_ref, *, scale, n_heads):
    n_loc = bias_ref.shape[-1]
    for h in range(n_heads):
        sl = slice(h * LANE, (h + 1) * LANE)
        q = (q_ref[:, sl].astype(F32) * (scale * math.log2(math.e))).astype(BF16)
        k = jnp.concatenate([k0_ref[:, sl], k1_ref[:, sl], k2_ref[:, sl], kc_ref[:, sl]], axis=0)
        v = jnp.concatenate([v0_ref[:, sl], v1_ref[:, sl], v2_ref[:, sl], vc_ref[:, sl]], axis=0)
        s = lax.dot_general(q, k, (((1,), (1,)), ((), ())), preferred_element_type=F32)
        s = jnp.concatenate([s[:, :n_loc] + bias_ref[h], s[:, n_loc:]], axis=1)
        pr = jnp.exp2(s - jnp.max(s, axis=-1, keepdims=True)).astype(BF16)
        ov = jnp.dot(pr, jnp.concatenate([v, jnp.ones_like(v)], axis=1), preferred_element_type=F32)
        o = ov[:, 0:LANE] / ov[:, LANE:]
        o_ref[:, sl] = (o * _silu(g_ref[:, sl].astype(F32))).astype(o_ref.dtype)


def _na_bias_table(rpb):
    qx = np.arange(GRID_W)
    c0 = np.clip(qx - NA_WIN_W // 2, 0, GRID_W - NA_WIN_W)
    col_ok = (qx[None, :] >= c0[:, None]) & (qx[None, :] < c0[:, None] + NA_WIN_W)
    dx = qx[None, :] - qx[:, None] + NA_WIN_W - 1
    pick = np.zeros((2 * NA_WIN_W - 1, GRID_W * GRID_W), np.float32)
    qi, ki = np.nonzero(col_ok)
    pick[dx[qi, ki], qi * GRID_W + ki] = 1.0
    by_dx = jnp.einsum("...d,dn->...n", rpb, pick, precision=lax.Precision.HIGHEST)
    qy, ky = np.arange(NA_QROWS), np.arange(NA_KROWS)
    dys, oks = [], []
    for q_off, first_key in ((0, None), (NA_WIN_H // 2, "q"), (NA_WIN_H, NA_KROWS - NA_WIN_H)):
        r0 = qy if first_key == "q" else np.full_like(qy, 0 if first_key is None else first_key)
        oks.append((ky[None, :] >= r0[:, None]) & (ky[None, :] < r0[:, None] + NA_WIN_H))
        dys.append(np.clip(ky[None, :] - (qy[:, None] + q_off) + NA_WIN_H - 1, 0, 2 * NA_WIN_H - 2))
    rows = jnp.take(by_dx, np.stack(dys).reshape(-1), axis=-2)
    lead = rows.shape[:-2]
    nl = len(lead)
    rows = rows.reshape(lead + (3, NA_QROWS, NA_KROWS, GRID_W, GRID_W))
    rows = rows.transpose(tuple(range(nl)) + (nl, nl + 1, nl + 3, nl + 2, nl + 4))
    ok = np.stack(oks)[:, :, None, :, None] & col_ok[None, None, :, None, :]
    tab = jnp.where(jnp.asarray(ok), rows * math.log2(math.e), NEG_BIG)
    return tab.reshape(lead + (3, NA_QROWS * GRID_W, NA_KROWS * GRID_W)).astype(BF16)


def _natten(p, bias, batch, n_lat, seq):
    n_ctx = seq - n_lat
    tq = NA_QROWS * GRID_W
    n_blk = n_lat // tq
    assert n_blk >= 3 and n_lat % tq == 0 and n_ctx % tq == 0
    rb = seq // tq
    cb = seq // n_ctx
    nk = NA_KROWS * GRID_W
    hps = NA_HEADS_PER_STEP
    hw = hps * LANE

    def kblk(i):
        return jnp.clip(i - 1, 0, n_blk - 3)

    def kspec(col0, j):
        return pl.BlockSpec((tq, hw), lambda b, h, i: (b * rb + kblk(i) + j, col0 // hw + h))

    def cspec(col0):
        return pl.BlockSpec((n_ctx, hw), lambda b, h, i: (b * cb + n_lat // n_ctx, col0 // hw + h))

    def qspec(col0):
        return pl.BlockSpec((tq, hw), lambda b, h, i: (b * rb + i, col0 // hw + h))

    def btype(i):
        return jnp.where(i == 0, 0, jnp.where(i == n_blk - 1, 2, 1))

    return pl.pallas_call(
        functools.partial(_natten_kernel, scale=HEAD_DIM ** -0.5, n_heads=hps),
        grid=(batch, NA_HEADS // hps, n_blk),
        in_specs=[qspec(P_NQ), kspec(P_NK, 0), kspec(P_NK, 1), kspec(P_NK, 2), cspec(P_NK),
                  kspec(P_NV, 0), kspec(P_NV, 1), kspec(P_NV, 2), cspec(P_NV), qspec(P_NG),
                  pl.BlockSpec((hps, None, tq, nk), lambda b, h, i: (h, btype(i), 0, 0))],
        out_specs=pl.BlockSpec((tq, hw), lambda b, h, i: (b * rb + i, h)),
        out_shape=jax.ShapeDtypeStruct((p.shape[0], NA_HEADS * LANE), BF16),
        compiler_params=_params(("parallel", "parallel", "parallel")),
        name="natten_latent",
    )(p, p, p, p, p, p, p, p, p, p, bias)


def _merge_kernel(o0_ref, o1_ref, o2_ref, o3_ref, w0_ref, w1_ref, w2_ref, w3_ref,
                  m0_ref, m1_ref, m2_ref, m3_ref, y_ref):
    acc = None
    for o_ref, w_ref, m_ref in ((o0_ref, w0_ref, m0_ref), (o1_ref, w1_ref, m1_ref),
                                (o2_ref, w2_ref, m2_ref), (o3_ref, w3_ref, m3_ref)):
        gate = _sigmoid(m_ref[...].astype(F32))
        term = gate * jnp.dot(o_ref[...], w_ref[...], preferred_element_type=F32)
        acc = term if acc is None else acc + term
    y_ref[...] = acc.astype(y_ref.dtype)


def _merge(outs, weights, p, seq):
    t = p.shape[0]
    d = weights[0].shape[1]
    tm = _pick(seq, (1056, 768, 512, 256))
    tn = 512
    o_specs =[pl.BlockSpec((tm, o.shape[1]), lambda i, j: (i, 0)) for o in outs]
    w_specs = [pl.BlockSpec((w.shape[0], tn), lambda i, j: (0, j)) for w in weights]
    m_specs = [pl.BlockSpec((tm, tn), lambda i, j, b=b: (i, (P_MIX + b * d) // tn + j))
               for b in range(N_BRANCH)]
    return pl.pallas_call(
        _merge_kernel,
        grid=(t // tm, d // tn),
        in_specs=o_specs + w_specs + m_specs,
        out_specs=pl.BlockSpec((tm, tn), lambda i, j: (i, j)),
        out_shape=jax.ShapeDtypeStruct((t, d), BF16),
        compiler_params=_params(("parallel", "parallel")),
        name="branch_merge",
    )(*outs, *weights, p, p, p, p)


def _out_kernel(y_ref, w_ref, x_ref, nw_ref, gate_ref, o_ref, *, tm, tiles_per_batch, n_lat):
    rc = 256 if tm % 256 == 0 else tm
    for r0 in range(0, tm, rc):
        rows = slice(r0, r0 + rc)
        z = jnp.dot(y_ref[rows, :], w_ref[...], preferred_element_type=F32)
        zn = z * lax.rsqrt(jnp.mean(z * z, axis=-1, keepdims=True) + EPS) * nw_ref[...]
        if tiles_per_batch is None:
            gate = gate_ref[0:1, :]
        else:
            row = (pl.program_id(0) % tiles_per_batch) * tm + r0 + lax.broadcasted_iota(
                jnp.int32, (rc, 1), 0)
            gate = jnp.where(row >= n_lat, gate_ref[1:2, :], gate_ref[0:1, :])
        o_ref[rows, :] = x_ref[rows, :] + gate * zn


def _out_projection_latent(y, w_out, xa, norm_w, gate, batch, n_lat, seq):
    d = xa.shape[1]
    tm = _pick(n_lat, (512, 256, 128))
    kern = functools.partial(_out_kernel, tm=tm, tiles_per_batch=None, n_lat=n_lat)
    return pl.pallas_call(
        kern,
        grid=(batch, n_lat // tm),
        in_specs=[pl.BlockSpec((None, tm, d), lambda b, i: (b, i, 0)),
                  pl.BlockSpec((d, d), lambda b, i: (0, 0)),
                  pl.BlockSpec((None, tm, d), lambda b, i: (b, i, 0)),
                  pl.BlockSpec((1, d), lambda b, i: (0, 0)),
                  pl.BlockSpec((None, 2, d), lambda b, i: (b, 0, 0))],
        out_specs=pl.BlockSpec((None, tm, d), lambda b, i: (b, i, 0)),
        out_shape=jax.ShapeDtypeStruct((batch, n_lat, d), F32),
        compiler_params=_params(("parallel", "parallel")),
        name="out_proj_last",
    )(y.reshape(batch, seq, d), w_out, xa.reshape(batch, seq, d), norm_w.reshape(1, d), gate)


def _out_projection(y, w_out, xa, norm_w, gate, n_lat, seq):
    t, d = xa.shape
    tm = _pick(seq, (768, 384, 256, 128))
    tpb = seq // tm
    kern = functools.partial(_out_kernel, tm=tm, tiles_per_batch=tpb, n_lat=n_lat)
    return pl.pallas_call(
        kern,
        grid=(t // tm,),
        in_specs=[pl.BlockSpec((tm, d), lambda i: (i, 0)),
                  pl.BlockSpec((d, d), lambda i: (0, 0)),
                  pl.BlockSpec((tm, d), lambda i: (i, 0)),
                  pl.BlockSpec((1, d), lambda i: (0, 0)),
                  pl.BlockSpec((None, 2, d), lambda i: (i // tpb, 0, 0))],
        out_specs=pl.BlockSpec((tm, d), lambda i: (i, 0)),
        out_shape=jax.ShapeDtypeStruct((t, d), F32),
        compiler_params=_params(("parallel",)),
        name="out_proj",
    )(y, w_out, xa, norm_w.reshape(1, d), gate)


def _pairs_apart(w, n_heads, dim):
    lead = w.shape[:-1]
    return w.reshape(lead + (n_heads, dim // 2, 2)).swapaxes(-1, -2).reshape(lead + (n_heads * dim,))


def _rope_tile(w):
    lead = w.shape[:-1]
    pr = w.reshape(lead + (MLA_ROPE // 2, 2))
    zero = jnp.zeros(lead + (MLA_ROPE // 2,), w.dtype)
    return jnp.concatenate([pr[..., 0], zero, pr[..., 1], zero], axis=-1)


def _layout_w_in(w):
    o = _OFF
    k = w.shape[0]

    def seg(name, width):
        return w[:, o[name]:o[name] + width]

    cols = [seg("mix", 8192), seg("z", 2048), seg("xbc", SSM_CONV_DIM),
            _pairs_apart(seg("gq", 1024), GQA_HEADS, HEAD_DIM), seg("gg", 1024),
            seg("nq", 1024), seg("nk", 1024), seg("nv", 1024), seg("ng", 1024), seg("mg", 1024),
            seg("mqa", MLA_Q_LORA), _rope_tile(seg("mkr", MLA_ROPE)),
            jnp.zeros((k, MQA_PAD - MLA_Q_LORA - LANE), w.dtype),
            _pairs_apart(seg("gk", 512), GQA_KV_HEADS, HEAD_DIM), seg("gv", 512),
            seg("mkva", MLA_KV_LORA)]
    main = jnp.concatenate(cols, axis=1).astype(BF16)
    assert main.shape[1] == P_WIDTH
    dtr = seg("dtr", 2 * SSM_HEADS)
    zero = jnp.zeros((k, LANE - SSM_HEADS), w.dtype)
    side = jnp.concatenate([dtr[:, :SSM_HEADS], zero, dtr[:, SSM_HEADS:], zero], axis=1).astype(BF16)
    return main, side


def _layout_w_uq(w_uq):
    k = w_uq.shape[0]
    w = w_uq.reshape(k, MLA_HEADS, MLA_NOPE + MLA_ROPE)
    w = jnp.concatenate([w[..., :MLA_NOPE], _rope_tile(w[..., MLA_NOPE:])], axis=-1)
    return w.reshape(k, MLA_HEADS * MLA_QK_PAD).astype(BF16)


def _rope_tables(n_lat, n_ctx, dim):
    t = np.arange(n_lat)
    quarter = dim // 4
    freqs = ROPE_THETA ** (-jnp.arange(quarter, dtype=F32) / quarter)
    row = jnp.asarray(t // GRID_W, F32)
    col = jnp.asarray(t % GRID_W, F32)
    ang = jnp.concatenate([row[:, None] * freqs, col[:, None] * freqs], axis=-1)
    cos, sin = jnp.cos(ang), jnp.sin(ang)
    pad = 64 - dim // 2
    one, zero = jnp.ones((n_lat, pad), F32), jnp.zeros((n_lat, pad), F32)
    cos_t = jnp.concatenate([cos, one, cos, one], axis=-1)
    sin_t = jnp.concatenate([-sin, zero, sin, zero], axis=-1)
    cos_t = jnp.concatenate([cos_t, jnp.ones((n_ctx, LANE), F32)], axis=0)
    sin_t = jnp.concatenate([sin_t, jnp.zeros((n_ctx, LANE), F32)], axis=0)
    return cos_t, sin_t


def _layer(xa, mod, rope_g, rope_m, lp, batch, n_lat, seq, last):
    d = xa.shape[1]

    def per_row(v):
        return jnp.stack([v[:batch], jnp.broadcast_to(v[batch:batch + 1], (batch, d))], axis=1)

    shift, scale, gate = (per_row(mod[:, k * d:(k + 1) * d]) for k in range(3))
    w_main, w_side = _layout_w_in(lp["w_in"])
    p, dtr = _in_projection(xa, lp["norm_pre"], scale, shift, w_main, w_side, n_lat, seq)

    u = _conv_silu(p, lp["conv_w"], lp["conv_b"], n_lat, seq)
    yf, yb = _ssd(u, dtr, lp["a_log"], lp["dt_bias"], batch, n_lat, seq)
    o_ssm = _ssm_out(yf, yb, u, p, lp["d_skip"], lp["ssm_norm"], seq)

    common = dict(batch=batch, n_lat=n_lat, seq=seq)
    qg = _head_norm_rope(p, P_GQ, GQA_HEADS, _pairs_apart(lp["gqa_q_norm"], 1, HEAD_DIM), *rope_g, seq)
    kg = _head_norm_rope(p, P_GK, GQA_KV_HEADS, _pairs_apart(lp["gqa_k_norm"], 1, HEAD_DIM), *rope_g, seq)
    o_gqa = _attention(
        qg, kg, p, p, GQA_TQ, q_col0=0, k_col0=0, v_col0=P_GV, g_col0=P_GG, v_stride=1,
        n_kv_heads=GQA_KV_HEADS, r=GQA_HEADS // GQA_KV_HEADS, dq=HEAD_DIM, scale=HEAD_DIM ** -0.5,
        **common)

    na_lat = _natten(p, lp["na_bias"], **common)
    o_na = _flash(p, p, p, p, q_col0=P_NQ, k_col0=P_NK, v_col0=P_NV, g_col0=P_NG, v_stride=1,
                  n_kv_heads=NA_HEADS, r=1, dq=HEAD_DIM, scale=HEAD_DIM ** -0.5, ctx_only=True,
                  tq=256, into=na_lat, **common)

    qm = _mla_q(p, lp["mla_q_norm"], _layout_w_uq(lp["w_uq"]), *rope_m, seq)
    km, vm = _mla_kv(p, lp["mla_kv_norm"], lp["w_ukv"].astype(BF16), *rope_m, seq)
    o_mla = _attention(
        qm, km, vm, p, MLA_TQ, q_col0=0, k_col0=0, v_col0=0, g_col0=P_MG, v_stride=1,
        n_kv_heads=MLA_HEADS, r=1, dq=MLA_QK_PAD, scale=(MLA_NOPE + MLA_ROPE) ** -0.5,
        **common)

    weights = [lp[n].astype(BF16) for n in ("w_o_ssm", "w_o_gqa", "w_o_na", "w_o_mla")]
    ymix = _merge([o_ssm, o_gqa, o_na, o_mla], weights, p, seq)
    w_out = lp["w_out"].astype(BF16)
    if last:
        return _out_projection_latent(ymix, w_out, xa, lp["norm_post"], gate, batch, n_lat, seq)
    return _out_projection(ymix, w_out, xa, lp["norm_post"], gate, n_lat, seq)


def kernel(x, c, ctx, c_ctx, ada_w, ada_b, norm_pre, norm_post, w_in, conv_w, conv_b, a_log, dt_bias,
           d_skip, ssm_norm, w_o_ssm, gqa_q_norm, gqa_k_norm, w_o_gqa, na_rpb, w_o_na, mla_q_norm,
           w_uq, mla_kv_norm, w_ukv, w_o_mla, w_out):
    batch, n_lat, d = x.shape
    n_ctx = ctx.shape[1]
    seq = n_lat + n_ctx
    stacked = dict(norm_pre=norm_pre, norm_post=norm_post, w_in=w_in,
                   conv_w=conv_w, conv_b=conv_b, a_log=a_log, dt_bias=dt_bias, d_skip=d_skip,
                   ssm_norm=ssm_norm, w_o_ssm=w_o_ssm, gqa_q_norm=gqa_q_norm, gqa_k_norm=gqa_k_norm,
                   w_o_gqa=w_o_gqa, na_bias=_na_bias_table(na_rpb), w_o_na=w_o_na,
                   mla_q_norm=mla_q_norm, w_uq=w_uq,
                   mla_kv_norm=mla_kv_norm, w_ukv=w_ukv, w_o_mla=w_o_mla, w_out=w_out)
    xa = jnp.concatenate([x, ctx], axis=1).reshape(batch * seq, d)
    cc = jnp.concatenate([c, c_ctx[None, :], jnp.zeros((8 - batch - 1, d), c.dtype)], axis=0)
    rope_g = _rope_tables(n_lat, n_ctx, HEAD_DIM)
    rope_m = _rope_tables(n_lat, n_ctx, MLA_ROPE)
    depth = ada_w.shape[0]
    mods = _modulation(cc, ada_w, ada_b)
    for layer in range(depth):
        lp = {k: v[layer] for k, v in stacked.items()}
        xa = _layer(xa, mods[layer], rope_g, rope_m, lp, batch, n_lat, seq, last=layer == depth - 1)
    return xa
```

```python
import functools
import math

import jax
import jax.numpy as jnp
import numpy as np
from jax import lax
from jax.experimental import pallas as pl
from jax.experimental.pallas import tpu as pltpu

F32 = jnp.float32
BF16 = jnp.bfloat16

GRID_W = 64
EPS = 1e-6
ROPE_THETA = 10000.0

SSM_HEADS = 32
SSM_HEAD_DIM = 64
SSM_INNER = SSM_HEADS * SSM_HEAD_DIM
SSM_GROUPS = 4
SSM_STATE = 128
SSM_CONV = 5
SSM_CHUNK = 128
SSM_CONV_DIM = SSM_INNER + 2 * SSM_GROUPS * SSM_STATE

GQA_HEADS = 8
GQA_KV_HEADS = 4
HEAD_DIM = 128
NA_HEADS = 8
NA_WIN_H = 8
NA_WIN_W = 16
NA_QROWS = 4
NA_KROWS = NA_QROWS + NA_WIN_H
NA_HEADS_PER_STEP = 8

MLA_HEADS = 8
MLA_Q_LORA = 768
MLA_KV_LORA = 512
MLA_NOPE = 128
MLA_ROPE = 64
MLA_QK_PAD = 256

N_BRANCH = 4
LANE = 128
VMEM_LIMIT = 56 * 1024 * 1024
NEG_BIG = -1e30
FLASH_TK = 256
GQA_TQ = 512
MLA_TQ = 1024

_SIZES = (SSM_INNER, SSM_CONV_DIM, 2 * SSM_HEADS, 1024, 512, 512, 1024, 1024, 1024, 1024, 1024,
          MLA_Q_LORA, MLA_KV_LORA, MLA_ROPE, 1024, N_BRANCH * 2048)
_OFF = dict(zip(("z", "xbc", "dtr", "gq", "gk", "gv", "gg", "nq", "nk", "nv", "ng",
                 "mqa", "mkva", "mkr", "mg", "mix"), np.cumsum((0,) + _SIZES[:-1]).tolist()))

P_MIX, P_Z, P_XBC = 0, 8192, 10240
P_GQ, P_GG, P_NQ, P_NK, P_NV, P_NG, P_MG, P_MQA = (13312, 14336, 15360, 16384, 17408, 18432,
                                                    19456, 20480)
P_GK, P_GV, P_MKVA = 21504, 22016, 22528
P_WIDTH = 23040
MQA_PAD = 1024
P_MKR = P_MQA + MLA_Q_LORA


def _pick(n, candidates):
    for c in candidates:
        if n % c == 0:
            return c
    raise ValueError(f"no tile for {n} among {candidates}")


def _params(sem):
    return pltpu.CompilerParams(dimension_semantics=sem, vmem_limit_bytes=VMEM_LIMIT)


def _sigmoid(v):
    return 0.5 + 0.5 * jnp.tanh(0.5 * v)


def _silu(v):
    return v * _sigmoid(v)


def _rope(v, cos, sin):
    return v * cos + pltpu.roll(v, 64, axis=1) * sin


def _mod_kernel(c_ref, w_ref, b_ref, o_ref):
    h = _silu(c_ref[...]).astype(BF16)
    o_ref[...] = jnp.dot(h, w_ref[...].astype(BF16), preferred_element_type=F32) + b_ref[...]


def _modulation(cc, ada_w, ada_b):
    rows, d = cc.shape
    depth, _, n = ada_w.shape
    tn = _pick(n, (512, 256, 128))
    return pl.pallas_call(
        _mod_kernel,
        grid=(depth, n // tn),
        in_specs=[pl.BlockSpec((rows, d), lambda l, j: (0, 0)),
                  pl.BlockSpec((None, d, tn), lambda l, j: (l, 0, j)),
                  pl.BlockSpec((None, 1, tn), lambda l, j: (l, 0, j))],
        out_specs=pl.BlockSpec((None, rows, tn), lambda l, j: (l, 0, j)),
        out_shape=jax.ShapeDtypeStruct((depth, rows, n), F32),
        compiler_params=_params(("parallel", "parallel")),
        name="adaln_mod",
    )(cc, ada_w, ada_b.reshape(depth, 1, n))


def _inproj_kernel(x_ref, nw_ref, sc_ref, sh_ref, w_ref, ws_ref, o_ref, os_ref, h_ref, *,
                   tm, tiles_per_batch, n_lat):
    i = pl.program_id(0)
    n_chunk = 3 if tm % 48 == 0 else 1
    rc = tm // n_chunk

    @pl.when(pl.program_id(1) == 0)
    def _():
        row0 = (i % tiles_per_batch) * tm
        for c in range(n_chunk):
            rows = slice(c * rc, (c + 1) * rc)
            x = x_ref[rows, :]
            y = x * lax.rsqrt(jnp.mean(x * x, axis=-1, keepdims=True) + EPS) * nw_ref[...]
            is_ctx = row0 + c * rc + lax.broadcasted_iota(jnp.int32, (rc, 1), 0) >= n_lat
            sc = jnp.where(is_ctx, sc_ref[1:2, :], sc_ref[0:1, :])
            sh = jnp.where(is_ctx, sh_ref[1:2, :], sh_ref[0:1, :])
            h = (y * (1.0 + sc) + sh).astype(BF16)
            h_ref[rows, :] = h
            os_ref[rows, :] = jnp.dot(h, ws_ref[...], preferred_element_type=F32)
            o_ref[rows, :] = jnp.dot(h, w_ref[...], preferred_element_type=F32).astype(o_ref.dtype)

    @pl.when(pl.program_id(1) != 0)
    def _():
        o_ref[...] = jnp.dot(h_ref[...], w_ref[...], preferred_element_type=F32).astype(o_ref.dtype)


def _in_projection(xa, norm_w, scale, shift, w_p, w_side, n_lat, seq):
    t, d = xa.shape
    tm = _pick(seq, (1056, 768, 512, 256))
    tn = _pick(P_WIDTH, (1536, 1024, 512))
    tpb = seq // tm
    kern = functools.partial(_inproj_kernel, tm=tm, tiles_per_batch=tpb, n_lat=n_lat)
    return pl.pallas_call(
        kern,
        grid=(t // tm, P_WIDTH // tn),
        in_specs=[pl.BlockSpec((tm, d), lambda i, j: (i, 0)),
                  pl.BlockSpec((1, d), lambda i, j: (0, 0)),
                  pl.BlockSpec((None, 2, d), lambda i, j: (i // tpb, 0, 0)),
                  pl.BlockSpec((None, 2, d), lambda i, j: (i // tpb, 0, 0)),
                  pl.BlockSpec((d, tn), lambda i, j: (0, j)),
                  pl.BlockSpec((d, 2 * LANE), lambda i, j: (0, 0))],
        out_specs=[pl.BlockSpec((tm, tn), lambda i, j: (i, j)),
                   pl.BlockSpec((tm, 2 * LANE), lambda i, j: (i, 0))],
        out_shape=[jax.ShapeDtypeStruct((t, P_WIDTH), BF16),
                   jax.ShapeDtypeStruct((t, 2 * LANE), F32)],
        scratch_shapes=[pltpu.VMEM((tm, d), BF16)],
        compiler_params=_params(("parallel", "arbitrary")),
        name="in_proj",
    )(xa, norm_w.reshape(1, d), scale, shift, w_p, w_side)


def _conv_kernel(u_ref, prev_ref, next_ref, w_ref, b_ref, s_ref, o_ref, *,
                 tm, tiles_per_batch, lat_tiles):
    ib = pl.program_id(0) % tiles_per_batch
    first = jnp.logical_or(ib == 0, ib == lat_tiles)
    last = jnp.logical_or(ib == lat_tiles - 1, ib == tiles_per_batch - 1)
    half = SSM_CONV // 2
    row = lax.broadcasted_iota(jnp.int32, (8, 1), 0)
    cw = 2 * LANE
    for c0 in range(0, u_ref.shape[1], cw):
        sl = slice(c0, c0 + cw)
        u = u_ref[:, sl]
        acc = b_ref[:, sl] + w_ref[half:half + 1, sl] * u.astype(F32)
        for idx, k in enumerate(k for k in range(SSM_CONV) if k != half):
            acc = acc + w_ref[k:k + 1, sl] * jnp.dot(s_ref[idx], u, preferred_element_type=F32)
        prev = jnp.where(first, 0.0, prev_ref[:, sl].astype(F32))
        nxt = jnp.where(last, 0.0, next_ref[:, sl].astype(F32))
        top = jnp.zeros_like(prev)
        bot = jnp.zeros_like(nxt)
        for k in range(half):
            reach = half - k
            top = top + w_ref[k:k + 1, sl] * jnp.where(row < reach, pltpu.roll(prev, reach, axis=0), 0.0)
            kk = SSM_CONV - 1 - k
            bot = bot + w_ref[kk:kk + 1, sl] * jnp.where(row >= 8 - reach,
                                                         pltpu.roll(nxt, 8 - reach, axis=0), 0.0)
        y = jnp.concatenate([acc[0:8] + top, acc[8:tm - 8], acc[tm - 8:tm] + bot], axis=0)
        o_ref[:, sl] = _silu(y).astype(o_ref.dtype)


def _conv_silu(p, conv_w, conv_b, n_lat, seq):
    t = p.shape[0]
    tm = 256
    tc = 1024
    tpb = seq // tm
    n_row8 = t // 8
    col0 = P_XBC // tc
    kern = functools.partial(_conv_kernel, tm=tm, tiles_per_batch=tpb, lat_tiles=n_lat // tm)
    half = SSM_CONV // 2
    shifts = np.stack([np.eye(tm, k=k - half, dtype=np.float32) for k in range(SSM_CONV) if k != half])
    return pl.pallas_call(
        kern,
        grid=(t // tm, SSM_CONV_DIM // tc),
        in_specs=[pl.BlockSpec((tm, tc), lambda i, j: (i, col0 + j)),
                  pl.BlockSpec((8, tc), lambda i, j: (jnp.maximum(i * (tm // 8) - 1, 0), col0 + j)),
                  pl.BlockSpec((8, tc), lambda i, j: (jnp.minimum((i + 1) * (tm // 8), n_row8 - 1),
                                                      col0 + j)),
                  pl.BlockSpec((8, tc), lambda i, j: (0, j)),
                  pl.BlockSpec((1, tc), lambda i, j: (0, j)),
                  pl.BlockSpec((SSM_CONV - 1, tm, tm), lambda i, j: (0, 0, 0))],
        out_specs=pl.BlockSpec((tm, tc), lambda i, j: (i, j)),
        out_shape=jax.ShapeDtypeStruct((t, SSM_CONV_DIM), BF16),
        compiler_params=_params(("parallel", "parallel")),
        name="ssm_conv",
    )(p, p, p, jnp.pad(conv_w, ((0, 8 - SSM_CONV), (0, 0))), conv_b.reshape(1, -1),
      jnp.asarray(shifts, BF16))


def _split_dot(a_bf16, v):
    v1 = v.astype(BF16)
    r1 = v - v1.astype(F32)
    v2 = r1.astype(BF16)
    v3 = (r1 - v2.astype(F32)).astype(BF16)
    return jnp.dot(jnp.concatenate([a_bf16, a_bf16, a_bf16], axis=1),
                   jnp.concatenate([v1, v2, v3], axis=0), preferred_element_type=F32)


def _ssd_kernel(uf_ref, ub_ref, dtf_ref, dtb_ref, alog_ref, bias_ref, eh_ref, yf_ref, yb_ref,
                stf_ref, stb_ref):
    @pl.when(pl.program_id(1) == 0)
    def _():
        stf_ref[...] = jnp.zeros_like(stf_ref)
        stb_ref[...] = jnp.zeros_like(stb_ref)

    _ssd_chunk(True, uf_ref, dtf_ref, alog_ref[0], bias_ref[0], eh_ref, yf_ref, stf_ref)
    _ssd_chunk(False, ub_ref, dtb_ref, alog_ref[1], bias_ref[1], eh_ref, yb_ref, stb_ref)


def _ssd_chunk(fwd, u_ref, dtr_ref, a_log, dt_bias, eh_ref, y_ref, st_ref):
    cl = SSM_CHUNK
    raw = dtr_ref[...] + dt_bias
    dt = jnp.maximum(raw, 0.0) + jnp.log(1.0 + jnp.exp(-jnp.abs(raw)))
    da = dt * (-jnp.exp(a_log))
    r = lax.broadcasted_iota(jnp.int32, (cl, cl), 0)
    c = lax.broadcasted_iota(jnp.int32, (cl, cl), 1)
    tri = (r >= c) if fwd else (r <= c)
    acs = _split_dot(jnp.where(tri, 1.0, 0.0).astype(BF16), da)
    total = acs[cl - 1:cl, :] if fwd else acs[0:1, :]
    e_acs = jnp.exp(acs)
    w_end = dt * jnp.exp(total - acs)
    acs_t = acs.T
    dt_t = dt.T

    def hi_lo(v):
        hi = v.astype(BF16)
        return jnp.concatenate([hi, (v - hi.astype(F32)).astype(BF16)], axis=1)

    ex = jnp.dot(jnp.concatenate([hi_lo(w_end), hi_lo(e_acs)], axis=0), eh_ref[...],
                 preferred_element_type=F32)
    w_exp = ex[0:cl]
    e_exp = ex[cl:2 * cl]
    dec = e_exp[cl - 1:cl, :] if fwd else e_exp[0:1, :]
    xw =(u_ref[:, 0:SSM_INNER].astype(F32) * w_exp).astype(BF16)
    lane = lax.broadcasted_iota(jnp.int32, (cl, LANE), 1)
    gw = SSM_INNER // SSM_GROUPS
    hpg = SSM_HEADS // SSM_GROUPS
    for g in range(SSM_GROUPS):
        b_g = u_ref[:, SSM_INNER + g * SSM_STATE:SSM_INNER + (g + 1) * SSM_STATE]
        c_off = SSM_INNER + SSM_GROUPS * SSM_STATE
        c_g = u_ref[:, c_off + g * SSM_STATE:c_off + (g + 1) * SSM_STATE]
        cb = lax.dot_general(c_g, b_g, (((1,), (1,)), ((), ())), preferred_element_type=F32)
        st_g = st_ref[:, g * gw:(g + 1) * gw]
        y_off = jnp.dot(c_g, st_g.astype(BF16), preferred_element_type=F32)
        ys = []
        for k in range(hpg // 2):
            ms = []
            for h in (g * hpg + 2 * k, g * hpg + 2 * k + 1):
                seg = acs[:, h:h + 1] - acs_t[h:h + 1, :]
                dec_h = jnp.exp(jnp.where(tri, seg, NEG_BIG))
                ms.append((dec_h * cb * dt_t[h:h + 1, :]).astype(BF16))
            x_p = u_ref[:, g * gw + k * LANE:g * gw + (k + 1) * LANE]
            zero = jnp.zeros_like(x_p)
            rhs = jnp.concatenate([jnp.where(lane < SSM_HEAD_DIM, x_p, zero),
                                   jnp.where(lane >= SSM_HEAD_DIM, x_p, zero)], axis=0)
            ys.append(jnp.dot(jnp.concatenate(ms, axis=1), rhs, preferred_element_type=F32))
        y_g = jnp.concatenate(ys, axis=1) + y_off * e_exp[:, g * gw:(g + 1) * gw]
        y_ref[:, g * gw:(g + 1) * gw] = y_g.astype(y_ref.dtype)
        upd = lax.dot_general(b_g, xw[:, g * gw:(g + 1) * gw], (((0,), (0,)), ((), ())),
                              preferred_element_type=F32)
        st_ref[:, g * gw:(g + 1) * gw] = st_g * dec[:, g * gw:(g + 1) * gw] + upd


def _ssd(u, dtr, a_log, dt_bias, batch, n_lat, seq):
    cl = SSM_CHUNK
    nch, nlat, nctx = seq // cl, n_lat // cl, (seq - n_lat) // cl

    def fchunk(c):
        return jnp.where(c < nctx, nlat + c, c - nctx)

    def bchunk(c):
        return jnp.where(c < nctx, nlat + nctx - 1 - c, nlat - 1 - (c - nctx))

    def pad_heads(v):
        return jnp.pad(v, ((0, 0), (0, LANE - SSM_HEADS))).reshape(2, 1, LANE)

    eh = np.zeros((2 * LANE, SSM_INNER), np.float32)
    for h in range(SSM_HEADS):
        eh[h, h * SSM_HEAD_DIM:(h + 1) * SSM_HEAD_DIM] = 1.0
        eh[LANE + h, h * SSM_HEAD_DIM:(h + 1) * SSM_HEAD_DIM] = 1.0
    return pl.pallas_call(
        _ssd_kernel,
        grid=(batch, nch),
        in_specs=[pl.BlockSpec((cl, SSM_CONV_DIM), lambda b, c: (b * nch + fchunk(c), 0)),
                  pl.BlockSpec((cl, SSM_CONV_DIM), lambda b, c: (b * nch + bchunk(c), 0)),
                  pl.BlockSpec((cl, LANE), lambda b, c: (b * nch + fchunk(c), 0)),
                  pl.BlockSpec((cl, LANE), lambda b, c: (b * nch + bchunk(c), 1)),
                  pl.BlockSpec((2, 1, LANE), lambda b, c: (0, 0, 0)),
                  pl.BlockSpec((2, 1, LANE), lambda b, c: (0, 0, 0)),
                  pl.BlockSpec((2 * LANE, SSM_INNER), lambda b, c: (0, 0))],
        out_specs=[pl.BlockSpec((cl, SSM_INNER), lambda b, c: (b * nch + fchunk(c), 0)),
                   pl.BlockSpec((cl, SSM_INNER), lambda b, c: (b * nch + bchunk(c), 0))],
        out_shape=[jax.ShapeDtypeStruct((u.shape[0], SSM_INNER), BF16)] * 2,
        scratch_shapes=[pltpu.VMEM((SSM_STATE, SSM_INNER), F32)] * 2,
        compiler_params=_params(("parallel", "arbitrary")),
        name="ssd_scan",
    )(u, u, dtr, dtr, pad_heads(a_log), pad_heads(dt_bias), jnp.asarray(eh, BF16))


def _ssm_out_kernel(yf_ref, yb_ref, x_ref, z_ref, skip_ref, nw_ref, o_ref):
    z = z_ref[...].astype(F32)
    y = (yf_ref[...] + yb_ref[...]).astype(F32)
    g = (y + skip_ref[...] * x_ref[...].astype(F32)) * _silu(z)
    gw = SSM_INNER // SSM_GROUPS
    for k in range(SSM_GROUPS):
        gk = g[:, k * gw:(k + 1) * gw]
        gk = gk * lax.rsqrt(jnp.mean(gk * gk, axis=-1, keepdims=True) + EPS)
        o_ref[:, k * gw:(k + 1) * gw] = (gk * nw_ref[:, k * gw:(k + 1) * gw]).astype(o_ref.dtype)


def _ssm_out(yf, yb, u, p, d_skip, ssm_norm, seq):
    t = u.shape[0]
    tm = _pick(seq, (384, 256, 128))
    w = SSM_INNER
    return pl.pallas_call(
        _ssm_out_kernel,
        grid=(t // tm,),
        in_specs=[pl.BlockSpec((tm, w), lambda i: (i, 0)),
                  pl.BlockSpec((tm, w), lambda i: (i, 0)),
                  pl.BlockSpec((tm, w), lambda i: (i, 0)),
                  pl.BlockSpec((tm, w), lambda i: (i, P_Z // w)),
                  pl.BlockSpec((1, w), lambda i: (0, 0)),
                  pl.BlockSpec((1, w), lambda i: (0, 0))],
        out_specs=pl.BlockSpec((tm, w), lambda i: (i, 0)),
        out_shape=jax.ShapeDtypeStruct((t, w), BF16),
        compiler_params=_params(("parallel",)),
        name="ssm_gated_norm",
    )(yf, yb, u, p, jnp.repeat(d_skip, SSM_HEAD_DIM).reshape(1, w), ssm_norm.reshape(1, w))


def _head_norm_rope_kernel(x_ref, nw_ref, cos_ref, sin_ref, o_ref, *, n_heads):
    for h in range(n_heads):
        sl = slice(h * LANE, (h + 1) * LANE)
        x = x_ref[:, sl].astype(F32)
        y = x * lax.rsqrt(jnp.mean(x * x, axis=-1, keepdims=True) + EPS) * nw_ref[...]
        o_ref[:, sl] = _rope(y, cos_ref[...], sin_ref[...]).astype(o_ref.dtype)


def _head_norm_rope(p, col0, n_heads, norm_w, cos, sin, seq):
    t = p.shape[0]
    tm = _pick(seq, (768, 512, 256))
    tpb = seq // tm
    w = n_heads * LANE
    return pl.pallas_call(
        functools.partial(_head_norm_rope_kernel, n_heads=n_heads),
        grid=(t // tm,),
        in_specs=[pl.BlockSpec((tm, w), lambda i: (i, col0 // w)),
                  pl.BlockSpec((1, LANE), lambda i: (0, 0)),
                  pl.BlockSpec((tm, LANE), lambda i: (i % tpb, 0)),
                  pl.BlockSpec((tm, LANE), lambda i: (i % tpb, 0))],
        out_specs=pl.BlockSpec((tm, w), lambda i: (i, 0)),
        out_shape=jax.ShapeDtypeStruct((t, w), BF16),
        compiler_params=_params(("parallel",)),
        name="head_norm_rope",
    )(p, norm_w.reshape(1, LANE), cos, sin)


def _mla_q_kernel(x_ref, nw_ref, w_ref, cos_ref, sin_ref, o_ref):
    x = x_ref[:, 0:MLA_Q_LORA].astype(F32)
    h = (x * lax.rsqrt(jnp.mean(x * x, axis=-1, keepdims=True) + EPS) * nw_ref[...]).astype(BF16)
    for hd in range(MLA_HEADS):
        c0 = hd * MLA_QK_PAD
        res = jnp.dot(h, w_ref[:, c0:c0 + MLA_QK_PAD], preferred_element_type=F32)
        o_ref[:, c0:c0 + LANE] = res[:, 0:LANE].astype(o_ref.dtype)
        o_ref[:, c0 + LANE:c0 + MLA_QK_PAD] = _rope(res[:, LANE:], cos_ref[...],
                                                    sin_ref[...]).astype(o_ref.dtype)


def _mla_q(p, q_norm, w_uq, cos, sin, seq):
    t = p.shape[0]
    tm = _pick(seq, (768, 512, 256))
    tpb = seq // tm
    kdim = MQA_PAD
    wo = MLA_HEADS * MLA_QK_PAD
    return pl.pallas_call(
        _mla_q_kernel,
        grid=(t // tm,),
        in_specs=[pl.BlockSpec((tm, kdim), lambda i: (i, P_MQA // kdim)),
                  pl.BlockSpec((1, MLA_Q_LORA), lambda i: (0, 0)),
                  pl.BlockSpec((MLA_Q_LORA, wo), lambda i: (0, 0)),
                  pl.BlockSpec((tm, LANE), lambda i: (i % tpb, 0)),
                  pl.BlockSpec((tm, LANE), lambda i: (i % tpb, 0))],
        out_specs=pl.BlockSpec((tm, wo), lambda i: (i, 0)),
        out_shape=jax.ShapeDtypeStruct((t, wo), BF16),
        compiler_params=_params(("parallel",)),
        name="mla_q_up",
    )(p, q_norm.reshape(1, MLA_Q_LORA), w_uq, cos, sin)


def _mla_kv_kernel(x_ref, kr_ref, nw_ref, w_ref, cos_ref, sin_ref, k_ref, v_ref):
    x = x_ref[...].astype(F32)
    h = (x * lax.rsqrt(jnp.mean(x * x, axis=-1, keepdims=True) + EPS) * nw_ref[...]).astype(BF16)
    k_rope = _rope(kr_ref[...].astype(F32), cos_ref[...], sin_ref[...]).astype(k_ref.dtype)
    for hd in range(MLA_HEADS):
        c0 = hd * MLA_QK_PAD
        res = jnp.dot(h, w_ref[:, c0:c0 + 2 * LANE], preferred_element_type=F32)
        k_ref[:, c0:c0 + LANE] = res[:, 0:LANE].astype(k_ref.dtype)
        k_ref[:, c0 + LANE:c0 + MLA_QK_PAD] = k_rope
        v_ref[:, hd * LANE:(hd + 1) * LANE] = res[:, LANE:].astype(v_ref.dtype)


def _mla_kv(p, kv_norm, w_ukv, cos, sin, seq):
    t = p.shape[0]
    tm = _pick(seq, (768, 512, 256))
    tpb = seq // tm
    kdim = MLA_KV_LORA
    wk = MLA_HEADS * MLA_QK_PAD
    return pl.pallas_call(
        _mla_kv_kernel,
        grid=(t // tm,),
        in_specs=[pl.BlockSpec((tm, kdim), lambda i: (i, P_MKVA // kdim)),
                  pl.BlockSpec((tm, LANE), lambda i: (i, P_MKR // LANE)),
                  pl.BlockSpec((1, kdim), lambda i: (0, 0)),
                  pl.BlockSpec((kdim, wk), lambda i: (0, 0)),
                  pl.BlockSpec((tm, LANE), lambda i: (i % tpb, 0)),
                  pl.BlockSpec((tm, LANE), lambda i: (i % tpb, 0))],
        out_specs=[pl.BlockSpec((tm, wk), lambda i: (i, 0)),
                   pl.BlockSpec((tm, MLA_HEADS * LANE), lambda i: (i, 0))],
        out_shape=[jax.ShapeDtypeStruct((t, wk), BF16),
                   jax.ShapeDtypeStruct((t, MLA_HEADS * LANE), BF16)],
        compiler_params=_params(("parallel",)),
        name="mla_kv_up",
    )(p, p, kv_norm.reshape(1, kdim), w_ukv, cos, sin)


def _flash_kernel(q_ref, k_ref, v_ref, g_ref, *rest, r, tq, dq, tk, n_keys, scale):
    o_ref = rest[-1]
    q = jnp.concatenate([q_ref[:, j * dq:(j + 1) * dq] for j in range(r)], axis=0)
    q = (q.astype(F32) * (scale * math.log2(math.e))).astype(BF16)
    m_rows = r * tq
    m = jnp.full((m_rows, 1), NEG_BIG, F32)
    l = jnp.zeros((m_rows, LANE), F32)
    acc = jnp.zeros((m_rows, LANE), F32)
    for c in range(n_keys // tk):
        k_c = k_ref[c * tk:(c + 1) * tk, :]
        v_c = v_ref[c * tk:(c + 1) * tk, :]
        s = lax.dot_general(q, k_c, (((1,), (1,)), ((), ())), preferred_element_type=F32)
        m_new = jnp.maximum(m, jnp.max(s, axis=-1, keepdims=True))
        alpha = jnp.exp2(m - m_new)
        pr = jnp.exp2(s - m_new)
        part = pr[:, 0:LANE]
        for j in range(1, tk // LANE):
            part = part + pr[:, j * LANE:(j + 1) * LANE]
        l = alpha * l + part
        acc = alpha * acc + jnp.dot(pr.astype(BF16), v_c, preferred_element_type=F32)
        m = m_new
    o = acc / jnp.sum(l, axis=-1, keepdims=True)
    for j in range(r):
        gate = _silu(g_ref[:, j * LANE:(j + 1) * LANE].astype(F32))
        o_ref[:, j * LANE:(j + 1) * LANE] = (o[j * tq:(j + 1) * tq] * gate).astype(o_ref.dtype)


def _flash(q, k, v, gates, *, q_col0, k_col0, v_col0, g_col0, v_stride, n_kv_heads, r, dq, scale,
           batch, n_lat, seq, ctx_only, tq, into=None):
    n_ctx = seq - n_lat
    tk = FLASH_TK
    if ctx_only:
        tq = n_ctx
        q_blk0, n_q, kv_rows, kv_blk0 = n_lat // tq, 1, n_ctx, n_lat // n_ctx
    else:
        q_blk0, n_q, kv_rows, kv_blk0 = 0, n_lat // tq, seq, 0
    assert kv_rows % tk == 0
    kern = functools.partial(_flash_kernel, r=r, tq=tq, dq=dq, tk=tk, n_keys=kv_rows, scale=scale)
    w_out = n_kv_heads * r * LANE
    args = [a.reshape(batch, seq, a.shape[-1]) for a in (q, k, v, gates)]
    in_specs = [pl.BlockSpec((None, tq, r * dq), lambda b, g, i: (b, q_blk0 + i, q_col0 // (r * dq) + g)),
                pl.BlockSpec((None, kv_rows, dq), lambda b, g, i: (b, kv_blk0, k_col0 // dq + g)),
                pl.BlockSpec((None, kv_rows, LANE),
                             lambda b, g, i: (b, kv_blk0, v_col0 // LANE + v_stride * g)),
                pl.BlockSpec((None, tq, r * LANE),
                             lambda b, g, i: (b, q_blk0 + i, g_col0 // (r * LANE) + g))]
    aliases = {}
    if into is not None:
        args.append(into.reshape(batch, seq, w_out))
        in_specs.append(pl.BlockSpec(memory_space=pl.ANY))
        aliases = {len(args) - 1: 0}
    out = pl.pallas_call(
        kern,
        grid=(batch, n_kv_heads, n_q),
        in_specs=in_specs,
        out_specs=pl.BlockSpec((None, tq, r * LANE), lambda b, g, i: (b, q_blk0 + i, g)),
        out_shape=jax.ShapeDtypeStruct((batch, seq, w_out), BF16),
        input_output_aliases=aliases,
        compiler_params=_params(("parallel", "parallel", "parallel")),
        name="flash_ctx" if ctx_only else "flash_latent",
    )(*args)
    return out.reshape(batch * seq, w_out)


def _attention(q, k, v, gates, tq, **kw):
    lat = _flash(q, k, v, gates, ctx_only=False, tq=tq, **kw)
    return _flash(q, k, v, gates, ctx_only=True, tq=tq, into=lat, **kw)


def _natten_kernel(q_ref, k0_ref, k1_ref, k2_ref, kc_ref, v0_ref, v1_ref, v2_ref, vc_ref, g_ref,
                   bias_ref, o_ref, *, scale, n_heads):
    n_loc = bias_ref.shape[-1]
    for h in range(n_heads):
        sl = slice(h * LANE, (h + 1) * LANE)
        q = (q_ref[:, sl].astype(F32) * (scale * math.log2(math.e))).astype(BF16)
        k = jnp.concatenate([k0_ref[:, sl], k1_ref[:, sl], k2_ref[:, sl], kc_ref[:, sl]], axis=0)
        v = jnp.concatenate([v0_ref[:, sl], v1_ref[:, sl], v2_ref[:, sl], vc_ref[:, sl]], axis=0)
        s = lax.dot_general(q, k, (((1,), (1,)), ((), ())), preferred_element_type=F32)
        s = jnp.concatenate([s[:, :n_loc] + bias_ref[h], s[:, n_loc:]], axis=1)
        pr = jnp.exp2(s - jnp.max(s, axis=-1, keepdims=True)).astype(BF16)
        ov = jnp.dot(pr, jnp.concatenate([v, jnp.ones_like(v)], axis=1), preferred_element_type=F32)
        o = ov[:, 0:LANE] / ov[:, LANE:]
        o_ref[:, sl] = (o * _silu(g_ref[:, sl].astype(F32))).astype(o_ref.dtype)


def _na_bias_table(rpb):
    qx = np.arange(GRID_W)
    c0 = np.clip(qx - NA_WIN_W // 2, 0, GRID_W - NA_WIN_W)
    col_ok = (qx[None, :] >= c0[:, None]) & (qx[None, :] < c0[:, None] + NA_WIN_W)
    dx = qx[None, :] - qx[:, None] + NA_WIN_W - 1
    pick = np.zeros((2 * NA_WIN_W - 1, GRID_W * GRID_W), np.float32)
    qi, ki = np.nonzero(col_ok)
    pick[dx[qi, ki], qi * GRID_W + ki] = 1.0
    by_dx = jnp.einsum("...d,dn->...n", rpb, pick, precision=lax.Precision.HIGHEST)
    qy, ky = np.arange(NA_QROWS), np.arange(NA_KROWS)
    dys, oks = [], []
    for q_off, first_key in ((0, None), (NA_WIN_H // 2, "q"), (NA_WIN_H, NA_KROWS - NA_WIN_H)):
        r0 = qy if first_key == "q" else np.full_like(qy, 0 if first_key is None else first_key)
        oks.append((ky[None, :] >= r0[:, None]) & (ky[None, :] < r0[:, None] + NA_WIN_H))
        dys.append(np.clip(ky[None, :] - (qy[:, None] + q_off) + NA_WIN_H - 1, 0, 2 * NA_WIN_H - 2))
    rows = jnp.take(by_dx, np.stack(dys).reshape(-1), axis=-2)
    lead = rows.shape[:-2]
    nl = len(lead)
    rows = rows.reshape(lead + (3, NA_QROWS, NA_KROWS, GRID_W, GRID_W))
    rows = rows.transpose(tuple(range(nl)) + (nl, nl + 1, nl + 3, nl + 2, nl + 4))
    ok = np.stack(oks)[:, :, None, :, None] & col_ok[None, None, :, None, :]
    tab = jnp.where(jnp.asarray(ok), rows * math.log2(math.e), NEG_BIG)
    return tab.reshape(lead + (3, NA_QROWS * GRID_W, NA_KROWS * GRID_W)).astype(BF16)


def _natten(p, bias, batch, n_lat, seq):
    n_ctx = seq - n_lat
    tq = NA_QROWS * GRID_W
    n_blk = n_lat // tq
    assert n_blk >= 3 and n_lat % tq == 0 and n_ctx % tq == 0
    rb = seq // tq
    cb = seq // n_ctx
    nk = NA_KROWS * GRID_W
    hps = NA_HEADS_PER_STEP
    hw = hps * LANE

    def kblk(i):
        return jnp.clip(i - 1, 0, n_blk - 3)

    def kspec(col0, j):
        return pl.BlockSpec((tq, hw), lambda b, h, i: (b * rb + kblk(i) + j, col0 // hw + h))

    def cspec(col0):
        return pl.BlockSpec((n_ctx, hw), lambda b, h, i: (b * cb + n_lat // n_ctx, col0 // hw + h))

    def qspec(col0):
        return pl.BlockSpec((tq, hw), lambda b, h, i: (b * rb + i, col0 // hw + h))

    def btype(i):
        return jnp.where(i == 0, 0, jnp.where(i == n_blk - 1, 2, 1))

    return pl.pallas_call(
        functools.partial(_natten_kernel, scale=HEAD_DIM ** -0.5, n_heads=hps),
        grid=(batch, NA_HEADS // hps, n_blk),
        in_specs=[qspec(P_NQ), kspec(P_NK, 0), kspec(P_NK, 1), kspec(P_NK, 2), cspec(P_NK),
                  kspec(P_NV, 0), kspec(P_NV, 1), kspec(P_NV, 2), cspec(P_NV), qspec(P_NG),
                  pl.BlockSpec((hps, None, tq, nk), lambda b, h, i: (h, btype(i), 0, 0))],
        out_specs=pl.BlockSpec((tq, hw), lambda b, h, i: (b * rb + i, h)),
        out_shape=jax.ShapeDtypeStruct((p.shape[0], NA_HEADS * LANE), BF16),
        compiler_params=_params(("parallel", "parallel", "parallel")),
        name="natten_latent",
    )(p, p, p, p, p, p, p, p, p, p, bias)


def _merge_kernel(o0_ref, o1_ref, o2_ref, o3_ref, w0_ref, w1_ref, w2_ref, w3_ref,
                  m0_ref, m1_ref, m2_ref, m3_ref, y_ref):
    acc = None
    for o_ref, w_ref, m_ref in ((o0_ref, w0_ref, m0_ref), (o1_ref, w1_ref, m1_ref),
                                (o2_ref, w2_ref, m2_ref), (o3_ref, w3_ref, m3_ref)):
        gate = _sigmoid(m_ref[...].astype(F32))
        term = gate * jnp.dot(o_ref[...], w_ref[...], preferred_element_type=F32)
        acc = term if acc is None else acc + term
    y_ref[...] = acc.astype(y_ref.dtype)


def _merge(outs, weights, p, seq):
    t = p.shape[0]
    d = weights[0].shape[1]
    tm = _pick(seq, (1056, 768, 512, 256))
    tn = 512
    o_specs =[pl.BlockSpec((tm, o.shape[1]), lambda i, j: (i, 0)) for o in outs]
    w_specs = [pl.BlockSpec((w.shape[0], tn), lambda i, j: (0, j)) for w in weights]
    m_specs = [pl.BlockSpec((tm, tn), lambda i, j, b=b: (i, (P_MIX + b * d) // tn + j))
               for b in range(N_BRANCH)]
    return pl.pallas_call(
        _merge_kernel,
        grid=(t // tm, d // tn),
        in_specs=o_specs + w_specs + m_specs,
        out_specs=pl.BlockSpec((tm, tn), lambda i, j: (i, j)),
        out_shape=jax.ShapeDtypeStruct((t, d), BF16),
        compiler_params=_params(("parallel", "parallel")),
        name="branch_merge",
    )(*outs, *weights, p, p, p, p)


def _out_kernel(y_ref, w_ref, x_ref, nw_ref, gate_ref, o_ref, *, tm, tiles_per_batch, n_lat):
    z = jnp.dot(y_ref[...], w_ref[...], preferred_element_type=F32)
    zn = z * lax.rsqrt(jnp.mean(z * z, axis=-1, keepdims=True) + EPS) * nw_ref[...]
    if tiles_per_batch is None:
        gate = gate_ref[0:1, :]
    else:
        row = (pl.program_id(0) % tiles_per_batch) * tm + lax.broadcasted_iota(jnp.int32, (tm, 1), 0)
        gate = jnp.where(row >= n_lat, gate_ref[1:2, :], gate_ref[0:1, :])
    o_ref[...] = x_ref[...] + gate * zn


def _out_projection_latent(y, w_out, xa, norm_w, gate, batch, n_lat, seq):
    d = xa.shape[1]
    tm = _pick(n_lat, (512, 256, 128))
    kern = functools.partial(_out_kernel, tm=tm, tiles_per_batch=None, n_lat=n_lat)
    return pl.pallas_call(
        kern,
        grid=(batch, n_lat // tm),
        in_specs=[pl.BlockSpec((None, tm, d), lambda b, i: (b, i, 0)),
                  pl.BlockSpec((d, d), lambda b, i: (0, 0)),
                  pl.BlockSpec((None, tm, d), lambda b, i: (b, i, 0)),
                  pl.BlockSpec((1, d), lambda b, i: (0, 0)),
                  pl.BlockSpec((None, 2, d), lambda b, i: (b, 0, 0))],
        out_specs=pl.BlockSpec((None, tm, d), lambda b, i: (b, i, 0)),
        out_shape=jax.ShapeDtypeStruct((batch, n_lat, d), F32),
        compiler_params=_params(("parallel", "parallel")),
        name="out_proj_last",
    )(y.reshape(batch, seq, d), w_out, xa.reshape(batch, seq, d), norm_w.reshape(1, d), gate)


def _out_projection(y, w_out, xa, norm_w, gate, n_lat, seq):
    t, d = xa.shape
    tm = _pick(seq, (768, 384, 256, 128))
    tpb = seq // tm
    kern = functools.partial(_out_kernel, tm=tm, tiles_per_batch=tpb, n_lat=n_lat)
    return pl.pallas_call(
        kern,
        grid=(t // tm,),
        in_specs=[pl.BlockSpec((tm, d), lambda i: (i, 0)),
                  pl.BlockSpec((d, d), lambda i: (0, 0)),
                  pl.BlockSpec((tm, d), lambda i: (i, 0)),
                  pl.BlockSpec((1, d), lambda i: (0, 0)),
                  pl.BlockSpec((None, 2, d), lambda i: (i // tpb, 0, 0))],
        out_specs=pl.BlockSpec((tm, d), lambda i: (i, 0)),
        out_shape=jax.ShapeDtypeStruct((t, d), F32),
        compiler_params=_params(("parallel",)),
        name="out_proj",
    )(y, w_out, xa, norm_w.reshape(1, d), gate)


def _pairs_apart(w, n_heads, dim):
    lead = w.shape[:-1]
    return w.reshape(lead + (n_heads, dim // 2, 2)).swapaxes(-1, -2).reshape(lead + (n_heads * dim,))


def _rope_tile(w):
    lead = w.shape[:-1]
    pr = w.reshape(lead + (MLA_ROPE // 2, 2))
    zero = jnp.zeros(lead + (MLA_ROPE // 2,), w.dtype)
    return jnp.concatenate([pr[..., 0], zero, pr[..., 1], zero], axis=-1)


def _layout_w_in(w):
    o = _OFF
    k = w.shape[0]

    def seg(name, width):
        return w[:, o[name]:o[name] + width]

    cols = [seg("mix", 8192), seg("z", 2048 + SSM_CONV_DIM),
            _pairs_apart(seg("gq", 1024), GQA_HEADS, HEAD_DIM), seg("gg", 1024),
            seg("nq", 4096), seg("mg", 1024),
            seg("mqa", MLA_Q_LORA), _rope_tile(seg("mkr", MLA_ROPE)),
            jnp.zeros((k, MQA_PAD - MLA_Q_LORA - LANE), w.dtype),
            _pairs_apart(seg("gk", 512), GQA_KV_HEADS, HEAD_DIM), seg("gv", 512),
            seg("mkva", MLA_KV_LORA)]
    main = jnp.concatenate(cols, axis=1).astype(BF16)
    assert main.shape[1] == P_WIDTH
    dtr = seg("dtr", 2 * SSM_HEADS)
    zero = jnp.zeros((k, LANE - SSM_HEADS), w.dtype)
    side = jnp.concatenate([dtr[:, :SSM_HEADS], zero, dtr[:, SSM_HEADS:], zero], axis=1).astype(BF16)
    return main, side


def _layout_w_uq(w_uq):
    k = w_uq.shape[0]
    w = w_uq.reshape(k, MLA_HEADS, MLA_NOPE + MLA_ROPE)
    w = jnp.concatenate([w[..., :MLA_NOPE], _rope_tile(w[..., MLA_NOPE:])], axis=-1)
    return w.reshape(k, MLA_HEADS * MLA_QK_PAD).astype(BF16)


def _rope_tables(n_lat, n_ctx, dim):
    t = np.arange(n_lat)
    quarter = dim // 4
    freqs = ROPE_THETA ** (-jnp.arange(quarter, dtype=F32) / quarter)
    row = jnp.asarray(t // GRID_W, F32)
    col = jnp.asarray(t % GRID_W, F32)
    ang = jnp.concatenate([row[:, None] * freqs, col[:, None] * freqs], axis=-1)
    cos, sin = jnp.cos(ang), jnp.sin(ang)
    pad = 64 - dim // 2
    one, zero = jnp.ones((n_lat, pad), F32), jnp.zeros((n_lat, pad), F32)
    cos_t = jnp.concatenate([cos, one, cos, one], axis=-1)
    sin_t = jnp.concatenate([-sin, zero, sin, zero], axis=-1)
    cos_t = jnp.concatenate([cos_t, jnp.ones((n_ctx, LANE), F32)], axis=0)
    sin_t = jnp.concatenate([sin_t, jnp.zeros((n_ctx, LANE), F32)], axis=0)
    return cos_t, sin_t


def _layer(xa, mod, rope_g, rope_m, lp, batch, n_lat, seq, last):
    d = xa.shape[1]

    def per_row(v):
        return jnp.stack([v[:batch], jnp.broadcast_to(v[batch:batch + 1], (batch, d))], axis=1)

    shift, scale, gate = (per_row(mod[:, k * d:(k + 1) * d]) for k in range(3))
    w_main, w_side = _layout_w_in(lp["w_in"])
    p, dtr = _in_projection(xa, lp["norm_pre"], scale, shift, w_main, w_side, n_lat, seq)

    u = _conv_silu(p, lp["conv_w"], lp["conv_b"], n_lat, seq)
    yf, yb = _ssd(u, dtr, lp["a_log"], lp["dt_bias"], batch, n_lat, seq)
    o_ssm = _ssm_out(yf, yb, u, p, lp["d_skip"], lp["ssm_norm"], seq)

    common = dict(batch=batch, n_lat=n_lat, seq=seq)
    qg = _head_norm_rope(p, P_GQ, GQA_HEADS, _pairs_apart(lp["gqa_q_norm"], 1, HEAD_DIM), *rope_g, seq)
    kg = _head_norm_rope(p, P_GK, GQA_KV_HEADS, _pairs_apart(lp["gqa_k_norm"], 1, HEAD_DIM), *rope_g, seq)
    o_gqa = _attention(
        qg, kg, p, p, GQA_TQ, q_col0=0, k_col0=0, v_col0=P_GV, g_col0=P_GG, v_stride=1,
        n_kv_heads=GQA_KV_HEADS, r=GQA_HEADS // GQA_KV_HEADS, dq=HEAD_DIM, scale=HEAD_DIM ** -0.5,
        **common)

    na_lat = _natten(p, lp["na_bias"], **common)
    o_na = _flash(p, p, p, p, q_col0=P_NQ, k_col0=P_NK, v_col0=P_NV, g_col0=P_NG, v_stride=1,
                  n_kv_heads=NA_HEADS, r=1, dq=HEAD_DIM, scale=HEAD_DIM ** -0.5, ctx_only=True,
                  tq=256, into=na_lat, **common)

    qm = _mla_q(p, lp["mla_q_norm"], _layout_w_uq(lp["w_uq"]), *rope_m, seq)
    km, vm = _mla_kv(p, lp["mla_kv_norm"], lp["w_ukv"].astype(BF16), *rope_m, seq)
    o_mla = _attention(
        qm, km, vm, p, MLA_TQ, q_col0=0, k_col0=0, v_col0=0, g_col0=P_MG, v_stride=1,
        n_kv_heads=MLA_HEADS, r=1, dq=MLA_QK_PAD, scale=(MLA_NOPE + MLA_ROPE) ** -0.5,
        **common)

    weights = [lp[n].astype(BF16) for n in ("w_o_ssm", "w_o_gqa", "w_o_na", "w_o_mla")]
    ymix = _merge([o_ssm, o_gqa, o_na, o_mla], weights, p, seq)
    w_out = lp["w_out"].astype(BF16)
    if last:
        return _out_projection_latent(ymix, w_out, xa, lp["norm_post"], gate, batch, n_lat, seq)
    return _out_projection(ymix, w_out, xa, lp["norm_post"], gate, n_lat, seq)


def kernel(x, c, ctx, c_ctx, ada_w, ada_b, norm_pre, norm_post, w_in, conv_w, conv_b, a_log, dt_bias,
           d_skip, ssm_norm, w_o_ssm, gqa_q_norm, gqa_k_norm, w_o_gqa, na_rpb, w_o_na, mla_q_norm,
           w_uq, mla_kv_norm, w_ukv, w_o_mla, w_out):
    batch, n_lat, d = x.shape
    n_ctx = ctx.shape[1]
    seq = n_lat + n_ctx
    stacked = dict(norm_pre=norm_pre, norm_post=norm_post, w_in=w_in,
                   conv_w=conv_w, conv_b=conv_b, a_log=a_log, dt_bias=dt_bias, d_skip=d_skip,
                   ssm_norm=ssm_norm, w_o_ssm=w_o_ssm, gqa_q_norm=gqa_q_norm, gqa_k_norm=gqa_k_norm,
                   w_o_gqa=w_o_gqa, na_bias=_na_bias_table(na_rpb), w_o_na=w_o_na,
                   mla_q_norm=mla_q_norm, w_uq=w_uq,
                   mla_kv_norm=mla_kv_norm, w_ukv=w_ukv, w_o_mla=w_o_mla, w_out=w_out)
    xa = jnp.concatenate([x, ctx], axis=1).reshape(batch * seq, d)
    cc = jnp.concatenate([c, c_ctx[None, :], jnp.zeros((8 - batch - 1, d), c.dtype)], axis=0)
    rope_g = _rope_tables(n_lat, n_ctx, HEAD_DIM)
    rope_m = _rope_tables(n_lat, n_ctx, MLA_ROPE)
    depth = ada_w.shape[0]
    mods = _modulation(cc, ada_w, ada_b)
    for layer in range(depth):
        lp = {k: v[layer] for k, v in stacked.items()}
        xa = _layer(xa, mods[layer], rope_g, rope_m, lp, batch, n_lat, seq, last=layer == depth - 1)
    return xa
```

```python
import functools
import math

import jax
import jax.numpy as jnp
import numpy as np
from jax import lax
from jax.experimental import pallas as pl
from jax.experimental.pallas import tpu as pltpu

F32 = jnp.float32
BF16 = jnp.bfloat16

GRID_W = 64
EPS = 1e-6
ROPE_THETA = 10000.0

SSM_HEADS = 32
SSM_HEAD_DIM = 64
SSM_INNER = SSM_HEADS * SSM_HEAD_DIM
SSM_GROUPS = 4
SSM_STATE = 128
SSM_CONV = 5
SSM_CHUNK = 128
SSM_CONV_DIM = SSM_INNER + 2 * SSM_GROUPS * SSM_STATE

GQA_HEADS = 8
GQA_KV_HEADS = 4
HEAD_DIM = 128
NA_HEADS = 8
NA_WIN_H = 8
NA_WIN_W = 16
NA_QROWS = 4
NA_KROWS = NA_QROWS + NA_WIN_H
NA_HEADS_PER_STEP = 8

MLA_HEADS = 8
MLA_Q_LORA = 768
MLA_KV_LORA = 512
MLA_NOPE = 128
MLA_ROPE = 64
MLA_QK_PAD = 256

N_BRANCH = 4
LANE = 128
VMEM_LIMIT = 56 * 1024 * 1024
NEG_BIG = -1e30
FLASH_TK = 256
GQA_TQ = 512
MLA_TQ = 1024

_SIZES = (SSM_INNER, SSM_CONV_DIM, 2 * SSM_HEADS, 1024, 512, 512, 1024, 1024, 1024, 1024, 1024,
          MLA_Q_LORA, MLA_KV_LORA, MLA_ROPE, 1024, N_BRANCH * 2048)
_OFF = dict(zip(("z", "xbc", "dtr", "gq", "gk", "gv", "gg", "nq", "nk", "nv", "ng",
                 "mqa", "mkva", "mkr", "mg", "mix"), np.cumsum((0,) + _SIZES[:-1]).tolist()))

P_MIX, P_Z, P_XBC = 0, 8192, 10240
P_GQ, P_GG, P_NQ, P_NK, P_NV, P_NG, P_MG, P_MQA = (13312, 14336, 15360, 16384, 17408, 18432,
                                                    19456, 20480)
P_GK, P_GV, P_MKVA = 21504, 22016, 22528
P_WIDTH = 23040
MQA_PAD = 1024
P_MKR = P_MQA + MLA_Q_LORA


def _pick(n, candidates):
    for c in candidates:
        if n % c == 0:
            return c
    raise ValueError(f"no tile for {n} among {candidates}")


def _params(sem):
    return pltpu.CompilerParams(dimension_semantics=sem, vmem_limit_bytes=VMEM_LIMIT)


def _sigmoid(v):
    return 0.5 + 0.5 * jnp.tanh(0.5 * v)


def _silu(v):
    return v * _sigmoid(v)


def _rope(v, cos, sin):
    return v * cos + pltpu.roll(v, 64, axis=1) * sin


def _mod_kernel(c_ref, w_ref, b_ref, o_ref):
    h = _silu(c_ref[...]).astype(BF16)
    o_ref[...] = jnp.dot(h, w_ref[...].astype(BF16), preferred_element_type=F32) + b_ref[...]


def _modulation(cc, ada_w, ada_b):
    rows, d = cc.shape
    depth, _, n = ada_w.shape
    tn = _pick(n, (512, 256, 128))
    return pl.pallas_call(
        _mod_kernel,
        grid=(depth, n // tn),
        in_specs=[pl.BlockSpec((rows, d), lambda l, j: (0, 0)),
                  pl.BlockSpec((None, d, tn), lambda l, j: (l, 0, j)),
                  pl.BlockSpec((None, 1, tn), lambda l, j: (l, 0, j))],
        out_specs=pl.BlockSpec((None, rows, tn), lambda l, j: (l, 0, j)),
        out_shape=jax.ShapeDtypeStruct((depth, rows, n), F32),
        compiler_params=_params(("parallel", "parallel")),
        name="adaln_mod",
    )(cc, ada_w, ada_b.reshape(depth, 1, n))


def _inproj_kernel(x_ref, nw_ref, sc_ref, sh_ref, w_ref, ws_ref, o_ref, os_ref, h_ref, *,
                   tm, tiles_per_batch, n_lat):
    i = pl.program_id(0)
    n_chunk = 3 if tm % 48 == 0 else 1
    rc = tm // n_chunk

    @pl.when(pl.program_id(1) == 0)
    def _():
        row0 = (i % tiles_per_batch) * tm
        for c in range(n_chunk):
            rows = slice(c * rc, (c + 1) * rc)
            x = x_ref[rows, :]
            y = x * lax.rsqrt(jnp.mean(x * x, axis=-1, keepdims=True) + EPS) * nw_ref[...]
            is_ctx = row0 + c * rc + lax.broadcasted_iota(jnp.int32, (rc, 1), 0) >= n_lat
            sc = jnp.where(is_ctx, sc_ref[1:2, :], sc_ref[0:1, :])
            sh = jnp.where(is_ctx, sh_ref[1:2, :], sh_ref[0:1, :])
            h = (y * (1.0 + sc) + sh).astype(BF16)
            h_ref[rows, :] = h
            os_ref[rows, :] = jnp.dot(h, ws_ref[...], preferred_element_type=F32)
            o_ref[rows, :] = jnp.dot(h, w_ref[...], preferred_element_type=F32).astype(o_ref.dtype)

    @pl.when(pl.program_id(1) != 0)
    def _():
        o_ref[...] = jnp.dot(h_ref[...], w_ref[...], preferred_element_type=F32).astype(o_ref.dtype)


def _in_projection(xa, norm_w, scale, shift, w_p, w_side, n_lat, seq):
    t, d = xa.shape
    tm = _pick(seq, (1056, 768, 512, 256))
    tn = _pick(P_WIDTH, (1536, 1024, 512))
    tpb = seq // tm
    kern = functools.partial(_inproj_kernel, tm=tm, tiles_per_batch=tpb, n_lat=n_lat)
    return pl.pallas_call(
        kern,
        grid=(t // tm, P_WIDTH // tn),
        in_specs=[pl.BlockSpec((tm, d), lambda i, j: (i, 0)),
                  pl.BlockSpec((1, d), lambda i, j: (0, 0)),
                  pl.BlockSpec((None, 2, d), lambda i, j: (i // tpb, 0, 0)),
                  pl.BlockSpec((None, 2, d), lambda i, j: (i // tpb, 0, 0)),
                  pl.BlockSpec((d, tn), lambda i, j: (0, j)),
                  pl.BlockSpec((d, 2 * LANE), lambda i, j: (0, 0))],
        out_specs=[pl.BlockSpec((tm, tn), lambda i, j: (i, j)),
                   pl.BlockSpec((tm, 2 * LANE), lambda i, j: (i, 0))],
        out_shape=[jax.ShapeDtypeStruct((t, P_WIDTH), BF16),
                   jax.ShapeDtypeStruct((t, 2 * LANE), F32)],
        scratch_shapes=[pltpu.VMEM((tm, d), BF16)],
        compiler_params=_params(("parallel", "arbitrary")),
        name="in_proj",
    )(xa, norm_w.reshape(1, d), scale, shift, w_p, w_side)


def _conv_kernel(u_ref, prev_ref, next_ref, w_ref, b_ref, s_ref, o_ref, *,
                 tm, tiles_per_batch, lat_tiles):
    ib = pl.program_id(0) % tiles_per_batch
    first = jnp.logical_or(ib == 0, ib == lat_tiles)
    last = jnp.logical_or(ib == lat_tiles - 1, ib == tiles_per_batch - 1)
    half = SSM_CONV // 2
    row = lax.broadcasted_iota(jnp.int32, (8, 1), 0)
    cw = 2 * LANE
    for c0 in range(0, u_ref.shape[1], cw):
        sl = slice(c0, c0 + cw)
        u = u_ref[:, sl]
        acc = b_ref[:, sl] + w_ref[half:half + 1, sl] * u.astype(F32)
        for idx, k in enumerate(k for k in range(SSM_CONV) if k != half):
            acc = acc + w_ref[k:k + 1, sl] * jnp.dot(s_ref[idx], u, preferred_element_type=F32)
        prev = jnp.where(first, 0.0, prev_ref[:, sl].astype(F32))
        nxt = jnp.where(last, 0.0, next_ref[:, sl].astype(F32))
        top = jnp.zeros_like(prev)
        bot = jnp.zeros_like(nxt)
        for k in range(half):
            reach = half - k
            top = top + w_ref[k:k + 1, sl] * jnp.where(row < reach, pltpu.roll(prev, reach, axis=0), 0.0)
            kk = SSM_CONV - 1 - k
            bot = bot + w_ref[kk:kk + 1, sl] * jnp.where(row >= 8 - reach,
                                                         pltpu.roll(nxt, 8 - reach, axis=0), 0.0)
        y = jnp.concatenate([acc[0:8] + top, acc[8:tm - 8], acc[tm - 8:tm] + bot], axis=0)
        o_ref[:, sl] = _silu(y).astype(o_ref.dtype)


def _conv_silu(p, conv_w, conv_b, n_lat, seq):
    t = p.shape[0]
    tm = 256
    tc = 1024
    tpb = seq // tm
    n_row8 = t // 8
    col0 = P_XBC // tc
    kern = functools.partial(_conv_kernel, tm=tm, tiles_per_batch=tpb, lat_tiles=n_lat // tm)
    half = SSM_CONV // 2
    shifts = np.stack([np.eye(tm, k=k - half, dtype=np.float32) for k in range(SSM_CONV) if k != half])
    return pl.pallas_call(
        kern,
        grid=(t // tm, SSM_CONV_DIM // tc),
        in_specs=[pl.BlockSpec((tm, tc), lambda i, j: (i, col0 + j)),
                  pl.BlockSpec((8, tc), lambda i, j: (jnp.maximum(i * (tm // 8) - 1, 0), col0 + j)),
                  pl.BlockSpec((8, tc), lambda i, j: (jnp.minimum((i + 1) * (tm // 8), n_row8 - 1),
                                                      col0 + j)),
                  pl.BlockSpec((8, tc), lambda i, j: (0, j)),
                  pl.BlockSpec((1, tc), lambda i, j: (0, j)),
                  pl.BlockSpec((SSM_CONV - 1, tm, tm), lambda i, j: (0, 0, 0))],
        out_specs=pl.BlockSpec((tm, tc), lambda i, j: (i, j)),
        out_shape=jax.ShapeDtypeStruct((t, SSM_CONV_DIM), BF16),
        compiler_params=_params(("parallel", "parallel")),
        name="ssm_conv",
    )(p, p, p, jnp.pad(conv_w, ((0, 8 - SSM_CONV), (0, 0))), conv_b.reshape(1, -1),
      jnp.asarray(shifts, BF16))


def _split_dot(a_bf16, v):
    v1 = v.astype(BF16)
    r1 = v - v1.astype(F32)
    v2 = r1.astype(BF16)
    v3 = (r1 - v2.astype(F32)).astype(BF16)
    return jnp.dot(jnp.concatenate([a_bf16, a_bf16, a_bf16], axis=1),
                   jnp.concatenate([v1, v2, v3], axis=0), preferred_element_type=F32)


def _ssd_kernel(uf_ref, ub_ref, dtf_ref, dtb_ref, alog_ref, bias_ref, eh_ref, yf_ref, yb_ref,
                stf_ref, stb_ref):
    @pl.when(pl.program_id(1) == 0)
    def _():
        stf_ref[...] = jnp.zeros_like(stf_ref)
        stb_ref[...] = jnp.zeros_like(stb_ref)

    _ssd_chunk(True, uf_ref, dtf_ref, alog_ref[0], bias_ref[0], eh_ref, yf_ref, stf_ref)
    _ssd_chunk(False, ub_ref, dtb_ref, alog_ref[1], bias_ref[1], eh_ref, yb_ref, stb_ref)


def _ssd_chunk(fwd, u_ref, dtr_ref, a_log, dt_bias, eh_ref, y_ref, st_ref):
    cl = SSM_CHUNK
    raw = dtr_ref[...] + dt_bias
    dt = jnp.maximum(raw, 0.0) + jnp.log(1.0 + jnp.exp(-jnp.abs(raw)))
    da = dt * (-jnp.exp(a_log))
    r = lax.broadcasted_iota(jnp.int32, (cl, cl), 0)
    c = lax.broadcasted_iota(jnp.int32, (cl, cl), 1)
    tri = (r >= c) if fwd else (r <= c)
    acs = _split_dot(jnp.where(tri, 1.0, 0.0).astype(BF16), da)
    total = acs[cl - 1:cl, :] if fwd else acs[0:1, :]
    e_acs = jnp.exp(acs)
    w_end = dt * jnp.exp(total - acs)
    acs_t = acs.T
    dt_t = dt.T

    def hi_lo(v):
        hi = v.astype(BF16)
        return jnp.concatenate([hi, (v - hi.astype(F32)).astype(BF16)], axis=1)

    ex = jnp.dot(jnp.concatenate([hi_lo(w_end), hi_lo(e_acs)], axis=0), eh_ref[...],
                 preferred_element_type=F32)
    w_exp = ex[0:cl]
    e_exp = ex[cl:2 * cl]
    dec = e_exp[cl - 1:cl, :] if fwd else e_exp[0:1, :]
    xw =(u_ref[:, 0:SSM_INNER].astype(F32) * w_exp).astype(BF16)
    lane = lax.broadcasted_iota(jnp.int32, (cl, LANE), 1)
    gw = SSM_INNER // SSM_GROUPS
    hpg = SSM_HEADS // SSM_GROUPS
    for g in range(SSM_GROUPS):
        b_g = u_ref[:, SSM_INNER + g * SSM_STATE:SSM_INNER + (g + 1) * SSM_STATE]
        c_off = SSM_INNER + SSM_GROUPS * SSM_STATE
        c_g = u_ref[:, c_off + g * SSM_STATE:c_off + (g + 1) * SSM_STATE]
        cb = lax.dot_general(c_g, b_g, (((1,), (1,)), ((), ())), preferred_element_type=F32)
        st_g = st_ref[:, g * gw:(g + 1) * gw]
        y_off = jnp.dot(c_g, st_g.astype(BF16), preferred_element_type=F32)
        ys = []
        for k in range(hpg // 2):
            ms = []
            for h in (g * hpg + 2 * k, g * hpg + 2 * k + 1):
                seg = acs[:, h:h + 1] - acs_t[h:h + 1, :]
                dec_h = jnp.exp(jnp.where(tri, seg, NEG_BIG))
                ms.append((dec_h * cb * dt_t[h:h + 1, :]).astype(BF16))
            x_p = u_ref[:, g * gw + k * LANE:g * gw + (k + 1) * LANE]
            zero = jnp.zeros_like(x_p)
            rhs = jnp.concatenate([jnp.where(lane < SSM_HEAD_DIM, x_p, zero),
                                   jnp.where(lane >= SSM_HEAD_DIM, x_p, zero)], axis=0)
            ys.append(jnp.dot(jnp.concatenate(ms, axis=1), rhs, preferred_element_type=F32))
        y_g = jnp.concatenate(ys, axis=1) + y_off * e_exp[:, g * gw:(g + 1) * gw]
        y_ref[:, g * gw:(g + 1) * gw] = y_g.astype(y_ref.dtype)
        upd = lax.dot_general(b_g, xw[:, g * gw:(g + 1) * gw], (((0,), (0,)), ((), ())),
                              preferred_element_type=F32)
        st_ref[:, g * gw:(g + 1) * gw] = st_g * dec[:, g * gw:(g + 1) * gw] + upd


def _ssd(u, dtr, a_log, dt_bias, batch, n_lat, seq):
    cl = SSM_CHUNK
    nch, nlat, nctx = seq // cl, n_lat // cl, (seq - n_lat) // cl

    def fchunk(c):
        return jnp.where(c < nctx, nlat + c, c - nctx)

    def bchunk(c):
        return jnp.where(c < nctx, nlat + nctx - 1 - c, nlat - 1 - (c - nctx))

    def pad_heads(v):
        return jnp.pad(v, ((0, 0), (0, LANE - SSM_HEADS))).reshape(2, 1, LANE)

    eh = np.zeros((2 * LANE, SSM_INNER), np.float32)
    for h in range(SSM_HEADS):
        eh[h, h * SSM_HEAD_DIM:(h + 1) * SSM_HEAD_DIM] = 1.0
        eh[LANE + h, h * SSM_HEAD_DIM:(h + 1) * SSM_HEAD_DIM] = 1.0
    return pl.pallas_call(
        _ssd_kernel,
        grid=(batch, nch),
        in_specs=[pl.BlockSpec((cl, SSM_CONV_DIM), lambda b, c: (b * nch + fchunk(c), 0)),
                  pl.BlockSpec((cl, SSM_CONV_DIM), lambda b, c: (b * nch + bchunk(c), 0)),
                  pl.BlockSpec((cl, LANE), lambda b, c: (b * nch + fchunk(c), 0)),
                  pl.BlockSpec((cl, LANE), lambda b, c: (b * nch + bchunk(c), 1)),
                  pl.BlockSpec((2, 1, LANE), lambda b, c: (0, 0, 0)),
                  pl.BlockSpec((2, 1, LANE), lambda b, c: (0, 0, 0)),
                  pl.BlockSpec((2 * LANE, SSM_INNER), lambda b, c: (0, 0))],
        out_specs=[pl.BlockSpec((cl, SSM_INNER), lambda b, c: (b * nch + fchunk(c), 0)),
                   pl.BlockSpec((cl, SSM_INNER), lambda b, c: (b * nch + bchunk(c), 0))],
        out_shape=[jax.ShapeDtypeStruct((u.shape[0], SSM_INNER), BF16)] * 2,
        scratch_shapes=[pltpu.VMEM((SSM_STATE, SSM_INNER), F32)] * 2,
        compiler_params=_params(("parallel", "arbitrary")),
        name="ssd_scan",
    )(u, u, dtr, dtr, pad_heads(a_log), pad_heads(dt_bias), jnp.asarray(eh, BF16))


def _ssm_out_kernel(yf_ref, yb_ref, x_ref, z_ref, skip_ref, nw_ref, o_ref):
    z = z_ref[...].astype(F32)
    y = (yf_ref[...] + yb_ref[...]).astype(F32)
    g = (y + skip_ref[...] * x_ref[...].astype(F32)) * _silu(z)
    gw = SSM_INNER // SSM_GROUPS
    for k in range(SSM_GROUPS):
        gk = g[:, k * gw:(k + 1) * gw]
        gk = gk * lax.rsqrt(jnp.mean(gk * gk, axis=-1, keepdims=True) + EPS)
        o_ref[:, k * gw:(k + 1) * gw] = (gk * nw_ref[:, k * gw:(k + 1) * gw]).astype(o_ref.dtype)


def _ssm_out(yf, yb, u, p, d_skip, ssm_norm, seq):
    t = u.shape[0]
    tm = _pick(seq, (384, 256, 128))
    w = SSM_INNER
    return pl.pallas_call(
        _ssm_out_kernel,
        grid=(t // tm,),
        in_specs=[pl.BlockSpec((tm, w), lambda i: (i, 0)),
                  pl.BlockSpec((tm, w), lambda i: (i, 0)),
                  pl.BlockSpec((tm, w), lambda i: (i, 0)),
                  pl.BlockSpec((tm, w), lambda i: (i, P_Z // w)),
                  pl.BlockSpec((1, w), lambda i: (0, 0)),
                  pl.BlockSpec((1, w), lambda i: (0, 0))],
        out_specs=pl.BlockSpec((tm, w), lambda i: (i, 0)),
        out_shape=jax.ShapeDtypeStruct((t, w), BF16),
        compiler_params=_params(("parallel",)),
        name="ssm_gated_norm",
    )(yf, yb, u, p, jnp.repeat(d_skip, SSM_HEAD_DIM).reshape(1, w), ssm_norm.reshape(1, w))


def _head_norm_rope_kernel(x_ref, nw_ref, cos_ref, sin_ref, o_ref, *, n_heads):
    for h in range(n_heads):
        sl = slice(h * LANE, (h + 1) * LANE)
        x = x_ref[:, sl].astype(F32)
        y = x * lax.rsqrt(jnp.mean(x * x, axis=-1, keepdims=True) + EPS) * nw_ref[...]
        o_ref[:, sl] = _rope(y, cos_ref[...], sin_ref[...]).astype(o_ref.dtype)


def _head_norm_rope(p, col0, n_heads, norm_w, cos, sin, seq):
    t = p.shape[0]
    tm = _pick(seq, (768, 512, 256))
    tpb = seq // tm
    w = n_heads * LANE
    return pl.pallas_call(
        functools.partial(_head_norm_rope_kernel, n_heads=n_heads),
        grid=(t // tm,),
        in_specs=[pl.BlockSpec((tm, w), lambda i: (i, col0 // w)),
                  pl.BlockSpec((1, LANE), lambda i: (0, 0)),
                  pl.BlockSpec((tm, LANE), lambda i: (i % tpb, 0)),
                  pl.BlockSpec((tm, LANE), lambda i: (i % tpb, 0))],
        out_specs=pl.BlockSpec((tm, w), lambda i: (i, 0)),
        out_shape=jax.ShapeDtypeStruct((t, w), BF16),
        compiler_params=_params(("parallel",)),
        name="head_norm_rope",
    )(p, norm_w.reshape(1, LANE), cos, sin)


def _mla_q_kernel(x_ref, nw_ref, w_ref, cos_ref, sin_ref, o_ref):
    x = x_ref[:, 0:MLA_Q_LORA].astype(F32)
    h = (x * lax.rsqrt(jnp.mean(x * x, axis=-1, keepdims=True) + EPS) * nw_ref[...]).astype(BF16)
    for hd in range(MLA_HEADS):
        c0 = hd * MLA_QK_PAD
        res = jnp.dot(h, w_ref[:, c0:c0 + MLA_QK_PAD], preferred_element_type=F32)
        o_ref[:, c0:c0 + LANE] = res[:, 0:LANE].astype(o_ref.dtype)
        o_ref[:, c0 + LANE:c0 + MLA_QK_PAD] = _rope(res[:, LANE:], cos_ref[...],
                                                    sin_ref[...]).astype(o_ref.dtype)


def _mla_q(p, q_norm, w_uq, cos, sin, seq):
    t = p.shape[0]
    tm = _pick(seq, (768, 512, 256))
    tpb = seq // tm
    kdim = MQA_PAD
    wo = MLA_HEADS * MLA_QK_PAD
    return pl.pallas_call(
        _mla_q_kernel,
        grid=(t // tm,),
        in_specs=[pl.BlockSpec((tm, kdim), lambda i: (i, P_MQA // kdim)),
                  pl.BlockSpec((1, MLA_Q_LORA), lambda i: (0, 0)),
                  pl.BlockSpec((MLA_Q_LORA, wo), lambda i: (0, 0)),
                  pl.BlockSpec((tm, LANE), lambda i: (i % tpb, 0)),
                  pl.BlockSpec((tm, LANE), lambda i: (i % tpb, 0))],
        out_specs=pl.BlockSpec((tm, wo), lambda i: (i, 0)),
        out_shape=jax.ShapeDtypeStruct((t, wo), BF16),
        compiler_params=_params(("parallel",)),
        name="mla_q_up",
    )(p, q_norm.reshape(1, MLA_Q_LORA), w_uq, cos, sin)


def _mla_kv_kernel(x_ref, kr_ref, nw_ref, w_ref, cos_ref, sin_ref, k_ref, v_ref):
    x = x_ref[...].astype(F32)
    h = (x * lax.rsqrt(jnp.mean(x * x, axis=-1, keepdims=True) + EPS) * nw_ref[...]).astype(BF16)
    k_rope = _rope(kr_ref[...].astype(F32), cos_ref[...], sin_ref[...]).astype(k_ref.dtype)
    for hd in range(MLA_HEADS):
        c0 = hd * MLA_QK_PAD
        res = jnp.dot(h, w_ref[:, c0:c0 + 2 * LANE], preferred_element_type=F32)
        k_ref[:, c0:c0 + LANE] = res[:, 0:LANE].astype(k_ref.dtype)
        k_ref[:, c0 + LANE:c0 + MLA_QK_PAD] = k_rope
        v_ref[:, hd * LANE:(hd + 1) * LANE] = res[:, LANE:].astype(v_ref.dtype)


def _mla_kv(p, kv_norm, w_ukv, cos, sin, seq):
    t = p.shape[0]
    tm = _pick(seq, (768, 512, 256))
    tpb = seq // tm
    kdim = MLA_KV_LORA
    wk = MLA_HEADS * MLA_QK_PAD
    return pl.pallas_call(
        _mla_kv_kernel,
        grid=(t // tm,),
        in_specs=[pl.BlockSpec((tm, kdim), lambda i: (i, P_MKVA // kdim)),
                  pl.BlockSpec((tm, LANE), lambda i: (i, P_MKR // LANE)),
                  pl.BlockSpec((1, kdim), lambda i: (0, 0)),
                  pl.BlockSpec((kdim, wk), lambda i: (0, 0)),
                  pl.BlockSpec((tm, LANE), lambda i: (i % tpb, 0)),
                  pl.BlockSpec((tm, LANE), lambda i: (i % tpb, 0))],
        out_specs=[pl.BlockSpec((tm, wk), lambda i: (i, 0)),
                   pl.BlockSpec((tm, MLA_HEADS * LANE), lambda i: (i, 0))],
        out_shape=[jax.ShapeDtypeStruct((t, wk), BF16),
                   jax.ShapeDtypeStruct((t, MLA_HEADS * LANE), BF16)],
        compiler_params=_params(("parallel",)),
        name="mla_kv_up",
    )(p, p, kv_norm.reshape(1, kdim), w_ukv, cos, sin)


def _flash_kernel(q_ref, k_ref, v_ref, g_ref, *rest, r, tq, dq, tk, n_keys, scale):
    o_ref = rest[-1]
    q = jnp.concatenate([q_ref[:, j * dq:(j + 1) * dq] for j in range(r)], axis=0)
    q = (q.astype(F32) * (scale * math.log2(math.e))).astype(BF16)
    m_rows = r * tq
    m = jnp.full((m_rows, 1), NEG_BIG, F32)
    l = jnp.zeros((m_rows, LANE), F32)
    acc = jnp.zeros((m_rows, LANE), F32)
    for c in range(n_keys // tk):
        k_c = k_ref[c * tk:(c + 1) * tk, :]
        v_c = v_ref[c * tk:(c + 1) * tk, :]
        s = lax.dot_general(q, k_c, (((1,), (1,)), ((), ())), preferred_element_type=F32)
        m_new = jnp.maximum(m, jnp.max(s, axis=-1, keepdims=True))
        alpha = jnp.exp2(m - m_new)
        pr = jnp.exp2(s - m_new)
        part = pr[:, 0:LANE]
        for j in range(1, tk // LANE):
            part = part + pr[:, j * LANE:(j + 1) * LANE]
        l = alpha * l + part
        acc = alpha * acc + jnp.dot(pr.astype(BF16), v_c, preferred_element_type=F32)
        m = m_new
    o = acc / jnp.sum(l, axis=-1, keepdims=True)
    for j in range(r):
        gate = _silu(g_ref[:, j * LANE:(j + 1) * LANE].astype(F32))
        o_ref[:, j * LANE:(j + 1) * LANE] = (o[j * tq:(j + 1) * tq] * gate).astype(o_ref.dtype)


def _flash(q, k, v, gates, *, q_col0, k_col0, v_col0, g_col0, v_stride, n_kv_heads, r, dq, scale,
           batch, n_lat, seq, ctx_only, tq, into=None):
    n_ctx = seq - n_lat
    tk = FLASH_TK
    if ctx_only:
        tq = n_ctx
        q_blk0, n_q, kv_rows, kv_blk0 = n_lat // tq, 1, n_ctx, n_lat // n_ctx
    else:
        q_blk0, n_q, kv_rows, kv_blk0 = 0, n_lat // tq, seq, 0
    assert kv_rows % tk == 0
    kern = functools.partial(_flash_kernel, r=r, tq=tq, dq=dq, tk=tk, n_keys=kv_rows, scale=scale)
    w_out = n_kv_heads * r * LANE
    args = [a.reshape(batch, seq, a.shape[-1]) for a in (q, k, v, gates)]
    in_specs = [pl.BlockSpec((None, tq, r * dq), lambda b, g, i: (b, q_blk0 + i, q_col0 // (r * dq) + g)),
                pl.BlockSpec((None, kv_rows, dq), lambda b, g, i: (b, kv_blk0, k_col0 // dq + g)),
                pl.BlockSpec((None, kv_rows, LANE),
                             lambda b, g, i: (b, kv_blk0, v_col0 // LANE + v_stride * g)),
                pl.BlockSpec((None, tq, r * LANE),
                             lambda b, g, i: (b, q_blk0 + i, g_col0 // (r * LANE) + g))]
    aliases = {}
    if into is not None:
        args.append(into.reshape(batch, seq, w_out))
        in_specs.append(pl.BlockSpec(memory_space=pl.ANY))
        aliases = {len(args) - 1: 0}
    out = pl.pallas_call(
        kern,
        grid=(batch, n_kv_heads, n_q),
        in_specs=in_specs,
        out_specs=pl.BlockSpec((None, tq, r * LANE), lambda b, g, i: (b, q_blk0 + i, g)),
        out_shape=jax.ShapeDtypeStruct((batch, seq, w_out), BF16),
        input_output_aliases=aliases,
        compiler_params=_params(("parallel", "parallel", "parallel")),
        name="flash_ctx" if ctx_only else "flash_latent",
    )(*args)
    return out.reshape(batch * seq, w_out)


def _attention(q, k, v, gates, tq, **kw):
    lat = _flash(q, k, v, gates, ctx_only=False, tq=tq, **kw)
    return _flash(q, k, v, gates, ctx_only=True, tq=tq, into=lat, **kw)


def _natten_kernel(q_ref, k0_ref, k1_ref, k2_ref, kc_ref, v0_ref, v1_ref, v2_ref, vc_ref, g_ref,
                   bias_ref, o_ref, *, scale, n_heads):
    n_loc = bias_ref.shape[-1]
    for h in range(n_heads):
        sl = slice(h * LANE, (h + 1) * LANE)
        q = (q_ref[:, sl].astype(F32) * (scale * math.log2(math.e))).astype(BF16)
        k = jnp.concatenate([k0_ref[:, sl], k1_ref[:, sl], k2_ref[:, sl], kc_ref[:, sl]], axis=0)
        v = jnp.concatenate([v0_ref[:, sl], v1_ref[:, sl], v2_ref[:, sl], vc_ref[:, sl]], axis=0)
        s = lax.dot_general(q, k, (((1,), (1,)), ((), ())), preferred_element_type=F32)
        s = jnp.concatenate([s[:, :n_loc] + bias_ref[h], s[:, n_loc:]], axis=1)
        pr = jnp.exp2(s - jnp.max(s, axis=-1, keepdims=True)).astype(BF16)
        ov = jnp.dot(pr, jnp.concatenate([v, jnp.ones_like(v)], axis=1), preferred_element_type=F32)
        o = ov[:, 0:LANE] / ov[:, LANE:]
        o_ref[:, sl] = (o * _silu(g_ref[:, sl].astype(F32))).astype(o_ref.dtype)


def _na_bias_table(rpb):
    qx = np.arange(GRID_W)
    c0 = np.clip(qx - NA_WIN_W // 2, 0, GRID_W - NA_WIN_W)
    col_ok = (qx[None, :] >= c0[:, None]) & (qx[None, :] < c0[:, None] + NA_WIN_W)
    dx = qx[None, :] - qx[:, None] + NA_WIN_W - 1
    pick = np.zeros((2 * NA_WIN_W - 1, GRID_W * GRID_W), np.float32)
    qi, ki = np.nonzero(col_ok)
    pick[dx[qi, ki], qi * GRID_W + ki] = 1.0
    by_dx = jnp.einsum("...d,dn->...n", rpb, pick, precision=lax.Precision.HIGHEST)
    qy, ky = np.arange(NA_QROWS), np.arange(NA_KROWS)
    dys, oks = [], []
    for q_off, first_key in ((0, None), (NA_WIN_H // 2, "q"), (NA_WIN_H, NA_KROWS - NA_WIN_H)):
        r0 = qy if first_key == "q" else np.full_like(qy, 0 if first_key is None else first_key)
        oks.append((ky[None, :] >= r0[:, None]) & (ky[None, :] < r0[:, None] + NA_WIN_H))
        dys.append(np.clip(ky[None, :] - (qy[:, None] + q_off) + NA_WIN_H - 1, 0, 2 * NA_WIN_H - 2))
    rows = jnp.take(by_dx, np.stack(dys).reshape(-1), axis=-2)
    lead = rows.shape[:-2]
    nl = len(lead)
    rows = rows.reshape(lead + (3, NA_QROWS, NA_KROWS, GRID_W, GRID_W))
    rows = rows.transpose(tuple(range(nl)) + (nl, nl + 1, nl + 3, nl + 2, nl + 4))
    ok = np.stack(oks)[:, :, None, :, None] & col_ok[None, None, :, None, :]
    tab = jnp.where(jnp.asarray(ok), rows * math.log2(math.e), NEG_BIG)
    return tab.reshape(lead + (3, NA_QROWS * GRID_W, NA_KROWS * GRID_W)).astype(BF16)


def _natten(p, bias, batch, n_lat, seq):
    n_ctx = seq - n_lat
    tq = NA_QROWS * GRID_W
    n_blk = n_lat // tq
    assert n_blk >= 3 and n_lat % tq == 0 and n_ctx % tq == 0
    rb = seq // tq
    cb = seq // n_ctx
    nk = NA_KROWS * GRID_W
    hps = NA_HEADS_PER_STEP
    hw = hps * LANE

    def kblk(i):
        return jnp.clip(i - 1, 0, n_blk - 3)

    def kspec(col0, j):
        return pl.BlockSpec((tq, hw), lambda b, h, i: (b * rb + kblk(i) + j, col0 // hw + h))

    def cspec(col0):
        return pl.BlockSpec((n_ctx, hw), lambda b, h, i: (b * cb + n_lat // n_ctx, col0 // hw + h))

    def qspec(col0):
        return pl.BlockSpec((tq, hw), lambda b, h, i: (b * rb + i, col0 // hw + h))

    def btype(i):
        return jnp.where(i == 0, 0, jnp.where(i == n_blk - 1, 2, 1))

    return pl.pallas_call(
        functools.partial(_natten_kernel, scale=HEAD_DIM ** -0.5, n_heads=hps),
        grid=(batch, NA_HEADS // hps, n_blk),
        in_specs=[qspec(P_NQ), kspec(P_NK, 0), kspec(P_NK, 1), kspec(P_NK, 2), cspec(P_NK),
                  kspec(P_NV, 0), kspec(P_NV, 1), kspec(P_NV, 2), cspec(P_NV), qspec(P_NG),
                  pl.BlockSpec((hps, None, tq, nk), lambda b, h, i: (h, btype(i), 0, 0))],
        out_specs=pl.BlockSpec((tq, hw), lambda b, h, i: (b * rb + i, h)),
        out_shape=jax.ShapeDtypeStruct((p.shape[0], NA_HEADS * LANE), BF16),
        compiler_params=_params(("parallel", "parallel", "parallel")),
        name="natten_latent",
    )(p, p, p, p, p, p, p, p, p, p, bias)


def _merge_kernel(yf_ref, yb_ref, x_ref, z_ref, skip_ref, nw_ref, o1_ref, o2_ref, o3_ref,
                  w0_ref, w1_ref, w2_ref, w3_ref, m0_ref, m1_ref, m2_ref, m3_ref, y_ref, ossm_ref, *, tm):
    n_chunk = 3 if tm % 48 == 0 else 1
    rc = tm // n_chunk
    gw = SSM_INNER // SSM_GROUPS

    def mix(rows, o_ssm):
        acc = None
        for o, w_ref, m_ref in ((o_ssm, w0_ref, m0_ref), (o1_ref[rows, :], w1_ref, m1_ref),
                                (o2_ref[rows, :], w2_ref, m2_ref), (o3_ref[rows, :], w3_ref, m3_ref)):
            term = _sigmoid(m_ref[rows, :].astype(F32)) * jnp.dot(o, w_ref[...], preferred_element_type=F32)
            acc = term if acc is None else acc + term
        y_ref[rows, :] = acc.astype(y_ref.dtype)

    @pl.when(pl.program_id(1) == 0)
    def _():
        for c in range(n_chunk):
            rows = slice(c * rc, (c + 1) * rc)
            y = (yf_ref[rows, :] + yb_ref[rows, :]).astype(F32)
            g = (y + skip_ref[...] * x_ref[rows, :].astype(F32)) * _silu(z_ref[rows, :].astype(F32))
            parts = []
            for k in range(SSM_GROUPS):
                gk = g[:, k * gw:(k + 1) * gw]
                gk = gk * lax.rsqrt(jnp.mean(gk * gk, axis=-1, keepdims=True) + EPS)
                parts.append((gk * nw_ref[:, k * gw:(k + 1) * gw]).astype(BF16))
            o_ssm = jnp.concatenate(parts, axis=1)
            ossm_ref[rows, :] = o_ssm
            mix(rows, o_ssm)

    @pl.when(pl.program_id(1) != 0)
    def _():
        mix(slice(0, tm), ossm_ref[...])


def _merge(yf, yb, u, d_skip, ssm_norm, outs, weights, p, seq):
    t = p.shape[0]
    d = weights[0].shape[1]
    w = SSM_INNER
    tm = _pick(seq, (528, 512, 256))
    tn = 512
    row = lambda i, j: (i, 0)
    s_specs = [pl.BlockSpec((tm, w), row), pl.BlockSpec((tm, w), row), pl.BlockSpec((tm, w), row),
               pl.BlockSpec((tm, w), lambda i, j: (i, P_Z // w)),
               pl.BlockSpec((1, w), lambda i, j: (0, 0)), pl.BlockSpec((1, w), lambda i, j: (0, 0))]
    o_specs = [pl.BlockSpec((tm, o.shape[1]), row) for o in outs]
    w_specs = [pl.BlockSpec((wt.shape[0], tn), lambda i, j: (0, j)) for wt in weights]
    m_specs = [pl.BlockSpec((tm, tn), lambda i, j, b=b: (i, (P_MIX + b * d) // tn + j))
               for b in range(N_BRANCH)]
    return pl.pallas_call(
        functools.partial(_merge_kernel, tm=tm),
        grid=(t // tm, d // tn),
        in_specs=s_specs + o_specs + w_specs + m_specs,
        out_specs=pl.BlockSpec((tm, tn), lambda i, j: (i, j)),
        out_shape=jax.ShapeDtypeStruct((t, d), BF16),
        scratch_shapes=[pltpu.VMEM((tm, w), BF16)],
        compiler_params=_params(("parallel", "arbitrary")),
        name="branch_merge",
    )(yf, yb, u, p, jnp.repeat(d_skip, SSM_HEAD_DIM).reshape(1, w), ssm_norm.reshape(1, w),
      *outs, *weights, p, p, p, p)


def _out_kernel(y_ref, w_ref, x_ref, nw_ref, gate_ref, o_ref, *, tm, tiles_per_batch, n_lat):
    z = jnp.dot(y_ref[...], w_ref[...], preferred_element_type=F32)
    zn = z * lax.rsqrt(jnp.mean(z * z, axis=-1, keepdims=True) + EPS) * nw_ref[...]
    if tiles_per_batch is None:
        gate = gate_ref[0:1, :]
    else:
        row = (pl.program_id(0) % tiles_per_batch) * tm + lax.broadcasted_iota(jnp.int32, (tm, 1), 0)
        gate = jnp.where(row >= n_lat, gate_ref[1:2, :], gate_ref[0:1, :])
    o_ref[...] = x_ref[...] + gate * zn


def _out_projection_latent(y, w_out, xa, norm_w, gate, batch, n_lat, seq):
    d = xa.shape[1]
    tm = _pick(n_lat, (512, 256, 128))
    kern = functools.partial(_out_kernel, tm=tm, tiles_per_batch=None, n_lat=n_lat)
    return pl.pallas_call(
        kern,
        grid=(batch, n_lat // tm),
        in_specs=[pl.BlockSpec((None, tm, d), lambda b, i: (b, i, 0)),
                  pl.BlockSpec((d, d), lambda b, i: (0, 0)),
                  pl.BlockSpec((None, tm, d), lambda b, i: (b, i, 0)),
                  pl.BlockSpec((1, d), lambda b, i: (0, 0)),
                  pl.BlockSpec((None, 2, d), lambda b, i: (b, 0, 0))],
        out_specs=pl.BlockSpec((None, tm, d), lambda b, i: (b, i, 0)),
        out_shape=jax.ShapeDtypeStruct((batch, n_lat, d), F32),
        compiler_params=_params(("parallel", "parallel")),
        name="out_proj_last",
    )(y.reshape(batch, seq, d), w_out, xa.reshape(batch, seq, d), norm_w.reshape(1, d), gate)


def _out_projection(y, w_out, xa, norm_w, gate, n_lat, seq):
    t, d = xa.shape
    tm = _pick(seq, (768, 384, 256, 128))
    tpb = seq // tm
    kern = functools.partial(_out_kernel, tm=tm, tiles_per_batch=tpb, n_lat=n_lat)
    return pl.pallas_call(
        kern,
        grid=(t // tm,),
        in_specs=[pl.BlockSpec((tm, d), lambda i: (i, 0)),
                  pl.BlockSpec((d, d), lambda i: (0, 0)),
                  pl.BlockSpec((tm, d), lambda i: (i, 0)),
                  pl.BlockSpec((1, d), lambda i: (0, 0)),
                  pl.BlockSpec((None, 2, d), lambda i: (i // tpb, 0, 0))],
        out_specs=pl.BlockSpec((tm, d), lambda i: (i, 0)),
        out_shape=jax.ShapeDtypeStruct((t, d), F32),
        compiler_params=_params(("parallel",)),
        name="out_proj",
    )(y, w_out, xa, norm_w.reshape(1, d), gate)


def _pairs_apart(w, n_heads, dim):
    lead = w.shape[:-1]
    return w.reshape(lead + (n_heads, dim // 2, 2)).swapaxes(-1, -2).reshape(lead + (n_heads * dim,))


def _rope_tile(w):
    lead = w.shape[:-1]
    pr = w.reshape(lead + (MLA_ROPE // 2, 2))
    zero = jnp.zeros(lead + (MLA_ROPE // 2,), w.dtype)
    return jnp.concatenate([pr[..., 0], zero, pr[..., 1], zero], axis=-1)


def _layout_w_in(w):
    o = _OFF
    k = w.shape[0]

    def seg(name, width):
        return w[:, o[name]:o[name] + width]

    cols = [seg("mix", 8192), seg("z", 2048 + SSM_CONV_DIM),
            _pairs_apart(seg("gq", 1024), GQA_HEADS, HEAD_DIM), seg("gg", 1024),
            seg("nq", 4096), seg("mg", 1024),
            seg("mqa", MLA_Q_LORA), _rope_tile(seg("mkr", MLA_ROPE)),
            jnp.zeros((k, MQA_PAD - MLA_Q_LORA - LANE), w.dtype),
            _pairs_apart(seg("gk", 512), GQA_KV_HEADS, HEAD_DIM), seg("gv", 512),
            seg("mkva", MLA_KV_LORA)]
    main = jnp.concatenate(cols, axis=1).astype(BF16)
    assert main.shape[1] == P_WIDTH
    dtr = seg("dtr", 2 * SSM_HEADS)
    zero = jnp.zeros((k, LANE - SSM_HEADS), w.dtype)
    side = jnp.concatenate([dtr[:, :SSM_HEADS], zero, dtr[:, SSM_HEADS:], zero], axis=1).astype(BF16)
    return main, side


def _layout_w_uq(w_uq):
    k = w_uq.shape[0]
    w = w_uq.reshape(k, MLA_HEADS, MLA_NOPE + MLA_ROPE)
    w = jnp.concatenate([w[..., :MLA_NOPE], _rope_tile(w[..., MLA_NOPE:])], axis=-1)
    return w.reshape(k, MLA_HEADS * MLA_QK_PAD).astype(BF16)


def _rope_tables(n_lat, n_ctx, dim):
    t = np.arange(n_lat)
    quarter = dim // 4
    freqs = ROPE_THETA ** (-jnp.arange(quarter, dtype=F32) / quarter)
    row = jnp.asarray(t // GRID_W, F32)
    col = jnp.asarray(t % GRID_W, F32)
    ang = jnp.concatenate([row[:, None] * freqs, col[:, None] * freqs], axis=-1)
    cos, sin = jnp.cos(ang), jnp.sin(ang)
    pad = 64 - dim // 2
    one, zero = jnp.ones((n_lat, pad), F32), jnp.zeros((n_lat, pad), F32)
    cos_t = jnp.concatenate([cos, one, cos, one], axis=-1)
    sin_t = jnp.concatenate([-sin, zero, sin, zero], axis=-1)
    cos_t = jnp.concatenate([cos_t, jnp.ones((n_ctx, LANE), F32)], axis=0)
    sin_t = jnp.concatenate([sin_t, jnp.zeros((n_ctx, LANE), F32)], axis=0)
    return cos_t, sin_t


def _layer(xa, mod, rope_g, rope_m, lp, batch, n_lat, seq, last):
    d = xa.shape[1]

    def per_row(v):
        return jnp.stack([v[:batch], jnp.broadcast_to(v[batch:batch + 1], (batch, d))], axis=1)

    shift, scale, gate = (per_row(mod[:, k * d:(k + 1) * d]) for k in range(3))
    w_main, w_side = _layout_w_in(lp["w_in"])
    p, dtr = _in_projection(xa, lp["norm_pre"], scale, shift, w_main, w_side, n_lat, seq)

    u = _conv_silu(p, lp["conv_w"], lp["conv_b"], n_lat, seq)
    yf, yb = _ssd(u, dtr, lp["a_log"], lp["dt_bias"], batch, n_lat, seq)

    common = dict(batch=batch, n_lat=n_lat, seq=seq)
    qg = _head_norm_rope(p, P_GQ, GQA_HEADS, _pairs_apart(lp["gqa_q_norm"], 1, HEAD_DIM), *rope_g, seq)
    kg = _head_norm_rope(p, P_GK, GQA_KV_HEADS, _pairs_apart(lp["gqa_k_norm"], 1, HEAD_DIM), *rope_g, seq)
    o_gqa = _attention(
        qg, kg, p, p, GQA_TQ, q_col0=0, k_col0=0, v_col0=P_GV, g_col0=P_GG, v_stride=1,
        n_kv_heads=GQA_KV_HEADS, r=GQA_HEADS // GQA_KV_HEADS, dq=HEAD_DIM, scale=HEAD_DIM ** -0.5,
        **common)

    na_lat = _natten(p, lp["na_bias"], **common)
    o_na = _flash(p, p, p, p, q_col0=P_NQ, k_col0=P_NK, v_col0=P_NV, g_col0=P_NG, v_stride=1,
                  n_kv_heads=NA_HEADS, r=1, dq=HEAD_DIM, scale=HEAD_DIM ** -0.5, ctx_only=True,
                  tq=256, into=na_lat, **common)

    qm = _mla_q(p, lp["mla_q_norm"], _layout_w_uq(lp["w_uq"]), *rope_m, seq)
    km, vm = _mla_kv(p, lp["mla_kv_norm"], lp["w_ukv"].astype(BF16), *rope_m, seq)
    o_mla = _attention(
        qm, km, vm, p, MLA_TQ, q_col0=0, k_col0=0, v_col0=0, g_col0=P_MG, v_stride=1,
        n_kv_heads=MLA_HEADS, r=1, dq=MLA_QK_PAD, scale=(MLA_NOPE + MLA_ROPE) ** -0.5,
        **common)

    weights = [lp[n].astype(BF16) for n in ("w_o_ssm", "w_o_gqa", "w_o_na", "w_o_mla")]
    ymix = _merge(yf, yb, u, lp["d_skip"], lp["ssm_norm"], [o_gqa, o_na, o_mla], weights, p, seq)
    w_out = lp["w_out"].astype(BF16)
    if last:
        return _out_projection_latent(ymix, w_out, xa, lp["norm_post"], gate, batch, n_lat, seq)
    return _out_projection(ymix, w_out, xa, lp["norm_post"], gate, n_lat, seq)


def kernel(x, c, ctx, c_ctx, ada_w, ada_b, norm_pre, norm_post, w_in, conv_w, conv_b, a_log, dt_bias,
           d_skip, ssm_norm, w_o_ssm, gqa_q_norm, gqa_k_norm, w_o_gqa, na_rpb, w_o_na, mla_q_norm,
           w_uq, mla_kv_norm, w_ukv, w_o_mla, w_out):
    batch, n_lat, d = x.shape
    n_ctx = ctx.shape[1]
    seq = n_lat + n_ctx
    stacked = dict(norm_pre=norm_pre, norm_post=norm_post, w_in=w_in,
                   conv_w=conv_w, conv_b=conv_b, a_log=a_log, dt_bias=dt_bias, d_skip=d_skip,
                   ssm_norm=ssm_norm, w_o_ssm=w_o_ssm, gqa_q_norm=gqa_q_norm, gqa_k_norm=gqa_k_norm,
                   w_o_gqa=w_o_gqa, na_bias=_na_bias_table(na_rpb), w_o_na=w_o_na,
                   mla_q_norm=mla_q_norm, w_uq=w_uq,
                   mla_kv_norm=mla_kv_norm, w_ukv=w_ukv, w_o_mla=w_o_mla, w_out=w_out)
    xa = jnp.concatenate([x, ctx], axis=1).reshape(batch * seq, d)
    cc = jnp.concatenate([c, c_ctx[None, :], jnp.zeros((8 - batch - 1, d), c.dtype)], axis=0)
    rope_g = _rope_tables(n_lat, n_ctx, HEAD_DIM)
    rope_m = _rope_tables(n_lat, n_ctx, MLA_ROPE)
    depth = ada_w.shape[0]
    mods = _modulation(cc, ada_w, ada_b)
    for layer in range(depth):
        lp = {k: v[layer] for k, v in stacked.items()}
        xa = _layer(xa, mods[layer], rope_g, rope_m, lp, batch, n_lat, seq, last=layer == depth - 1)
    return xa
```
